```python
import math
import jax
import jax.numpy as jnp
from jax import lax
import numpy as np

D_MODEL = 1024
BATCH = 32
SEQ = 256
DEPTH = 4
DEC_BATCH = 4
DEC_SEQ = 1024
PAST_LEN = 512

GRID_W = 64
N_EVEN = (DEPTH + 1) // 2
N_ODD = DEPTH // 2
ATT_HEAD_DIM = 64
ATT_HEADS = D_MODEL // (2 * ATT_HEAD_DIM)
ATT_KV_HEADS = ATT_HEADS // 4
ATT_GROUP = ATT_HEADS // ATT_KV_HEADS
ATT_Q_DIM = ATT_HEADS * ATT_HEAD_DIM
ATT_KV_DIM = ATT_KV_HEADS * ATT_HEAD_DIM
DN_KEY_DIM = 128
DN_VAL_DIM = 128
DN_HEADS = D_MODEL // (2 * DN_VAL_DIM)
DN_QK_DIM = DN_HEADS * DN_KEY_DIM
DN_V_DIM = DN_HEADS * DN_VAL_DIM
SSM_INNER = 2 * D_MODEL
SSM_HEAD_DIM = 64
SSM_HEADS = SSM_INNER // SSM_HEAD_DIM
SSM_GROUPS = 8
SSM_STATE = 128
MLP_HIDDEN = 4 * D_MODEL
CONV_K = 3
CHUNK = 64
Q_BLOCK = 128
ROPE_THETA = 10000.0
NORM_EPS = 1e-6
DT_MIN = 0.001
DT_MAX = 0.1
EVEN_SPLITS = (ATT_Q_DIM, ATT_KV_DIM, ATT_KV_DIM, 2 * DN_QK_DIM + DN_V_DIM, DN_V_DIM, 2 * DN_HEADS, 2 * DN_HEADS)
ODD_SPLITS = (SSM_INNER, SSM_INNER + 2 * SSM_GROUPS * SSM_STATE, 2 * SSM_HEADS)
EVEN_IN = sum(EVEN_SPLITS)
ODD_IN = sum(ODD_SPLITS)
F32 = jnp.float32

kernel_name = 'hybrid_prefix_flow_gqa_gdn_ssd_step'


def _split(x, sizes):
    return jnp.split(x, [int(s) for s in np.cumsum(sizes)[:-1]], axis=-1)


def _flip(x):
    return jnp.flip(x, axis=1)


def _rms_norm(x, gain):
    xf = x.astype(F32)
    y = xf * lax.rsqrt(jnp.mean(xf * xf, axis=-1, keepdims=True) + NORM_EPS)
    return (y * gain.astype(F32)).astype(x.dtype)


def _l2_normalize(x):
    xf = x.astype(F32)
    return xf * lax.rsqrt(jnp.sum(xf * xf, axis=-1, keepdims=True) + NORM_EPS)


def _modulate(x, gain, shift, scale):
    return _rms_norm(x, gain) * (1 + scale) + shift


def _axial_rope_tables(n_tok):
    rows = n_tok // GRID_W
    t = jnp.arange(rows * GRID_W)
    row = (t // GRID_W).astype(F32)
    col = (t % GRID_W).astype(F32)
    axis_dim = ATT_HEAD_DIM // 2
    inv_freq = ROPE_THETA ** (-jnp.arange(0, axis_dim, 2, dtype=F32) / axis_dim)
    ang_row = row[:, None] * inv_freq
    ang_col = col[:, None] * inv_freq
    return jnp.cos(ang_row), jnp.sin(ang_row), jnp.cos(ang_col), jnp.sin(ang_col)


def _rotate(x, cos, sin):
    x1, x2 = jnp.split(x, 2, axis=-1)
    cos = cos[:, None, :]
    sin = sin[:, None, :]
    return jnp.concatenate([x1 * cos - x2 * sin, x2 * cos + x1 * sin], axis=-1)


def _apply_axial_rope(x, tables):
    cos_r, sin_r, cos_c, sin_c = tables
    x_row, x_col = jnp.split(x.astype(F32), 2, axis=-1)
    return jnp.concatenate([_rotate(x_row, cos_r, sin_r), _rotate(x_col, cos_c, sin_c)], axis=-1).astype(x.dtype)


def _centred_dwconv(x, w):
    pad = CONV_K // 2
    return lax.conv_general_dilated(
        x, w[:, None, :].astype(x.dtype), window_strides=(1,), padding=[(pad, pad)],
        dimension_numbers=('NWC', 'WIO', 'NWC'), feature_group_count=x.shape[-1])


def _blocked_attention(q, k, v):
    b, sq, kvh, grp, hd = q.shape
    n_blk = sq // Q_BLOCK
    q_blocks = jnp.moveaxis(q.reshape(b, n_blk, Q_BLOCK, kvh, grp, hd), 1, 0)
    scale = hd ** -0.5

    def attend(q_blk):
        s = jnp.einsum('bqhgd,bkhd->bhgqk', q_blk, k, preferred_element_type=F32) * scale
        p = jax.nn.softmax(s, axis=-1).astype(v.dtype)
        return jnp.einsum('bhgqk,bkhd->bqhgd', p, v)

    out = lax.map(attend, q_blocks)
    return jnp.moveaxis(out, 0, 1).reshape(b, sq, kvh, grp, hd)


def _gated_delta_chunked(q, k, v, g, beta, s0):
    b, t, h, dk = k.shape
    dv = v.shape[-1]
    n = t // CHUNK

    def to_chunks(x):
        return jnp.moveaxis(x.reshape(b, n, CHUNK, h, *x.shape[3:]), 3, 2)

    q, k, v, g, beta = (to_chunks(a) for a in (q, k, v, g, beta))
    gc = jnp.cumsum(g, axis=-1)
    causal = jnp.tril(jnp.ones((CHUNK, CHUNK), bool))
    strict = jnp.tril(jnp.ones((CHUNK, CHUNK), bool), -1)
    decay = jnp.exp(jnp.where(causal, gc[..., :, None] - gc[..., None, :], -jnp.inf))
    kb = k * beta[..., None]
    m = jnp.where(strict, jnp.einsum('bnhid,bnhjd->bnhij', kb, k) * decay, 0.0)
    eye = jnp.eye(CHUNK, dtype=F32)
    t_mat = lax.linalg.triangular_solve(eye + m, jnp.broadcast_to(eye, m.shape),
                                        left_side=True, lower=True, unit_diagonal=True)
    u = t_mat @ (v * beta[..., None])
    w = t_mat @ (kb * jnp.exp(gc)[..., None])
    qk = jnp.where(causal, jnp.einsum('bnhid,bnhjd->bnhij', q, k) * decay, 0.0)
    q_dec = q * jnp.exp(gc)[..., None]
    k_dec = k * jnp.exp(gc[..., -1:] - gc)[..., None]
    chunk_decay = jnp.exp(gc[..., -1])

    def step(s, inp):
        u_c, w_c, qk_c, qd_c, kd_c, dec_c = inp
        v_new = u_c - w_c @ s
        o_c = qd_c @ s + qk_c @ v_new
        s = s * dec_c[..., None, None] + jnp.einsum('bhcd,bhce->bhde', kd_c, v_new)
        return s, o_c

    xs = tuple(jnp.moveaxis(a, 1, 0) for a in (u, w, qk, q_dec, k_dec, chunk_decay))
    s_fin, o = lax.scan(step, s0, xs)
    o = jnp.moveaxis(jnp.moveaxis(o, 0, 1), 3, 2).reshape(b, t, h, dv)
    return o, s_fin


def _ssd_chunked(x, a, bm, cm, s0):
    b, t, nh, p = x.shape
    ng, ns = bm.shape[2:]
    r = nh // ng
    n = t // CHUNK
    x = x.reshape(b, n, CHUNK, ng, r, p)
    bm = bm.reshape(b, n, CHUNK, ng, ns)
    cm = cm.reshape(b, n, CHUNK, ng, ns)
    a = jnp.moveaxis(a.reshape(b, n, CHUNK, ng, r), 2, 4)
    acum = jnp.cumsum(a, axis=-1)
    causal = jnp.tril(jnp.ones((CHUNK, CHUNK), bool))
    lmat = jnp.exp(jnp.where(causal, acum[..., :, None] - acum[..., None, :], -jnp.inf))
    cb = jnp.einsum('bcige,bcjge->bcgij', cm, bm)
    y_diag = jnp.einsum('bcgrij,bcjgrp->bcigrp', cb[:, :, :, None] * lmat, x)
    decay_in = jnp.exp(acum[..., -1:] - acum)
    xd = x * jnp.moveaxis(decay_in, 4, 2)[..., None]
    chunk_states = jnp.einsum('bcjge,bcjgrp->bcgrpe', bm, xd)
    chunk_decay = jnp.exp(acum[..., -1])

    def step(s, inp):
        st, dec = inp
        return s * dec[..., None, None] + st, s

    s_fin, s_in = lax.scan(step, s0.reshape(b, ng, r, p, ns),
                           (jnp.moveaxis(chunk_states, 1, 0), jnp.moveaxis(chunk_decay, 1, 0)))
    s_in = jnp.moveaxis(s_in, 0, 1)
    y_off = jnp.einsum('bcige,bcgrpe->bcigrp', cm, s_in) * jnp.moveaxis(jnp.exp(acum), 4, 2)[..., None]
    return (y_diag + y_off).reshape(b, t, nh, p), s_fin.reshape(b, nh, p, ns)


def _even_mixer(h, w_in, q_gain, k_gain, conv_w, a_log, dt_bias, out_gain, w_out, ctx):
    b, n, _ = h.shape
    aq, ak, av, dqkv, dz, dbeta, dalpha = _split(h @ w_in, EVEN_SPLITS)
    q = _rms_norm(aq.reshape(b, n, ATT_HEADS, ATT_HEAD_DIM), q_gain)
    k = _rms_norm(ak.reshape(b, n, ATT_KV_HEADS, ATT_HEAD_DIM), k_gain)
    v = av.reshape(b, n, ATT_KV_HEADS, ATT_HEAD_DIM)
    if ctx is None:
        keys, vals = k, v
    else:
        ctx_k, ctx_v, ctx_s = ctx
        tables = _axial_rope_tables(n)
        q = _apply_axial_rope(q, tables)
        keys = jnp.concatenate([_apply_axial_rope(k, tables), ctx_k.astype(k.dtype)], axis=1)
        vals = jnp.concatenate([v, ctx_v.astype(v.dtype)], axis=1)
    o_att = _blocked_attention(q.reshape(b, n, ATT_KV_HEADS, ATT_GROUP, ATT_HEAD_DIM), keys, vals)
    o_att = o_att.reshape(b, n, ATT_Q_DIM)
    dqkv = jax.nn.silu(_centred_dwconv(dqkv, conv_w))
    dq, dk, dv = _split(dqkv, (DN_QK_DIM, DN_QK_DIM, DN_V_DIM))
    dq = _l2_normalize(dq.reshape(b, n, DN_HEADS, DN_KEY_DIM)) * (DN_KEY_DIM ** -0.5)
    dk = _l2_normalize(dk.reshape(b, n, DN_HEADS, DN_KEY_DIM))
    dv = dv.reshape(b, n, DN_HEADS, DN_VAL_DIM).astype(F32)
    beta = jax.nn.sigmoid(dbeta.reshape(b, n, 2, DN_HEADS).astype(F32))
    log_decay = -jnp.exp(a_log.astype(F32)) * jax.nn.softplus(
        dalpha.reshape(b, n, 2, DN_HEADS).astype(F32) + dt_bias.astype(F32))
    s0 = jnp.zeros((b, 2, DN_HEADS, DN_KEY_DIM, DN_VAL_DIM), F32) if ctx is None else ctx_s.astype(F32)
    o_f, s_f = _gated_delta_chunked(dq, dk, dv, log_decay[:, :, 0], beta[:, :, 0], s0[:, 0])
    o_b, s_b = _gated_delta_chunked(_flip(dq), _flip(dk), _flip(dv), _flip(log_decay[:, :, 1]),
                                    _flip(beta[:, :, 1]), s0[:, 1])
    o_dn = _rms_norm(o_f + _flip(o_b), out_gain) * jax.nn.silu(
        dz.reshape(b, n, DN_HEADS, DN_VAL_DIM).astype(F32))
    y = jnp.concatenate([o_att, o_dn.reshape(b, n, DN_V_DIM).astype(h.dtype)], axis=-1) @ w_out
    new_ctx = (k, v, jnp.stack([s_f, s_b], axis=1)) if ctx is None else None
    return y, new_ctx


def _odd_mixer(h, w_in, conv_w, conv_b, a_log, dt_bias, d_skip, out_gain, w_out, ctx):
    b, n, _ = h.shape
    z, xbc, dt_raw = _split(h @ w_in, ODD_SPLITS)
    xbc = jax.nn.silu(_centred_dwconv(xbc, conv_w) + conv_b)
    xs, bm, cm = _split(xbc.astype(F32), (SSM_INNER, SSM_GROUPS * SSM_STATE, SSM_GROUPS * SSM_STATE))
    xs = xs.reshape(b, n, SSM_HEADS, SSM_HEAD_DIM)
    bm = bm.reshape(b, n, SSM_GROUPS, SSM_STATE)
    cm = cm.reshape(b, n, SSM_GROUPS, SSM_STATE)
    dt = jax.nn.softplus(dt_raw.reshape(b, n, 2, SSM_HEADS).astype(F32) + dt_bias.astype(F32))
    a = -jnp.exp(a_log.astype(F32))
    s0 = jnp.zeros((b, 2, SSM_HEADS, SSM_HEAD_DIM, SSM_STATE), F32) if ctx is None else ctx.astype(F32)
    y_f, s_f = _ssd_chunked(xs * dt[:, :, 0, :, None], dt[:, :, 0] * a[0], bm, cm, s0[:, 0])
    y_b, s_b = _ssd_chunked(_flip(xs * dt[:, :, 1, :, None]), _flip(dt[:, :, 1] * a[1]),
                            _flip(bm), _flip(cm), s0[:, 1])
    y = y_f + _flip(y_b) + d_skip.astype(F32)[:, None] * xs
    y = _rms_norm(y.reshape(b, n, SSM_INNER) * jax.nn.silu(z.astype(F32)), out_gain)
    out = y.astype(h.dtype) @ w_out
    new_ctx = jnp.stack([s_f, s_b], axis=1) if ctx is None else None
    return out, new_ctx


def _sq_relu_mlp(h, w_in, w_out):
    a = jnp.maximum(h @ w_in, 0)
    return (a * a) @ w_out


def setup_inputs(seed: int = 0) -> dict:
    key = jax.random.key(seed)
    ks = iter(jax.random.split(key, 40))
    d = D_MODEL

    def nrm(shape, scale):
        return jax.random.normal(next(ks), shape, F32) * scale

    def gain(shape):
        return 1.0 + nrm(shape, 0.02)

    def a_log(shape):
        return jnp.log(jax.random.uniform(next(ks), shape, F32, 1.0, 16.0))

    def dt_bias(shape):
        dt = jnp.exp(jax.random.uniform(next(ks), shape, F32, math.log(DT_MIN), math.log(DT_MAX)))
        return dt + jnp.log(-jnp.expm1(-dt))

    return {
        'x_prompt': nrm((BATCH, SEQ, d), 1.0),
        'x_sample': nrm((DEC_BATCH, DEC_SEQ, d), 1.0),
        'c': nrm((DEC_BATCH, d), 1.0),
        'cache_attn_k': nrm((DEC_BATCH, N_EVEN, PAST_LEN, ATT_KV_HEADS, ATT_HEAD_DIM), 1.0),
        'cache_attn_v': nrm((DEC_BATCH, N_EVEN, PAST_LEN, ATT_KV_HEADS, ATT_HEAD_DIM), 1.0),
        'state_delta': nrm((DEC_BATCH, N_EVEN, 2, DN_HEADS, DN_KEY_DIM, DN_VAL_DIM), 0.1),
        'state_ssm': nrm((DEC_BATCH, N_ODD, 2, SSM_HEADS, SSM_HEAD_DIM, SSM_STATE), 0.1),
        'c_ctx': nrm((d,), 1.0),
        'norm_mix_g': gain((DEPTH, d)),
        'norm_mlp_g': gain((DEPTH, d)),
        'w_mod': nrm((DEPTH, d, 6 * d), 0.5 * d ** -0.5),
        'b_mod': nrm((DEPTH, 6 * d), 0.01),
        'w_mlp_in': nrm((DEPTH, d, MLP_HIDDEN), d ** -0.5),
        'w_mlp_out': nrm((DEPTH, MLP_HIDDEN, d), MLP_HIDDEN ** -0.5),
        'w_in_even': nrm((N_EVEN, d, EVEN_IN), d ** -0.5),
        'attn_q_norm_g': gain((N_EVEN, ATT_HEAD_DIM)),
        'attn_k_norm_g': gain((N_EVEN, ATT_HEAD_DIM)),
        'delta_conv_w': nrm((N_EVEN, CONV_K, 2 * DN_QK_DIM + DN_V_DIM), CONV_K ** -0.5),
        'delta_a_log': a_log((N_EVEN, 2, DN_HEADS)),
        'delta_dt_bias': dt_bias((N_EVEN, 2, DN_HEADS)),
        'delta_norm_g': gain((N_EVEN, DN_VAL_DIM)),
        'w_out_even': nrm((N_EVEN, ATT_Q_DIM + DN_V_DIM, d), (ATT_Q_DIM + DN_V_DIM) ** -0.5),
        'w_in_odd': nrm((N_ODD, d, ODD_IN), d ** -0.5),
        'ssm_conv_w': nrm((N_ODD, CONV_K, SSM_INNER + 2 * SSM_GROUPS * SSM_STATE), CONV_K ** -0.5),
        'ssm_conv_b': nrm((N_ODD, SSM_INNER + 2 * SSM_GROUPS * SSM_STATE), 0.01),
        'ssm_a_log': a_log((N_ODD, 2, SSM_HEADS)),
        'ssm_dt_bias': dt_bias((N_ODD, 2, SSM_HEADS)),
        'ssm_d': 1.0 + nrm((N_ODD, SSM_HEADS), 0.1),
        'ssm_norm_g': gain((N_ODD, SSM_INNER)),
        'w_out_odd': nrm((N_ODD, SSM_INNER, d), SSM_INNER ** -0.5),
        'final_norm_g': gain((d,)),
    }


def reference(x_prompt, x_sample, c, cache_attn_k, cache_attn_v, state_delta, state_ssm, c_ctx,
              norm_mix_g, norm_mlp_g, w_mod, b_mod, w_mlp_in, w_mlp_out,
              w_in_even, attn_q_norm_g, attn_k_norm_g, delta_conv_w, delta_a_log, delta_dt_bias,
              delta_norm_g, w_out_even,
              w_in_odd, ssm_conv_w, ssm_conv_b, ssm_a_log, ssm_dt_bias, ssm_d, ssm_norm_g, w_out_odd,
              final_norm_g):

    def run_trunk(x, cond, caches):
        ctx_path = caches is None
        ks, vs, sds, sss = [], [], [], []
        for layer in range(DEPTH):
            j = layer // 2
            mod = (jax.nn.silu(cond) @ w_mod[layer] + b_mod[layer])[:, None, :]
            sh1, sc1, g1, sh2, sc2, g2 = jnp.split(mod, 6, axis=-1)
            h = _modulate(x, norm_mix_g[layer], sh1, sc1)
            if layer % 2 == 0:
                ctx = None if ctx_path else (caches[0][:, j], caches[1][:, j], caches[2][:, j])
                y, st = _even_mixer(h, w_in_even[j], attn_q_norm_g[j], attn_k_norm_g[j], delta_conv_w[j],
                                    delta_a_log[j], delta_dt_bias[j], delta_norm_g[j], w_out_even[j], ctx)
                if ctx_path:
                    ks.append(st[0])
                    vs.append(st[1])
                    sds.append(st[2])
            else:
                ctx = None if ctx_path else caches[3][:, j]
                y, st = _odd_mixer(h, w_in_odd[j], ssm_conv_w[j], ssm_conv_b[j], ssm_a_log[j],
                                   ssm_dt_bias[j], ssm_d[j], ssm_norm_g[j], w_out_odd[j], ctx)
                if ctx_path:
                    sss.append(st)
            x = x + g1 * y
            h = _modulate(x, norm_mlp_g[layer], sh2, sc2)
            x = x + g2 * _sq_relu_mlp(h, w_mlp_in[layer], w_mlp_out[layer])
        return _rms_norm(x, final_norm_g), ks, vs, sds, sss

    y_prompt, ks, vs, sds, sss = run_trunk(x_prompt, c_ctx[None, :], None)
    out_dtype = x_prompt.dtype
    new_cache_attn_k = jnp.stack(ks, axis=1).astype(out_dtype)
    new_cache_attn_v = jnp.stack(vs, axis=1).astype(out_dtype)
    new_state_delta = jnp.stack(sds, axis=1).astype(out_dtype)
    new_state_ssm = jnp.stack(sss, axis=1).astype(out_dtype)
    y_sample, _, _, _, _ = run_trunk(x_sample, c, (cache_attn_k, cache_attn_v, state_delta, state_ssm))
    return (y_prompt, y_sample, new_cache_attn_k, new_cache_attn_v, new_state_delta, new_state_ssm)
```

```python
import functools
import math

import jax
import jax.numpy as jnp
import numpy as np
from jax import lax
from jax.experimental import pallas as pl
from jax.experimental.pallas import tpu as pltpu

F32 = jnp.float32
BF16 = jnp.bfloat16

D_MODEL = 1024
BATCH = 32
SEQ = 256
DEPTH = 4
DEC_BATCH = 4
DEC_SEQ = 1024
PAST_LEN = 512
GRID_W = 64
N_EVEN = (DEPTH + 1) // 2
N_ODD = DEPTH // 2
ATT_HEAD_DIM = 64
ATT_HEADS = 8
ATT_KV_HEADS = 2
ATT_Q_DIM = ATT_HEADS * ATT_HEAD_DIM
ATT_KV_DIM = ATT_KV_HEADS * ATT_HEAD_DIM
DN_KEY_DIM = 128
DN_VAL_DIM = 128
DN_HEADS = 4
DN_QK_DIM = DN_HEADS * DN_KEY_DIM
DN_V_DIM = DN_HEADS * DN_VAL_DIM
DN_CONV_DIM = 2 * DN_QK_DIM + DN_V_DIM
SSM_INNER = 2 * D_MODEL
SSM_HEAD_DIM = 64
SSM_HEADS = SSM_INNER // SSM_HEAD_DIM
SSM_GROUPS = 8
SSM_STATE = 128
SSM_BC_DIM = SSM_GROUPS * SSM_STATE
SSM_CONV_DIM = SSM_INNER + 2 * SSM_BC_DIM
MLP_HIDDEN = 4 * D_MODEL
CHUNK = 64
ROPE_THETA = 10000.0
NORM_EPS = 1e-6

LANES = 128
N_PROMPT_TOK = BATCH * SEQ
N_SAMPLE_TOK = DEC_BATCH * DEC_SEQ
N_TOK = N_PROMPT_TOK + N_SAMPLE_TOK
N_COND = 8
TOKEN_TILE = 512
N_PROMPT_TILES = N_PROMPT_TOK // TOKEN_TILE
TILES_PER_DEC_SEQ = DEC_SEQ // TOKEN_TILE
COL_CHUNK = 512
NEG_BIG = -1e30
VMEM_LIMIT = 56 * 1024 * 1024


def _cparams(n_grid):
    return pltpu.CompilerParams(dimension_semantics=("arbitrary",) * n_grid,
                                vmem_limit_bytes=VMEM_LIMIT)


def _silu(x):
    return x / (1.0 + jnp.exp(-x))


def _sigmoid(x):
    return 1.0 / (1.0 + jnp.exp(-x))


def _softplus(x):
    return jnp.maximum(x, 0.0) + jnp.log1p(jnp.exp(-jnp.abs(x)))


def _dot(a, b):
    return jnp.dot(a, b, preferred_element_type=F32)


def _dot_nt(a, b):
    return lax.dot_general(a, b, (((1,), (1,)), ((), ())), preferred_element_type=F32)


def _dot_tn(a, b):
    return lax.dot_general(a, b, (((0,), (0,)), ((), ())), preferred_element_type=F32)


def _split3(a):
    hi = a.astype(BF16)
    r = a - hi.astype(F32)
    mid = r.astype(BF16)
    lo = (r - mid.astype(F32)).astype(BF16)
    return hi, mid, lo


def _dot_xl(a, b_exact):
    hi, mid, lo = _split3(a)
    return _dot(hi, b_exact) + _dot(mid, b_exact) + _dot(lo, b_exact)


def _dot_xr(a_exact, b):
    hi, mid, lo = _split3(b)
    return _dot(a_exact, hi) + _dot(a_exact, mid) + _dot(a_exact, lo)


def _mod_norm(x, gain, shift, scale):
    y = x * lax.rsqrt(jnp.mean(x * x, axis=-1, keepdims=True) + NORM_EPS) * gain
    return y * (1.0 + scale) + shift


def _mod_row(i):
    return jnp.where(i < N_PROMPT_TILES, 0, 1 + (i - N_PROMPT_TILES) // TILES_PER_DEC_SEQ)


def _shr(i, k):
    return lax.shift_right_logical(i, jnp.int32(k))


def _mod_kernel(c_ref, w_ref, b_ref, o_ref):
    s = _silu(c_ref[...]).astype(BF16)
    o_ref[...] = _dot(s, w_ref[...].astype(BF16)) + b_ref[...]


def _modulation(cond, w_mod, b_mod):
    n_col = 6 * D_MODEL // D_MODEL
    return pl.pallas_call(
        _mod_kernel,
        grid=(DEPTH, n_col),
        in_specs=[
            pl.BlockSpec((N_COND, D_MODEL), lambda l, j: (0, 0)),
            pl.BlockSpec((None, D_MODEL, D_MODEL), lambda l, j: (l, 0, j)),
            pl.BlockSpec((None, 1, D_MODEL), lambda l, j: (l, 0, j)),
        ],
        out_specs=pl.BlockSpec((None, N_COND, D_MODEL), lambda l, j: (l, 0, j)),
        out_shape=jax.ShapeDtypeStruct((DEPTH, N_COND, 6 * D_MODEL), F32),
        compiler_params=_cparams(2),
        name="modulation",
    )(cond, w_mod, b_mod.reshape(DEPTH, 1, 6 * D_MODEL))


def _inproj_kernel(x_ref, gain_ref, mod_ref, w_ref, *out_refs, splits):
    m = mod_ref[...]
    h = _mod_norm(x_ref[...], gain_ref[...], m[:, 0:D_MODEL], m[:, D_MODEL:2 * D_MODEL]).astype(BF16)
    for o_ref, (a, b) in zip(out_refs, splits):
        for c0 in range(a, b, COL_CHUNK):
            c1 = min(c0 + COL_CHUNK, b)
            o_ref[:, c0 - a:c1 - a] = _dot(h, w_ref[:, c0:c1]).astype(o_ref.dtype)


def _inproj(x, gain, mod_l, w, splits, dtypes, name):
    n_out = w.shape[1]
    tok = lambda i: (i, 0)
    const = lambda i: (0, 0)
    return pl.pallas_call(
        functools.partial(_inproj_kernel, splits=splits),
        grid=(N_TOK // TOKEN_TILE,),
        in_specs=[
            pl.BlockSpec((TOKEN_TILE, D_MODEL), tok),
            pl.BlockSpec((1, D_MODEL), const),
            pl.BlockSpec((None, 1, 6 * D_MODEL), lambda i: (_mod_row(i), 0, 0)),
            pl.BlockSpec((D_MODEL, n_out), const),
        ],
        out_specs=[pl.BlockSpec((TOKEN_TILE, b - a), tok) for a, b in splits],
        out_shape=[jax.ShapeDtypeStruct((N_TOK, b - a), dt) for (a, b), dt in zip(splits, dtypes)],
        compiler_params=_cparams(1),
        name=name,
    )(x, gain.reshape(1, D_MODEL), mod_l, w)


def _outproj_mlp_kernel(x_ref, y_ref, wo_ref, gain_ref, mod_ref, w1_ref, w2_ref, o_ref):
    m = mod_ref[...]
    g1 = m[:, 2 * D_MODEL:3 * D_MODEL]
    sh2 = m[:, 3 * D_MODEL:4 * D_MODEL]
    sc2 = m[:, 4 * D_MODEL:5 * D_MODEL]
    g2 = m[:, 5 * D_MODEL:6 * D_MODEL]
    x1 = x_ref[...] + g1 * _dot(y_ref[...], wo_ref[...])
    h = _mod_norm(x1, gain_ref[...], sh2, sc2).astype(BF16)
    acc = jnp.zeros(x1.shape, F32)
    for c0 in range(0, MLP_HIDDEN, COL_CHUNK):
        a = jnp.maximum(_dot(h, w1_ref[:, c0:c0 + COL_CHUNK]), 0.0)
        acc = acc + _dot((a * a).astype(BF16), w2_ref[c0:c0 + COL_CHUNK, :])
    o_ref[...] = x1 + g2 * acc


def _outproj_mlp(x, y, w_out, gain, mod_l, w1, w2, name):
    k_in = y.shape[1]
    tile = TOKEN_TILE // 2
    per_row = TOKEN_TILE // tile
    tok = lambda i: (i, 0)
    const = lambda i: (0, 0)
    return pl.pallas_call(
        _outproj_mlp_kernel,
        grid=(N_TOK // tile,),
        in_specs=[
            pl.BlockSpec((tile, D_MODEL), tok),
            pl.BlockSpec((tile, k_in), tok),
            pl.BlockSpec((k_in, D_MODEL), const),
            pl.BlockSpec((1, D_MODEL), const),
            pl.BlockSpec((None, 1, 6 * D_MODEL), lambda i: (_mod_row(i // per_row), 0, 0)),
            pl.BlockSpec((D_MODEL, MLP_HIDDEN), const),
            pl.BlockSpec((MLP_HIDDEN, D_MODEL), const),
        ],
        out_specs=pl.BlockSpec((tile, D_MODEL), tok),
        out_shape=jax.ShapeDtypeStruct((N_TOK, D_MODEL), F32),
        compiler_params=_cparams(1),
        name=name,
    )(x, y, w_out, gain.reshape(1, D_MODEL), mod_l, w1, w2)


def _final_norm_kernel(x_ref, g_ref, o_ref):
    x = x_ref[...]
    o_ref[...] = x * lax.rsqrt(jnp.mean(x * x, axis=-1, keepdims=True) + NORM_EPS) * g_ref[...]


def _final_norm(x, gain):
    return pl.pallas_call(
        _final_norm_kernel,
        grid=(N_TOK // TOKEN_TILE,),
        in_specs=[pl.BlockSpec((TOKEN_TILE, D_MODEL), lambda i: (i, 0)),
                  pl.BlockSpec((1, D_MODEL), lambda i: (0, 0))],
        out_specs=pl.BlockSpec((TOKEN_TILE, D_MODEL), lambda i: (i, 0)),
        out_shape=jax.ShapeDtypeStruct((N_TOK, D_MODEL), F32),
        compiler_params=_cparams(1),
        name="final_norm",
    )(x, gain.reshape(1, D_MODEL))


def _rope_tables(n_tok):
    t = np.arange(n_tok)
    pos = np.stack([t // GRID_W, t % GRID_W], axis=1).astype(np.float64)
    lane = np.arange(LANES)
    d = lane % ATT_HEAD_DIM
    part = d // (ATT_HEAD_DIM // 2)
    within = d % (ATT_HEAD_DIM // 2)
    quarter = ATT_HEAD_DIM // 4
    first_half = within < quarter
    return pos, part, within % quarter, first_half


def _rope_arrays(n_tok):
    axis_dim = ATT_HEAD_DIM // 2
    inv_freq = ROPE_THETA ** (-jnp.arange(0, axis_dim, 2, dtype=F32) / axis_dim)
    t = jnp.arange(n_tok)
    row = (t // GRID_W).astype(F32)
    col = (t % GRID_W).astype(F32)
    _, part, fidx, first_half = _rope_tables(n_tok)
    freq = inv_freq[jnp.asarray(fidx)]
    pos = jnp.where(jnp.asarray(part)[None, :] == 0, row[:, None], col[:, None])
    ang = pos * freq[None, :]
    cos = jnp.cos(ang)
    sin = jnp.sin(ang)
    fh = jnp.asarray(first_half)[None, :]
    return cos, jnp.where(fh, -sin, 0.0), jnp.where(fh, 0.0, sin)


def _apply_rope(x, cos, sin_a, sin_b):
    quarter = ATT_HEAD_DIM // 4
    return (x * cos + pltpu.roll(x, LANES - quarter, axis=1) * sin_a
            + pltpu.roll(x, quarter, axis=1) * sin_b)


def _head_rms(x, gmat, gain):
    ss = _dot_xl(x * x, gmat)
    return x * lax.rsqrt(ss * (1.0 / ATT_HEAD_DIM) + NORM_EPS) * gain


def _attn_kernel(*refs, n, n_ctx, rope, emit_k, tq):
    it = iter(refs)
    q_ref, k_ref, v_ref = next(it), next(it), next(it)
    qg_ref, kg_ref, gmat_ref = next(it), next(it), next(it)
    if n_ctx:
        ck_ref, cv_ref = next(it), next(it)
    if rope:
        cos_ref, sa_ref, sb_ref = next(it), next(it), next(it)
    o_ref = next(it)
    if emit_k:
        ko_ref = next(it)
    keys_ref, vals_ref = next(it), next(it)

    gmat = gmat_ref[...]
    kn = _head_rms(k_ref[...], gmat, kg_ref[...])
    if emit_k:
        ko_ref[...] = kn
    if rope:
        kn = _apply_rope(kn, cos_ref[...], sa_ref[...], sb_ref[...])
    keys_ref[0:n, :] = kn.astype(BF16)
    vals_ref[0:n, :] = v_ref[...].astype(BF16)
    if n_ctx:
        keys_ref[n:n + n_ctx, :] = ck_ref[...].astype(BF16)
        vals_ref[n:n + n_ctx, :] = cv_ref[...].astype(BF16)
    low =lax.broadcasted_iota(jnp.int32, (1, LANES), 1) < ATT_HEAD_DIM
    scale = ATT_HEAD_DIM ** -0.5
    heads_per_tile = LANES // ATT_HEAD_DIM
    tiles_per_kv = (ATT_HEADS // ATT_KV_HEADS) // heads_per_tile

    def attend(qm):
        s = _dot_nt(qm.astype(BF16), keys_ref[...])
        p = jnp.exp(s - jnp.max(s, axis=-1, keepdims=True))
        l = jnp.sum(p, axis=-1, keepdims=True)
        return _dot(p.astype(BF16), vals_ref[...]) * (1.0 / l)

    def q_tile(qi, carry):
        rows = pl.ds(pl.multiple_of(qi * tq, tq), tq)
        for t in range(ATT_Q_DIM // LANES):
            cols = slice(t * LANES, (t + 1) * LANES)
            qn = _head_rms(q_ref[rows, cols], gmat, qg_ref[...])
            if rope:
                qn = _apply_rope(qn, cos_ref[rows, :], sa_ref[rows, :], sb_ref[rows, :])
            qn = qn * scale
            qs = pltpu.roll(qn, ATT_HEAD_DIM, axis=1)
            if t // tiles_per_kv == 0:
                r_e = attend(jnp.where(low, qn, 0.0))
                r_o = attend(jnp.where(low, qs, 0.0))
                o = jnp.where(low, r_e, pltpu.roll(r_o, ATT_HEAD_DIM, axis=1))
            else:
                r_e = attend(jnp.where(low, 0.0, qs))
                r_o = attend(jnp.where(low, 0.0, qn))
                o = jnp.where(low, pltpu.roll(r_e, ATT_HEAD_DIM, axis=1), r_o)
            o_ref[rows, cols] = o.astype(o_ref.dtype)
        return carry

    lax.fori_loop(0, n // tq, q_tile, 0)


def _attention(q, k, v, q_gain, k_gain, *, n_batch, n, tok0, ctx=None, emit_k=False):
    rope = ctx is not None
    n_ctx = ctx[0].shape[1] if rope else 0
    b0 = tok0 // n
    tq = 128 if rope else n
    tokb = lambda b: (b0 + b, 0)
    const = lambda b: (0, 0)
    heads_per_tile = LANES // ATT_HEAD_DIM
    gmat = jnp.asarray(np.kron(np.eye(heads_per_tile), np.ones((ATT_HEAD_DIM, ATT_HEAD_DIM))), BF16)
    args = [q, k, v, jnp.tile(q_gain, heads_per_tile).reshape(1, LANES),
            jnp.tile(k_gain, heads_per_tile).reshape(1, LANES), gmat]
    in_specs = [
        pl.BlockSpec((n, ATT_Q_DIM), tokb),
        pl.BlockSpec((n, ATT_KV_DIM), tokb),
        pl.BlockSpec((n, ATT_KV_DIM), tokb),
        pl.BlockSpec((1, LANES), const),
        pl.BlockSpec((1, LANES), const),
        pl.BlockSpec((LANES, LANES), const),
    ]
    if rope:
        args += [ctx[0], ctx[1]]
        in_specs += [pl.BlockSpec((None, n_ctx, ATT_KV_DIM), lambda b: (b, 0, 0))] * 2
        args += list(_rope_arrays(n))
        in_specs += [pl.BlockSpec((n, LANES), const)] * 3
    out_shape = [jax.ShapeDtypeStruct((n_batch * n, ATT_Q_DIM), BF16)]
    out_specs = [pl.BlockSpec((n, ATT_Q_DIM), lambda b: (b, 0))]
    if emit_k:
        out_shape.append(jax.ShapeDtypeStruct((n_batch * n, ATT_KV_DIM), F32))
        out_specs.append(pl.BlockSpec((n, ATT_KV_DIM), lambda b: (b, 0)))
    return pl.pallas_call(
        functools.partial(_attn_kernel, n=n, n_ctx=n_ctx, rope=rope, emit_k=emit_k, tq=tq),
        grid=(n_batch,),
        in_specs=in_specs,
        out_specs=out_specs,
        out_shape=out_shape,
        scratch_shapes=[pltpu.VMEM((n + n_ctx, ATT_KV_DIM), BF16),
                        pltpu.VMEM((n + n_ctx, ATT_KV_DIM), BF16)],
        compiler_params=_cparams(1),
        name="attention_latent" if rope else "attention_context",
    )(*args)


def _conv_silu(x, w, bias, n):
    row = lax.broadcasted_iota(jnp.int32, x.shape, 0)
    prev = jnp.where(row == 0, 0.0, pltpu.roll(x, 1, axis=0))
    nxt = jnp.where(row == n - 1, 0.0, pltpu.roll(x, n - 1, axis=0))
    y = prev * w[0:1, :] + x * w[1:2, :] + nxt * w[2:3, :]
    if bias is not None:
        y = y + bias
    return _silu(y)


def _delta_kernel(*refs, n, has_s0, emit_s):
    it = iter(refs)
    dqkv_ref, dz_ref, small_ref, arow_ref = next(it), next(it), next(it), next(it)
    convw_ref, pcol_ref, prow_ref, gain_ref = next(it), next(it), next(it), next(it)
    if has_s0:
        s0_ref = next(it)
    o_ref = next(it)
    if emit_s:
        sfin_ref = next(it)
    q_scr, k_scr, v_scr, g_scr, b_scr, of_scr, ob_scr, s_scr = (next(it) for _ in range(8))

    n_chunks = n // CHUNK
    for h in range(DN_HEADS):
        for part, scr in ((0, q_scr), (1, k_scr), (2, v_scr)):
            c0 = part * DN_QK_DIM + h * DN_KEY_DIM
            x = _conv_silu(dqkv_ref[:, c0:c0 + LANES].astype(F32), convw_ref[:, c0:c0 + LANES], None, n)
            if part < 2:
                x = x * lax.rsqrt(jnp.sum(x * x, axis=-1, keepdims=True) + NORM_EPS)
            if part == 0:
                x = x * (DN_KEY_DIM ** -0.5)
            scr[:, h * LANES:(h + 1) * LANES] = x
    small = small_ref[...]
    pcol = pcol_ref[...]
    b_scr[...] = _sigmoid(small)
    g_scr[...] = -jnp.exp(pcol[0:1, :]) * _softplus(small + pcol[1:2, :])
    prow = prow_ref[...]
    neg_a_row = -jnp.exp(prow[:, 0:1])
    bias_row = prow[:, 1:2]
    if has_s0:
        s_scr[...] = s0_ref[...]
    else:
        s_scr[...] = jnp.zeros(s_scr.shape, F32)

    ri = lax.broadcasted_iota(jnp.int32, (CHUNK, CHUNK), 0)
    ci = lax.broadcasted_iota(jnp.int32, (CHUNK, CHUNK), 1)
    eye = (ri == ci).astype(F32)
    lower_b = (ri >= ci).astype(BF16)
    upper_b = (ri <= ci).astype(BF16)

    def chunk_step(c, d):
        upper = d == 1
        r0 = pl.multiple_of(c * CHUNK, CHUNK)
        rows = pl.ds(r0, CHUNK)
        incl = (ri <= ci) if upper else (ri >= ci)
        strict = (ri < ci) if upper else (ri > ci)
        off_masks = []
        for lvl in range(int(math.log2(CHUNK))):
            same_pair = _shr(ri, lvl + 1) == _shr(ci, lvl + 1)
            half_r, half_c = _shr(ri, lvl), _shr(ci, lvl)
            side = (half_r < half_c) if upper else (half_r > half_c)
            off_masks.append(jnp.where(same_pair, jnp.where(side, 1.0, 0.0), 0.0))
        tri_col = upper_b if upper else lower_b
        tri_row = lower_b if upper else upper_b
        gc_all = _dot_xr(tri_col, g_scr[rows, :])
        g_row = neg_a_row * _softplus(arow_ref[c] + bias_row)
        gcr_all = _dot_xl(g_row, tri_row)
        beta_all = b_scr[rows, :]
        for h in range(DN_HEADS):
            idx = d * DN_HEADS + h
            gc_c = gc_all[:, 8 + idx:9 + idx]
            gc_r = gcr_all[idx:idx + 1, :]
            g_tot = gc_c[0:1, :] if upper else gc_c[CHUNK - 1:CHUNK, :]
            decay = jnp.exp(jnp.where(incl, gc_c - gc_r, NEG_BIG))
            beta_c = beta_all[:, idx:idx + 1]
            cols = slice(h * LANES, (h + 1) * LANES)
            q = q_scr[rows, cols]
            k = k_scr[rows, cols]
            v = v_scr[rows, cols]
            kb = k * beta_c
            egc = jnp.exp(gc_c)
            a = _dot_nt(jnp.concatenate([kb, q], axis=0).astype(BF16), k.astype(BF16))
            m = jnp.where(strict, a[0:CHUNK] * decay, 0.0)
            qk = a[CHUNK:2 * CHUNK] * decay
            p = eye - m * off_masks[0]
            for off in off_masks[1:]:
                pb = p.astype(BF16)
                p = p - _dot(_dot(pb, (m * off).astype(BF16)).astype(BF16), pb)
            rhs = jnp.concatenate([v * beta_c, kb * egc], axis=1).astype(BF16)
            uw = _dot(p.astype(BF16), rhs)
            u = uw[:, 0:DN_VAL_DIM]
            w = uw[:, DN_VAL_DIM:]
            s_old = s_scr[d, h]
            wq = _dot(jnp.concatenate([w, q * egc], axis=0).astype(BF16), s_old.astype(BF16))
            v_new = u - wq[0:CHUNK]
            o = wq[CHUNK:2 * CHUNK] + _dot(qk.astype(BF16), v_new.astype(BF16))
            k_dec = k * jnp.exp(g_tot - gc_c)
            s_scr[d, h] = s_old * jnp.exp(g_tot) + _dot_tn(k_dec.astype(BF16), v_new.astype(BF16))
            (ob_scr if upper else of_scr)[rows, cols] = o

    def body(c, carry):
        chunk_step(c, 0)
        chunk_step(n_chunks - 1 - c, 1)
        return carry

    lax.fori_loop(0, n_chunks, body, 0)

    if emit_s:
        sfin_ref[...] = s_scr[...]
    for h in range(DN_HEADS):
        cols = slice(h * LANES, (h + 1) * LANES)
        o = of_scr[:, cols] + ob_scr[:, cols]
        o = o * lax.rsqrt(jnp.mean(o * o, axis=-1, keepdims=True) + NORM_EPS) * gain_ref[...]
        o_ref[:, cols] = (o * _silu(dz_ref[:, cols].astype(F32))).astype(o_ref.dtype)


def _delta(dqkv, dz, small, conv_w, a_log, dt_bias, out_gain, *, n_batch, n, tok0, s0=None, emit_s=False):
    b0 = tok0 // n
    n_chunks = n // CHUNK
    n_hd = 2 * DN_HEADS
    tokb = lambda b: (b0 + b, 0)
    const = lambda b: (0, 0)
    alpha = small[tok0:tok0 + n_batch * n, n_hd:2 * n_hd]
    arow = alpha.reshape(n_batch, n_chunks, CHUNK, n_hd).transpose(0, 1, 3, 2)
    pcol = jnp.zeros((2, LANES), F32)
    pcol = pcol.at[0, n_hd:2 * n_hd].set(a_log.reshape(-1)).at[1, n_hd:2 * n_hd].set(dt_bias.reshape(-1))
    prow = jnp.stack([a_log.reshape(-1), dt_bias.reshape(-1)], axis=1)
    args = [dqkv, dz, small, arow, conv_w, pcol, prow, out_gain.reshape(1, LANES)]
    in_specs = [
        pl.BlockSpec((n, DN_CONV_DIM), tokb),
        pl.BlockSpec((n, DN_V_DIM), tokb),
        pl.BlockSpec((n, LANES), tokb),
        pl.BlockSpec((None, n_chunks, n_hd, CHUNK), lambda b: (b, 0, 0, 0)),
        pl.BlockSpec((3, DN_CONV_DIM), const),
        pl.BlockSpec((2, LANES), const),
        pl.BlockSpec((n_hd, 2), const),
        pl.BlockSpec((1, LANES), const),
    ]
    state_spec = pl.BlockSpec((None, 2, DN_HEADS, DN_KEY_DIM, DN_VAL_DIM), lambda b: (b, 0, 0, 0, 0))
    if s0 is not None:
        args.append(s0)
        in_specs.append(state_spec)
    out_shape = [jax.ShapeDtypeStruct((n_batch * n, DN_V_DIM), BF16)]
    out_specs = [pl.BlockSpec((n, DN_V_DIM), lambda b: (b, 0))]
    if emit_s:
        out_shape.append(jax.ShapeDtypeStruct((n_batch, 2, DN_HEADS, DN_KEY_DIM, DN_VAL_DIM), F32))
        out_specs.append(state_spec)
    return pl.pallas_call(
        functools.partial(_delta_kernel, n=n, has_s0=s0 is not None, emit_s=emit_s),
        grid=(n_batch,),
        in_specs=in_specs,
        out_specs=out_specs,
        out_shape=out_shape,
        scratch_shapes=[
            pltpu.VMEM((n, DN_QK_DIM), F32), pltpu.VMEM((n, DN_QK_DIM), F32), pltpu.VMEM((n, DN_V_DIM), F32),
            pltpu.VMEM((n, LANES), F32), pltpu.VMEM((n, LANES), F32),
            pltpu.VMEM((n, DN_V_DIM), F32), pltpu.VMEM((n, DN_V_DIM), F32),
            pltpu.VMEM((2, DN_HEADS, DN_KEY_DIM, DN_VAL_DIM), F32),
        ],
        compiler_params=_cparams(1),
        name="delta_latent" if s0 is not None else "delta_context",
    )(*args)


SSM_GROUP_W = SSM_INNER // SSM_GROUPS
SSM_TILES = SSM_INNER // LANES


def _ssd_kernel(*refs, n, has_s0, emit_s):
    it = iter(refs)
    z_ref, xbc_ref, dtc_ref, dtr_ref = next(it), next(it), next(it), next(it)
    convw_ref, convb_ref, pcol_ref, prow_ref = next(it), next(it), next(it), next(it)
    dskip_ref, gain_ref, expand_ref, tri2_ref = next(it), next(it), next(it), next(it)
    if has_s0:
        s0_ref = next(it)
    o_ref = next(it)
    if emit_s:
        sfin_ref = next(it)
    xs_scr, bc_scr, y_scr, dt_scr, st_scr = (next(it) for _ in range(5))

    n_chunks = n // CHUNK
    def conv_tile(t):
        cols = pl.ds(pl.multiple_of(t * LANES, LANES), LANES)
        return _conv_silu(xbc_ref[:, cols].astype(F32), convw_ref[:, cols], convb_ref[:, cols], n)

    def prep_x(t, carry):
        cols = pl.ds(pl.multiple_of(t * LANES, LANES), LANES)
        x = conv_tile(t)
        xs_scr[:, cols] = x
        y_scr[:, cols] = x * dskip_ref[:, cols]
        return carry

    def prep_bc(t, carry):
        cols = pl.ds(pl.multiple_of(t * LANES, LANES), LANES)
        bc_scr[:, cols] = conv_tile(t + SSM_TILES).astype(BF16)
        return carry

    lax.fori_loop(0, SSM_TILES, prep_x, 0)
    lax.fori_loop(0, 2 * SSM_BC_DIM // LANES, prep_bc, 0)
    pcol = pcol_ref[...]
    dt_scr[...] = _softplus(dtc_ref[...] + pcol[1:2, :])
    neg_a_col = -jnp.exp(pcol[0:1, :])
    for d in range(2):
        for g in range(SSM_GROUPS):
            gs = slice(g * SSM_GROUP_W, (g + 1) * SSM_GROUP_W)
            if has_s0:
                st_scr[d, :, gs] = s0_ref[d, gs, :].T
            else:
                st_scr[d, :, gs] = jnp.zeros((SSM_STATE, SSM_GROUP_W), F32)

    ri = lax.broadcasted_iota(jnp.int32, (CHUNK, CHUNK), 0)
    ci = lax.broadcasted_iota(jnp.int32, (CHUNK, CHUNK), 1)
    lower_b = (ri >= ci).astype(BF16)
    upper_b = (ri <= ci).astype(BF16)
    r2 = lax.broadcasted_iota(jnp.int32, (CHUNK, LANES), 0)
    c2 = lax.broadcasted_iota(jnp.int32, (CHUNK, LANES), 1) & (CHUNK - 1)
    low = lax.broadcasted_iota(jnp.int32, (1, LANES), 1) < SSM_HEAD_DIM

    def chunk_step(c, d):
        upper = d == 1
        r0 = pl.multiple_of(c * CHUNK, CHUNK)
        rows = pl.ds(r0, CHUNK)
        incl2 = (r2 <= c2) if upper else (r2 >= c2)
        dt = dt_scr[rows, :]
        acum = _dot_xr(upper_b if upper else lower_b, dt * neg_a_col)
        a_tot = acum[0:1, :] if upper else acum[CHUNK - 1:CHUNK, :]
        stack = jnp.concatenate(
            [acum, dt, dt * jnp.exp(a_tot - acum), jnp.exp(acum),
             jnp.broadcast_to(jnp.exp(a_tot), (8, LANES))], axis=0)
        s_hi, s_mid, s_lo = _split3(stack)
        prow = prow_ref[d]
        dt_r = _softplus(dtr_ref[c, d] + prow[1])
        acum_r = _dot_xl(dt_r * (-jnp.exp(prow[0])), tri2_ref[1 - d])
        for g in range(SSM_GROUPS):
            gs = slice(g * SSM_GROUP_W, (g + 1) * SSM_GROUP_W)
            e = expand_ref[d, :, gs]
            ex = _dot(s_hi, e) + _dot(s_mid, e) + _dot(s_lo, e)
            acum_e = ex[0:CHUNK]
            dt_e = ex[CHUNK:2 * CHUNK]
            dtd_e = ex[2 * CHUNK:3 * CHUNK]
            eac_e = ex[3 * CHUNK:4 * CHUNK]
            dec_e = ex[4 * CHUNK:4 * CHUNK + 1]
            xs_g = xs_scr[rows, gs]
            bm_g = bc_scr[rows, g * SSM_STATE:(g + 1) * SSM_STATE]
            cm_g = bc_scr[rows, SSM_BC_DIM + g * SSM_STATE:SSM_BC_DIM + (g + 1) * SSM_STATE]
            cb2 = _dot_nt(cm_g, jnp.concatenate([bm_g, bm_g], axis=0))
            x_dt = xs_g * dt_e
            y_parts = []
            for tt in range(SSM_GROUP_W // LANES):
                t = g * (SSM_GROUP_W // LANES) + tt
                ls = slice(tt * LANES, (tt + 1) * LANES)
                lmat = jnp.exp(jnp.where(incl2, acum_e[:, ls] - acum_r[t:t + 1, :], NEG_BIG))
                xt = x_dt[:, ls]
                bd = jnp.concatenate([jnp.where(low, xt, 0.0), jnp.where(low, 0.0, xt)], axis=0)
                y_parts.append(_dot((cb2 * lmat).astype(BF16), bd.astype(BF16)))
            st = st_scr[d, :, gs]
            y_off = _dot(cm_g, st.astype(BF16)) * eac_e
            st_scr[d, :, gs] = st * dec_e + _dot_tn(bm_g, (xs_g * dtd_e).astype(BF16))
            y_scr[rows, gs] = y_scr[rows, gs] + jnp.concatenate(y_parts, axis=1) + y_off

    def body(c, carry):
        chunk_step(c, 0)
        chunk_step(n_chunks - 1 - c, 1)
        return carry

    lax.fori_loop(0, n_chunks, body, 0)

    if emit_s:
        for d in range(2):
            for g in range(SSM_GROUPS):
                gs = slice(g * SSM_GROUP_W, (g + 1) * SSM_GROUP_W)
                sfin_ref[d, gs, :] = st_scr[d, :, gs].T
    row_tile = 128

    def finish(r, carry):
        rows = pl.ds(pl.multiple_of(r * row_tile, row_tile), row_tile)
        y = y_scr[rows, :] * _silu(z_ref[rows, :].astype(F32))
        y = y * lax.rsqrt(jnp.mean(y * y, axis=-1, keepdims=True) + NORM_EPS) * gain_ref[...]
        o_ref[rows, :] = y.astype(o_ref.dtype)
        return carry

    lax.fori_loop(0, n // row_tile, finish, 0)


def _ssd(z, xbc, dt_raw, conv_w, conv_b, a_log, dt_bias, d_skip, out_gain, *, n_batch, n, tok0,
         s0=None, emit_s=False):
    b0 = tok0 // n
    n_chunks = n // CHUNK
    tokb = lambda b: (b0 + b, 0)
    const = lambda b: (0, 0)
    const3 = lambda b: (0, 0, 0)
    dtr = dt_raw[tok0:tok0 + n_batch * n, 0:2 * SSM_HEADS]
    dtr = dtr.reshape(n_batch, n_chunks, CHUNK, 2, SSM_TILES, 2).transpose(0, 1, 3, 4, 5, 2)
    dtr = dtr.reshape(n_batch, n_chunks, 2, SSM_TILES, LANES)
    pcol = jnp.zeros((2, LANES), F32)
    pcol = pcol.at[0, 0:2 * SSM_HEADS].set(a_log.reshape(-1)).at[1, 0:2 * SSM_HEADS].set(dt_bias.reshape(-1))
    to_rows = lambda p: jnp.repeat(p.reshape(2, SSM_TILES, 2), CHUNK, axis=-1)
    prow = jnp.stack([to_rows(a_log), to_rows(dt_bias)], axis=1)
    expand = np.zeros((2, LANES, SSM_INNER), np.float32)
    for d in range(2):
        for h in range(SSM_HEADS):
            expand[d, d * SSM_HEADS + h, h * SSM_HEAD_DIM:(h + 1) * SSM_HEAD_DIM] = 1.0
    tri = np.tril(np.ones((CHUNK, CHUNK), np.float32))
    tri2 = np.stack([np.kron(np.eye(2), tri), np.kron(np.eye(2), tri.T)])
    args = [z, xbc, dt_raw, dtr, conv_w, conv_b.reshape(1, SSM_CONV_DIM), pcol, prow,
            jnp.repeat(d_skip, SSM_HEAD_DIM).reshape(1, SSM_INNER), out_gain.reshape(1, SSM_INNER),
            jnp.asarray(expand, BF16), jnp.asarray(tri2, BF16)]
    big = dict(pipeline_mode=pl.Buffered(1)) if n > SEQ else {}
    in_specs = [
        pl.BlockSpec((n, SSM_INNER), tokb, **big),
        pl.BlockSpec((n, SSM_CONV_DIM), tokb, **big),
        pl.BlockSpec((n, LANES), tokb),
        pl.BlockSpec((None, n_chunks, 2, SSM_TILES, LANES), lambda b: (b, 0, 0, 0, 0)),
        pl.BlockSpec((3, SSM_CONV_DIM), const),
        pl.BlockSpec((1, SSM_CONV_DIM), const),
        pl.BlockSpec((2, LANES), const),
        pl.BlockSpec((2, 2, SSM_TILES, LANES), lambda b: (0, 0, 0, 0)),
        pl.BlockSpec((1, SSM_INNER), const),
        pl.BlockSpec((1, SSM_INNER), const),
        pl.BlockSpec((2, LANES, SSM_INNER), const3),
        pl.BlockSpec((2, LANES, LANES), const3),
    ]
    state_spec = pl.BlockSpec((None, 2, SSM_INNER, SSM_STATE), lambda b: (b, 0, 0, 0))
    if s0 is not None:
        args.append(s0.reshape(n_batch, 2, SSM_INNER, SSM_STATE))
        in_specs.append(state_spec)
    out_shape = [jax.ShapeDtypeStruct((n_batch * n, SSM_INNER), BF16)]
    out_specs = [pl.BlockSpec((n, SSM_INNER), lambda b: (b, 0))]
    if emit_s:
        out_shape.append(jax.ShapeDtypeStruct((n_batch, 2, SSM_INNER, SSM_STATE), F32))
        out_specs.append(state_spec)
    return pl.pallas_call(
        functools.partial(_ssd_kernel, n=n, has_s0=s0 is not None, emit_s=emit_s),
        grid=(n_batch,),
        in_specs=in_specs,
        out_specs=out_specs,
        out_shape=out_shape,
        scratch_shapes=[
            pltpu.VMEM((n, SSM_INNER), F32), pltpu.VMEM((n, 2 * SSM_BC_DIM), BF16),
            pltpu.VMEM((n, SSM_INNER), F32), pltpu.VMEM((n, LANES), F32),
            pltpu.VMEM((2, SSM_STATE, SSM_INNER), F32),
        ],
        compiler_params=_cparams(1),
        name="ssd_latent" if s0 is not None else "ssd_context",
    )(*args)


EVEN_MAIN = ATT_Q_DIM + 2 * ATT_KV_DIM + DN_CONV_DIM + DN_V_DIM
EVEN_SPLITS = ((0, ATT_Q_DIM), (ATT_Q_DIM, ATT_Q_DIM + ATT_KV_DIM),
               (ATT_Q_DIM + ATT_KV_DIM, ATT_Q_DIM + 2 * ATT_KV_DIM),
               (ATT_Q_DIM + 2 * ATT_KV_DIM, ATT_Q_DIM + 2 * ATT_KV_DIM + DN_CONV_DIM),
               (ATT_Q_DIM + 2 * ATT_KV_DIM + DN_CONV_DIM, EVEN_MAIN), (EVEN_MAIN, EVEN_MAIN + LANES))
EVEN_DTYPES = (F32, F32, F32, BF16, BF16, F32)
ODD_MAIN = SSM_INNER + SSM_CONV_DIM
ODD_SPLITS = ((0, SSM_INNER), (SSM_INNER, ODD_MAIN), (ODD_MAIN, ODD_MAIN + LANES))
ODD_DTYPES = (BF16, BF16, F32)


def _pad_cols(w, n_total):
    return jnp.pad(w, ((0, 0), (0, n_total - w.shape[1]))).astype(BF16)


def kernel(x_prompt, x_sample, c, cache_attn_k, cache_attn_v, state_delta, state_ssm, c_ctx, norm_mix_g, norm_mlp_g, w_mod, b_mod, w_mlp_in, w_mlp_out, w_in_even, attn_q_norm_g, attn_k_norm_g, delta_conv_w, delta_a_log, delta_dt_bias, delta_norm_g, w_out_even, w_in_odd, ssm_conv_w, ssm_conv_b, ssm_a_log, ssm_dt_bias, ssm_d, ssm_norm_g, w_out_odd, final_norm_g):
    x = jnp.concatenate([x_prompt.reshape(N_PROMPT_TOK, D_MODEL), x_sample.reshape(N_SAMPLE_TOK, D_MODEL)], axis=0)
    cond = jnp.zeros((N_COND, D_MODEL), F32).at[0].set(c_ctx).at[1:1 + DEC_BATCH].set(c)
    mod = _modulation(cond, w_mod, b_mod).reshape(DEPTH, N_COND, 1, 6 * D_MODEL)

    ks, vs, sds, sss = [], [], [], []
    for layer in range(DEPTH):
        j = layer // 2
        if layer % 2 == 0:
            w_in = _pad_cols(w_in_even[j], EVEN_MAIN + LANES)
            q, k, v, dqkv, dz, small = _inproj(x, norm_mix_g[layer], mod[layer], w_in, EVEN_SPLITS,
                                               EVEN_DTYPES, "inproj_even")
            ctx_k = cache_attn_k[:, j].reshape(DEC_BATCH, PAST_LEN, ATT_KV_DIM)
            ctx_v = cache_attn_v[:, j].reshape(DEC_BATCH, PAST_LEN, ATT_KV_DIM)
            o_att_p, k_norm = _attention(q, k, v, attn_q_norm_g[j], attn_k_norm_g[j],
                                         n_batch=BATCH, n=SEQ, tok0=0, emit_k=True)
            (o_att_s,) = _attention(q, k, v, attn_q_norm_g[j], attn_k_norm_g[j],
                                    n_batch=DEC_BATCH, n=DEC_SEQ, tok0=N_PROMPT_TOK, ctx=(ctx_k, ctx_v))
            dn_args = (dqkv, dz, small, delta_conv_w[j], delta_a_log[j], delta_dt_bias[j], delta_norm_g[j])
            o_dn_p, s_delta = _delta(*dn_args, n_batch=BATCH, n=SEQ, tok0=0, emit_s=True)
            (o_dn_s,) = _delta(*dn_args, n_batch=DEC_BATCH, n=DEC_SEQ, tok0=N_PROMPT_TOK, s0=state_delta[:, j])
            y = jnp.concatenate([jnp.concatenate([o_att_p, o_dn_p], axis=1),
                                 jnp.concatenate([o_att_s, o_dn_s], axis=1)], axis=0)
            w_out = w_out_even[j].astype(BF16)
            ks.append(k_norm.reshape(BATCH, SEQ, ATT_KV_HEADS, ATT_HEAD_DIM))
            vs.append(v[:N_PROMPT_TOK].reshape(BATCH, SEQ, ATT_KV_HEADS, ATT_HEAD_DIM))
            sds.append(s_delta)
        else:
            w_in = _pad_cols(w_in_odd[j], ODD_MAIN + LANES)
            z, xbc, dt_raw = _inproj(x, norm_mix_g[layer], mod[layer], w_in, ODD_SPLITS, ODD_DTYPES, "inproj_odd")
            ssd_args = (z, xbc, dt_raw, ssm_conv_w[j], ssm_conv_b[j], ssm_a_log[j], ssm_dt_bias[j], ssm_d[j],
                        ssm_norm_g[j])
            y_p, s_ssm = _ssd(*ssd_args, n_batch=BATCH, n=SEQ, tok0=0, emit_s=True)
            (y_s,) = _ssd(*ssd_args, n_batch=DEC_BATCH, n=DEC_SEQ, tok0=N_PROMPT_TOK, s0=state_ssm[:, j])
            y = jnp.concatenate([y_p, y_s], axis=0)
            w_out = w_out_odd[j].astype(BF16)
            sss.append(s_ssm.reshape(BATCH, 2, SSM_HEADS, SSM_HEAD_DIM, SSM_STATE))
        x = _outproj_mlp(x, y, w_out, norm_mlp_g[layer], mod[layer], w_mlp_in[layer].astype(BF16),
                         w_mlp_out[layer].astype(BF16), "outproj_mlp_even" if layer % 2 == 0 else "outproj_mlp_odd")

    out = _final_norm(x, final_norm_g)
    y_prompt = out[:N_PROMPT_TOK].reshape(BATCH, SEQ, D_MODEL)
    y_sample = out[N_PROMPT_TOK:].reshape(DEC_BATCH, DEC_SEQ, D_MODEL)
    return (y_prompt, y_sample, jnp.stack(ks, axis=1), jnp.stack(vs, axis=1),
            jnp.stack(sds, axis=1), jnp.stack(sss, axis=1))
```

```python
import functools
import math

import jax
import jax.numpy as jnp
import numpy as np
from jax import lax
from jax.experimental import pallas as pl
from jax.experimental.pallas import tpu as pltpu

F32 = jnp.float32
BF16 = jnp.bfloat16

D_MODEL = 1024
BATCH = 32
SEQ = 256
DEPTH = 4
DEC_BATCH = 4
DEC_SEQ = 1024
PAST_LEN = 512
GRID_W = 64
N_EVEN = (DEPTH + 1) // 2
N_ODD = DEPTH // 2
ATT_HEAD_DIM = 64
ATT_HEADS = 8
ATT_KV_HEADS = 2
ATT_Q_DIM = ATT_HEADS * ATT_HEAD_DIM
ATT_KV_DIM = ATT_KV_HEADS * ATT_HEAD_DIM
DN_KEY_DIM = 128
DN_VAL_DIM = 128
DN_HEADS = 4
DN_QK_DIM = DN_HEADS * DN_KEY_DIM
DN_V_DIM = DN_HEADS * DN_VAL_DIM
DN_CONV_DIM = 2 * DN_QK_DIM + DN_V_DIM
SSM_INNER = 2 * D_MODEL
SSM_HEAD_DIM = 64
SSM_HEADS = SSM_INNER // SSM_HEAD_DIM
SSM_GROUPS = 8
SSM_STATE = 128
SSM_BC_DIM = SSM_GROUPS * SSM_STATE
SSM_CONV_DIM = SSM_INNER + 2 * SSM_BC_DIM
MLP_HIDDEN = 4 * D_MODEL
CHUNK = 64
ROPE_THETA = 10000.0
NORM_EPS = 1e-6

LANES = 128
N_PROMPT_TOK = BATCH * SEQ
N_SAMPLE_TOK = DEC_BATCH * DEC_SEQ
N_TOK = N_PROMPT_TOK + N_SAMPLE_TOK
N_COND = 8
TOKEN_TILE = 512
N_PROMPT_TILES = N_PROMPT_TOK // TOKEN_TILE
TILES_PER_DEC_SEQ = DEC_SEQ // TOKEN_TILE
COL_CHUNK = 512
NEG_BIG = -1e30
VMEM_LIMIT = 56 * 1024 * 1024


def _cparams(n_grid):
    return pltpu.CompilerParams(dimension_semantics=("arbitrary",) * n_grid,
                                vmem_limit_bytes=VMEM_LIMIT)


def _silu(x):
    return x / (1.0 + jnp.exp(-x))


def _sigmoid(x):
    return 1.0 / (1.0 + jnp.exp(-x))


def _softplus(x):
    return jnp.maximum(x, 0.0) + jnp.log1p(jnp.exp(-jnp.abs(x)))


def _dot(a, b):
    return jnp.dot(a, b, preferred_element_type=F32)


def _dot_nt(a, b):
    return lax.dot_general(a, b, (((1,), (1,)), ((), ())), preferred_element_type=F32)


def _dot_tn(a, b):
    return lax.dot_general(a, b, (((0,), (0,)), ((), ())), preferred_element_type=F32)


def _split3(a):
    hi = a.astype(BF16)
    r = a - hi.astype(F32)
    mid = r.astype(BF16)
    lo = (r - mid.astype(F32)).astype(BF16)
    return hi, mid, lo


def _dot_xl(a, b_exact):
    hi, mid, lo = _split3(a)
    return _dot(hi, b_exact) + _dot(mid, b_exact) + _dot(lo, b_exact)


def _dot_xr(a_exact, b):
    hi, mid, lo = _split3(b)
    return _dot(a_exact, hi) + _dot(a_exact, mid) + _dot(a_exact, lo)


def _mod_norm(x, gain, shift, scale):
    y = x * lax.rsqrt(jnp.mean(x * x, axis=-1, keepdims=True) + NORM_EPS) * gain
    return y * (1.0 + scale) + shift


def _mod_row(i):
    return jnp.where(i < N_PROMPT_TILES, 0, 1 + (i - N_PROMPT_TILES) // TILES_PER_DEC_SEQ)


def _shr(i, k):
    return lax.shift_right_logical(i, jnp.int32(k))


def _mod_kernel(c_ref, w_ref, b_ref, o_ref):
    s = _silu(c_ref[...]).astype(BF16)
    o_ref[...] = _dot(s, w_ref[...].astype(BF16)) + b_ref[...]


def _modulation(cond, w_mod, b_mod):
    n_col = 6 * D_MODEL // D_MODEL
    return pl.pallas_call(
        _mod_kernel,
        grid=(DEPTH, n_col),
        in_specs=[
            pl.BlockSpec((N_COND, D_MODEL), lambda l, j: (0, 0)),
            pl.BlockSpec((None, D_MODEL, D_MODEL), lambda l, j: (l, 0, j)),
            pl.BlockSpec((None, 1, D_MODEL), lambda l, j: (l, 0, j)),
        ],
        out_specs=pl.BlockSpec((None, N_COND, D_MODEL), lambda l, j: (l, 0, j)),
        out_shape=jax.ShapeDtypeStruct((DEPTH, N_COND, 6 * D_MODEL), F32),
        compiler_params=_cparams(2),
        name="modulation",
    )(cond, w_mod, b_mod.reshape(DEPTH, 1, 6 * D_MODEL))


def _inproj_kernel(x_ref, gain_ref, mod_ref, w_ref, *out_refs, splits):
    m = mod_ref[...]
    h = _mod_norm(x_ref[...], gain_ref[...], m[:, 0:D_MODEL], m[:, D_MODEL:2 * D_MODEL]).astype(BF16)
    for o_ref, (a, b) in zip(out_refs, splits):
        for c0 in range(a, b, COL_CHUNK):
            c1 = min(c0 + COL_CHUNK, b)
            o_ref[:, c0 - a:c1 - a] = _dot(h, w_ref[:, c0:c1]).astype(o_ref.dtype)


def _inproj(x, gain, mod_l, w, splits, dtypes, name):
    n_out = w.shape[1]
    tok = lambda i: (i, 0)
    const = lambda i: (0, 0)
    return pl.pallas_call(
        functools.partial(_inproj_kernel, splits=splits),
        grid=(N_TOK // TOKEN_TILE,),
        in_specs=[
            pl.BlockSpec((TOKEN_TILE, D_MODEL), tok),
            pl.BlockSpec((1, D_MODEL), const),
            pl.BlockSpec((None, 1, 6 * D_MODEL), lambda i: (_mod_row(i), 0, 0)),
            pl.BlockSpec((D_MODEL, n_out), const),
        ],
        out_specs=[pl.BlockSpec((TOKEN_TILE, b - a), tok) for a, b in splits],
        out_shape=[jax.ShapeDtypeStruct((N_TOK, b - a), dt) for (a, b), dt in zip(splits, dtypes)],
        compiler_params=_cparams(1),
        name=name,
    )(x, gain.reshape(1, D_MODEL), mod_l, w)


def _outproj_mlp_kernel(x_ref, y_ref, wo_ref, gain_ref, mod_ref, w1_ref, w2_ref, o_ref):
    m = mod_ref[...]
    g1 = m[:, 2 * D_MODEL:3 * D_MODEL]
    sh2 = m[:, 3 * D_MODEL:4 * D_MODEL]
    sc2 = m[:, 4 * D_MODEL:5 * D_MODEL]
    g2 = m[:, 5 * D_MODEL:6 * D_MODEL]
    x1 = x_ref[...] + g1 * _dot(y_ref[...], wo_ref[...])
    h = _mod_norm(x1, gain_ref[...], sh2, sc2).astype(BF16)
    acc = jnp.zeros(x1.shape, F32)
    for c0 in range(0, MLP_HIDDEN, COL_CHUNK):
        a = jnp.maximum(_dot(h, w1_ref[:, c0:c0 + COL_CHUNK]), 0.0)
        acc = acc + _dot((a * a).astype(BF16), w2_ref[c0:c0 + COL_CHUNK, :])
    o_ref[...] = x1 + g2 * acc


def _outproj_mlp(x, y, w_out, gain, mod_l, w1, w2, name):
    k_in = y.shape[1]
    tile = TOKEN_TILE // 2
    per_row = TOKEN_TILE // tile
    tok = lambda i: (i, 0)
    const = lambda i: (0, 0)
    return pl.pallas_call(
        _outproj_mlp_kernel,
        grid=(N_TOK // tile,),
        in_specs=[
            pl.BlockSpec((tile, D_MODEL), tok),
            pl.BlockSpec((tile, k_in), tok),
            pl.BlockSpec((k_in, D_MODEL), const),
            pl.BlockSpec((1, D_MODEL), const),
            pl.BlockSpec((None, 1, 6 * D_MODEL), lambda i: (_mod_row(i // per_row), 0, 0)),
            pl.BlockSpec((D_MODEL, MLP_HIDDEN), const),
            pl.BlockSpec((MLP_HIDDEN, D_MODEL), const),
        ],
        out_specs=pl.BlockSpec((tile, D_MODEL), tok),
        out_shape=jax.ShapeDtypeStruct((N_TOK, D_MODEL), F32),
        compiler_params=_cparams(1),
        name=name,
    )(x, y, w_out, gain.reshape(1, D_MODEL), mod_l, w1, w2)


def _final_norm_kernel(x_ref, g_ref, o_ref):
    x = x_ref[...]
    o_ref[...] = x * lax.rsqrt(jnp.mean(x * x, axis=-1, keepdims=True) + NORM_EPS) * g_ref[...]


def _final_norm(x, gain):
    return pl.pallas_call(
        _final_norm_kernel,
        grid=(N_TOK // TOKEN_TILE,),
        in_specs=[pl.BlockSpec((TOKEN_TILE, D_MODEL), lambda i: (i, 0)),
                  pl.BlockSpec((1, D_MODEL), lambda i: (0, 0))],
        out_specs=pl.BlockSpec((TOKEN_TILE, D_MODEL), lambda i: (i, 0)),
        out_shape=jax.ShapeDtypeStruct((N_TOK, D_MODEL), F32),
        compiler_params=_cparams(1),
        name="final_norm",
    )(x, gain.reshape(1, D_MODEL))


def _rope_tables(n_tok):
    t = np.arange(n_tok)
    pos = np.stack([t // GRID_W, t % GRID_W], axis=1).astype(np.float64)
    lane = np.arange(LANES)
    d = lane % ATT_HEAD_DIM
    part = d // (ATT_HEAD_DIM // 2)
    within = d % (ATT_HEAD_DIM // 2)
    quarter = ATT_HEAD_DIM // 4
    first_half = within < quarter
    return pos, part, within % quarter, first_half


def _rope_arrays(n_tok):
    axis_dim = ATT_HEAD_DIM // 2
    inv_freq = ROPE_THETA ** (-jnp.arange(0, axis_dim, 2, dtype=F32) / axis_dim)
    t = jnp.arange(n_tok)
    row = (t // GRID_W).astype(F32)
    col = (t % GRID_W).astype(F32)
    _, part, fidx, first_half = _rope_tables(n_tok)
    freq = inv_freq[jnp.asarray(fidx)]
    pos = jnp.where(jnp.asarray(part)[None, :] == 0, row[:, None], col[:, None])
    ang = pos * freq[None, :]
    cos = jnp.cos(ang)
    sin = jnp.sin(ang)
    fh = jnp.asarray(first_half)[None, :]
    return cos, jnp.where(fh, -sin, 0.0), jnp.where(fh, 0.0, sin)


def _apply_rope(x, cos, sin_a, sin_b):
    quarter = ATT_HEAD_DIM // 4
    return (x * cos + pltpu.roll(x, LANES - quarter, axis=1) * sin_a
            + pltpu.roll(x, quarter, axis=1) * sin_b)


def _head_rms(x, gmat, gain):
    ss = _dot_xl(x * x, gmat)
    return x * lax.rsqrt(ss * (1.0 / ATT_HEAD_DIM) + NORM_EPS) * gain


def _attn_kernel(*refs, n, n_ctx, rope, emit_k, tq):
    it = iter(refs)
    q_ref, k_ref, v_ref = next(it), next(it), next(it)
    qg_ref, kg_ref, gmat_ref = next(it), next(it), next(it)
    if n_ctx:
        ck_ref, cv_ref = next(it), next(it)
    if rope:
        cos_ref, sa_ref, sb_ref = next(it), next(it), next(it)
    o_ref = next(it)
    if emit_k:
        ko_ref = next(it)
    keys_ref, vals_ref = next(it), next(it)

    gmat = gmat_ref[...]
    kn = _head_rms(k_ref[...], gmat, kg_ref[...])
    if emit_k:
        ko_ref[...] = kn
    if rope:
        kn = _apply_rope(kn, cos_ref[...], sa_ref[...], sb_ref[...])
    keys_ref[0:n, :] = kn.astype(BF16)
    vals_ref[0:n, :] = v_ref[...].astype(BF16)
    if n_ctx:
        keys_ref[n:n + n_ctx, :] = ck_ref[...].astype(BF16)
        vals_ref[n:n + n_ctx, :] = cv_ref[...].astype(BF16)
    low =lax.broadcasted_iota(jnp.int32, (1, LANES), 1) < ATT_HEAD_DIM
    scale = ATT_HEAD_DIM ** -0.5
    heads_per_tile = LANES // ATT_HEAD_DIM
    tiles_per_kv = (ATT_HEADS // ATT_KV_HEADS) // heads_per_tile

    def attend(qm):
        s = _dot_nt(qm.astype(BF16), keys_ref[...])
        p = jnp.exp(s - jnp.max(s, axis=-1, keepdims=True))
        l = jnp.sum(p, axis=-1, keepdims=True)
        return _dot(p.astype(BF16), vals_ref[...]) * (1.0 / l)

    def q_tile(qi, carry):
        rows = pl.ds(pl.multiple_of(qi * tq, tq), tq)
        for t in range(ATT_Q_DIM // LANES):
            cols = slice(t * LANES, (t + 1) * LANES)
            qn = _head_rms(q_ref[rows, cols], gmat, qg_ref[...])
            if rope:
                qn = _apply_rope(qn, cos_ref[rows, :], sa_ref[rows, :], sb_ref[rows, :])
            qn = qn * scale
            qs = pltpu.roll(qn, ATT_HEAD_DIM, axis=1)
            if t // tiles_per_kv == 0:
                r_e = attend(jnp.where(low, qn, 0.0))
                r_o = attend(jnp.where(low, qs, 0.0))
                o = jnp.where(low, r_e, pltpu.roll(r_o, ATT_HEAD_DIM, axis=1))
            else:
                r_e = attend(jnp.where(low, 0.0, qs))
                r_o = attend(jnp.where(low, 0.0, qn))
                o = jnp.where(low, pltpu.roll(r_e, ATT_HEAD_DIM, axis=1), r_o)
            o_ref[rows, cols] = o.astype(o_ref.dtype)
        return carry

    lax.fori_loop(0, n // tq, q_tile, 0)


def _attention(q, k, v, q_gain, k_gain, *, n_batch, n, tok0, ctx=None, emit_k=False):
    rope = ctx is not None
    n_ctx = ctx[0].shape[1] if rope else 0
    b0 = tok0 // n
    tq = 128 if rope else n
    tokb = lambda b: (b0 + b, 0)
    const = lambda b: (0, 0)
    heads_per_tile = LANES // ATT_HEAD_DIM
    gmat = jnp.asarray(np.kron(np.eye(heads_per_tile), np.ones((ATT_HEAD_DIM, ATT_HEAD_DIM))), BF16)
    args = [q, k, v, jnp.tile(q_gain, heads_per_tile).reshape(1, LANES),
            jnp.tile(k_gain, heads_per_tile).reshape(1, LANES), gmat]
    in_specs = [
        pl.BlockSpec((n, ATT_Q_DIM), tokb),
        pl.BlockSpec((n, ATT_KV_DIM), tokb),
        pl.BlockSpec((n, ATT_KV_DIM), tokb),
        pl.BlockSpec((1, LANES), const),
        pl.BlockSpec((1, LANES), const),
        pl.BlockSpec((LANES, LANES), const),
    ]
    if rope:
        args += [ctx[0], ctx[1]]
        in_specs += [pl.BlockSpec((None, n_ctx, ATT_KV_DIM), lambda b: (b, 0, 0))] * 2
        args += list(_rope_arrays(n))
        in_specs += [pl.BlockSpec((n, LANES), const)] * 3
    out_shape = [jax.ShapeDtypeStruct((n_batch * n, ATT_Q_DIM), BF16)]
    out_specs = [pl.BlockSpec((n, ATT_Q_DIM), lambda b: (b, 0))]
    if emit_k:
        out_shape.append(jax.ShapeDtypeStruct((n_batch * n, ATT_KV_DIM), F32))
        out_specs.append(pl.BlockSpec((n, ATT_KV_DIM), lambda b: (b, 0)))
    return pl.pallas_call(
        functools.partial(_attn_kernel, n=n, n_ctx=n_ctx, rope=rope, emit_k=emit_k, tq=tq),
        grid=(n_batch,),
        in_specs=in_specs,
        out_specs=out_specs,
        out_shape=out_shape,
        scratch_shapes=[pltpu.VMEM((n + n_ctx, ATT_KV_DIM), BF16),
                        pltpu.VMEM((n + n_ctx, ATT_KV_DIM), BF16)],
        compiler_params=_cparams(1),
        name="attention_latent" if rope else "attention_context",
    )(*args)


def _conv_silu(x, w, bias, n):
    row = lax.broadcasted_iota(jnp.int32, x.shape, 0)
    prev = jnp.where(row == 0, 0.0, pltpu.roll(x, 1, axis=0))
    nxt = jnp.where(row == n - 1, 0.0, pltpu.roll(x, n - 1, axis=0))
    y = prev * w[0:1, :] + x * w[1:2, :] + nxt * w[2:3, :]
    if bias is not None:
        y = y + bias
    return _silu(y)


def _delta_kernel(*refs, n, has_s0, emit_s):
    it = iter(refs)
    dqkv_ref, dz_ref, small_ref, arow_ref = next(it), next(it), next(it), next(it)
    convw_ref, pcol_ref, prow_ref, gain_ref = next(it), next(it), next(it), next(it)
    if has_s0:
        s0_ref = next(it)
    o_ref = next(it)
    if emit_s:
        sfin_ref = next(it)
    q_scr, k_scr, v_scr, g_scr, b_scr, of_scr, ob_scr, s_scr = (next(it) for _ in range(8))

    n_chunks = n // CHUNK
    for h in range(DN_HEADS):
        for part, scr in ((0, q_scr), (1, k_scr), (2, v_scr)):
            c0 = part * DN_QK_DIM + h * DN_KEY_DIM
            x = _conv_silu(dqkv_ref[:, c0:c0 + LANES].astype(F32), convw_ref[:, c0:c0 + LANES], None, n)
            if part < 2:
                x = x * lax.rsqrt(jnp.sum(x * x, axis=-1, keepdims=True) + NORM_EPS)
            if part == 0:
                x = x * (DN_KEY_DIM ** -0.5)
            scr[:, h * LANES:(h + 1) * LANES] = x
    small = small_ref[...]
    pcol = pcol_ref[...]
    b_scr[...] = _sigmoid(small)
    g_scr[...] = -jnp.exp(pcol[0:1, :]) * _softplus(small + pcol[1:2, :])
    prow = prow_ref[...]
    neg_a_row = -jnp.exp(prow[:, 0:1])
    bias_row = prow[:, 1:2]
    if has_s0:
        s_scr[...] = s0_ref[...]
    else:
        s_scr[...] = jnp.zeros(s_scr.shape, F32)

    ri = lax.broadcasted_iota(jnp.int32, (CHUNK, CHUNK), 0)
    ci = lax.broadcasted_iota(jnp.int32, (CHUNK, CHUNK), 1)
    eye = (ri == ci).astype(F32)
    lower_b = (ri >= ci).astype(BF16)
    upper_b = (ri <= ci).astype(BF16)

    def chunk_step(c, d):
        upper = d == 1
        r0 = pl.multiple_of(c * CHUNK, CHUNK)
        rows = pl.ds(r0, CHUNK)
        incl = (ri <= ci) if upper else (ri >= ci)
        strict = (ri < ci) if upper else (ri > ci)
        off_masks = []
        for lvl in range(int(math.log2(CHUNK))):
            same_pair = _shr(ri, lvl + 1) == _shr(ci, lvl + 1)
            half_r, half_c = _shr(ri, lvl), _shr(ci, lvl)
            side = (half_r < half_c) if upper else (half_r > half_c)
            off_masks.append(jnp.where(same_pair, jnp.where(side, 1.0, 0.0), 0.0))
        tri_col = upper_b if upper else lower_b
        tri_row = lower_b if upper else upper_b
        gc_all = _dot_xr(tri_col, g_scr[rows, :])
        g_row = neg_a_row * _softplus(arow_ref[c] + bias_row)
        gcr_all = _dot_xl(g_row, tri_row)
        beta_all = b_scr[rows, :]
        for h in range(DN_HEADS):
            idx = d * DN_HEADS + h
            gc_c = gc_all[:, 8 + idx:9 + idx]
            gc_r = gcr_all[idx:idx + 1, :]
            g_tot = gc_c[0:1, :] if upper else gc_c[CHUNK - 1:CHUNK, :]
            decay = jnp.exp(jnp.where(incl, gc_c - gc_r, NEG_BIG))
            beta_c = beta_all[:, idx:idx + 1]
            cols = slice(h * LANES, (h + 1) * LANES)
            q = q_scr[rows, cols]
            k = k_scr[rows, cols]
            v = v_scr[rows, cols]
            kb = k * beta_c
            egc = jnp.exp(gc_c)
            a = _dot_nt(jnp.concatenate([kb, q], axis=0).astype(BF16), k.astype(BF16))
            m = jnp.where(strict, a[0:CHUNK] * decay, 0.0)
            qk = a[CHUNK:2 * CHUNK] * decay
            p = eye - m * off_masks[0]
            for off in off_masks[1:]:
                pb = p.astype(BF16)
                p = p - _dot(_dot(pb, (m * off).astype(BF16)).astype(BF16), pb)
            rhs = jnp.concatenate([v * beta_c, kb * egc], axis=1).astype(BF16)
            uw = _dot(p.astype(BF16), rhs)
            u = uw[:, 0:DN_VAL_DIM]
            w = uw[:, DN_VAL_DIM:]
            s_old = s_scr[d, h]
            wq = _dot(jnp.concatenate([w, q * egc], axis=0).astype(BF16), s_old.astype(BF16))
            v_new = u - wq[0:CHUNK]
            o = wq[CHUNK:2 * CHUNK] + _dot(qk.astype(BF16), v_new.astype(BF16))
            k_dec = k * jnp.exp(g_tot - gc_c)
            s_scr[d, h] = s_old * jnp.exp(g_tot) + _dot_tn(k_dec.astype(BF16), v_new.astype(BF16))
            (ob_scr if upper else of_scr)[rows, cols] = o

    def body(c, carry):
        chunk_step(c, 0)
        chunk_step(n_chunks - 1 - c, 1)
        return carry

    lax.fori_loop(0, n_chunks, body, 0)

    if emit_s:
        sfin_ref[...] = s_scr[...]
    for h in range(DN_HEADS):
        cols = slice(h * LANES, (h + 1) * LANES)
        o = of_scr[:, cols] + ob_scr[:, cols]
        o = o * lax.rsqrt(jnp.mean(o * o, axis=-1, keepdims=True) + NORM_EPS) * gain_ref[...]
        o_ref[:, cols] = (o * _silu(dz_ref[:, cols].astype(F32))).astype(o_ref.dtype)


def _delta(dqkv, dz, small, conv_w, a_log, dt_bias, out_gain, *, n_batch, n, tok0, s0=None, emit_s=False):
    b0 = tok0 // n
    n_chunks = n // CHUNK
    n_hd = 2 * DN_HEADS
    tokb = lambda b: (b0 + b, 0)
    const = lambda b: (0, 0)
    alpha = small[tok0:tok0 + n_batch * n, n_hd:2 * n_hd]
    arow = alpha.reshape(n_batch, n_chunks, CHUNK, n_hd).transpose(0, 1, 3, 2)
    pcol = jnp.zeros((2, LANES), F32)
    pcol = pcol.at[0, n_hd:2 * n_hd].set(a_log.reshape(-1)).at[1, n_hd:2 * n_hd].set(dt_bias.reshape(-1))
    prow = jnp.stack([a_log.reshape(-1), dt_bias.reshape(-1)], axis=1)
    args = [dqkv, dz, small, arow, conv_w, pcol, prow, out_gain.reshape(1, LANES)]
    in_specs = [
        pl.BlockSpec((n, DN_CONV_DIM), tokb),
        pl.BlockSpec((n, DN_V_DIM), tokb),
        pl.BlockSpec((n, LANES), tokb),
        pl.BlockSpec((None, n_chunks, n_hd, CHUNK), lambda b: (b, 0, 0, 0)),
        pl.BlockSpec((3, DN_CONV_DIM), const),
        pl.BlockSpec((2, LANES), const),
        pl.BlockSpec((n_hd, 2), const),
        pl.BlockSpec((1, LANES), const),
    ]
    state_spec = pl.BlockSpec((None, 2, DN_HEADS, DN_KEY_DIM, DN_VAL_DIM), lambda b: (b, 0, 0, 0, 0))
    if s0 is not None:
        args.append(s0)
        in_specs.append(state_spec)
    out_shape = [jax.ShapeDtypeStruct((n_batch * n, DN_V_DIM), BF16)]
    out_specs = [pl.BlockSpec((n, DN_V_DIM), lambda b: (b, 0))]
    if emit_s:
        out_shape.append(jax.ShapeDtypeStruct((n_batch, 2, DN_HEADS, DN_KEY_DIM, DN_VAL_DIM), F32))
        out_specs.append(state_spec)
    return pl.pallas_call(
        functools.partial(_delta_kernel, n=n, has_s0=s0 is not None, emit_s=emit_s),
        grid=(n_batch,),
        in_specs=in_specs,
        out_specs=out_specs,
        out_shape=out_shape,
        scratch_shapes=[
            pltpu.VMEM((n, DN_QK_DIM), F32), pltpu.VMEM((n, DN_QK_DIM), F32), pltpu.VMEM((n, DN_V_DIM), F32),
            pltpu.VMEM((n, LANES), F32), pltpu.VMEM((n, LANES), F32),
            pltpu.VMEM((n, DN_V_DIM), F32), pltpu.VMEM((n, DN_V_DIM), F32),
            pltpu.VMEM((2, DN_HEADS, DN_KEY_DIM, DN_VAL_DIM), F32),
        ],
        compiler_params=_cparams(1),
        name="delta_latent" if s0 is not None else "delta_context",
    )(*args)


SSM_GROUP_W = SSM_INNER // SSM_GROUPS
SSM_TILES = SSM_INNER // LANES


def _ssd_kernel(*refs, n, has_s0, emit_s):
    it = iter(refs)
    z_ref, xbc_ref, dtc_ref, dtr_ref = next(it), next(it), next(it), next(it)
    convw_ref, convb_ref, pcol_ref, prow_ref = next(it), next(it), next(it), next(it)
    dskip_ref, gain_ref, expand_ref, tri2_ref = next(it), next(it), next(it), next(it)
    if has_s0:
        s0_ref = next(it)
    o_ref = next(it)
    if emit_s:
        sfin_ref = next(it)
    xs_scr, bc_scr, y_scr, dt_scr, st_scr = (next(it) for _ in range(5))

    n_chunks = n // CHUNK
    def conv_tile(t):
        cols = pl.ds(pl.multiple_of(t * LANES, LANES), LANES)
        return _conv_silu(xbc_ref[:, cols].astype(F32), convw_ref[:, cols], convb_ref[:, cols], n)

    def prep_x(t, carry):
        cols = pl.ds(pl.multiple_of(t * LANES, LANES), LANES)
        x = conv_tile(t)
        xs_scr[:, cols] = x
        y_scr[:, cols] = x * dskip_ref[:, cols]
        return carry

    def prep_bc(t, carry):
        cols = pl.ds(pl.multiple_of(t * LANES, LANES), LANES)
        bc_scr[:, cols] = conv_tile(t + SSM_TILES).astype(BF16)
        return carry

    lax.fori_loop(0, SSM_TILES, prep_x, 0)
    lax.fori_loop(0, 2 * SSM_BC_DIM // LANES, prep_bc, 0)
    pcol = pcol_ref[...]
    dt_scr[...] = _softplus(dtc_ref[...] + pcol[1:2, :])
    neg_a_col = -jnp.exp(pcol[0:1, :])
    for d in range(2):
        for g in range(SSM_GROUPS):
            gs = slice(g * SSM_GROUP_W, (g + 1) * SSM_GROUP_W)
            if has_s0:
                st_scr[d, :, gs] = s0_ref[d, gs, :].T
            else:
                st_scr[d, :, gs] = jnp.zeros((SSM_STATE, SSM_GROUP_W), F32)

    ri = lax.broadcasted_iota(jnp.int32, (CHUNK, CHUNK), 0)
    ci = lax.broadcasted_iota(jnp.int32, (CHUNK, CHUNK), 1)
    lower_b = (ri >= ci).astype(BF16)
    upper_b = (ri <= ci).astype(BF16)
    r2 = lax.broadcasted_iota(jnp.int32, (CHUNK, LANES), 0)
    c2 = lax.broadcasted_iota(jnp.int32, (CHUNK, LANES), 1) & (CHUNK - 1)
    low = lax.broadcasted_iota(jnp.int32, (1, LANES), 1) < SSM_HEAD_DIM

    def chunk_step(c, d):
        upper = d == 1
        r0 = pl.multiple_of(c * CHUNK, CHUNK)
        rows = pl.ds(r0, CHUNK)
        incl2 = (r2 <= c2) if upper else (r2 >= c2)
        dt = dt_scr[rows, :]
        acum = _dot_xr(upper_b if upper else lower_b, dt * neg_a_col)
        a_tot = acum[0:1, :] if upper else acum[CHUNK - 1:CHUNK, :]
        stack = jnp.concatenate(
            [acum, dt, dt * jnp.exp(a_tot - acum), jnp.exp(acum),
             jnp.broadcast_to(jnp.exp(a_tot), (8, LANES))], axis=0)
        s_hi, s_mid, s_lo = _split3(stack)
        prow = prow_ref[d]
        dt_r = _softplus(dtr_ref[c, d] + prow[1])
        acum_r = _dot_xl(dt_r * (-jnp.exp(prow[0])), tri2_ref[1 - d])
        for g in range(SSM_GROUPS):
            gs = slice(g * SSM_GROUP_W, (g + 1) * SSM_GROUP_W)
            e = expand_ref[d, :, gs]
            ex = _dot(s_hi, e) + _dot(s_mid, e) + _dot(s_lo, e)
            acum_e = ex[0:CHUNK]
            dt_e = ex[CHUNK:2 * CHUNK]
            dtd_e = ex[2 * CHUNK:3 * CHUNK]
            eac_e = ex[3 * CHUNK:4 * CHUNK]
            dec_e = ex[4 * CHUNK:4 * CHUNK + 1]
            xs_g = xs_scr[rows, gs]
            bm_g = bc_scr[rows, g * SSM_STATE:(g + 1) * SSM_STATE]
            cm_g = bc_scr[rows, SSM_BC_DIM + g * SSM_STATE:SSM_BC_DIM + (g + 1) * SSM_STATE]
            cb2 = _dot_nt(cm_g, jnp.concatenate([bm_g, bm_g], axis=0))
            x_dt = xs_g * dt_e
            y_parts = []
            for tt in range(SSM_GROUP_W // LANES):
                t = g * (SSM_GROUP_W // LANES) + tt
                ls = slice(tt * LANES, (tt + 1) * LANES)
                lmat = jnp.exp(jnp.where(incl2, acum_e[:, ls] - acum_r[t:t + 1, :], NEG_BIG))
                xt = x_dt[:, ls]
                bd = jnp.concatenate([jnp.where(low, xt, 0.0), jnp.where(low, 0.0, xt)], axis=0)
                y_parts.append(_dot((cb2 * lmat).astype(BF16), bd.astype(BF16)))
            st = st_scr[d, :, gs]
            y_off = _dot(cm_g, st.astype(BF16)) * eac_e
            st_scr[d, :, gs] = st * dec_e + _dot_tn(bm_g, (xs_g * dtd_e).astype(BF16))
            y_scr[rows, gs] = y_scr[rows, gs] + jnp.concatenate(y_parts, axis=1) + y_off

    def body(c, carry):
        chunk_step(c, 0)
        chunk_step(n_chunks - 1 - c, 1)
        return carry

    lax.fori_loop(0, n_chunks, body, 0)

    if emit_s:
        for d in range(2):
            for g in range(SSM_GROUPS):
                gs = slice(g * SSM_GROUP_W, (g + 1) * SSM_GROUP_W)
                sfin_ref[d, gs, :] = st_scr[d, :, gs].T
    row_tile = 128

    def finish(r, carry):
        rows = pl.ds(pl.multiple_of(r * row_tile, row_tile), row_tile)
        y = y_scr[rows, :] * _silu(z_ref[rows, :].astype(F32))
        y = y * lax.rsqrt(jnp.mean(y * y, axis=-1, keepdims=True) + NORM_EPS) * gain_ref[...]
        o_ref[rows, :] = y.astype(o_ref.dtype)
        return carry

    lax.fori_loop(0, n // row_tile, finish, 0)


def _ssd(z, xbc, dt_raw, conv_w, conv_b, a_log, dt_bias, d_skip, out_gain, *, n_batch, n, tok0,
         s0=None, emit_s=False):
    b0 = tok0 // n
    n_chunks = n // CHUNK
    tokb = lambda b: (b0 + b, 0)
    const = lambda b: (0, 0)
    const3 = lambda b: (0, 0, 0)
    dtr = dt_raw[tok0:tok0 + n_batch * n, 0:2 * SSM_HEADS]
    dtr = dtr.reshape(n_batch, n_chunks, CHUNK, 2, SSM_TILES, 2).transpose(0, 1, 3, 4, 5, 2)
    dtr = dtr.reshape(n_batch, n_chunks, 2, SSM_TILES, LANES)
    pcol = jnp.zeros((2, LANES), F32)
    pcol = pcol.at[0, 0:2 * SSM_HEADS].set(a_log.reshape(-1)).at[1, 0:2 * SSM_HEADS].set(dt_bias.reshape(-1))
    to_rows = lambda p: jnp.repeat(p.reshape(2, SSM_TILES, 2), CHUNK, axis=-1)
    prow = jnp.stack([to_rows(a_log), to_rows(dt_bias)], axis=1)
    expand = np.zeros((2, LANES, SSM_INNER), np.float32)
    for d in range(2):
        for h in range(SSM_HEADS):
            expand[d, d * SSM_HEADS + h, h * SSM_HEAD_DIM:(h + 1) * SSM_HEAD_DIM] = 1.0
    tri = np.tril(np.ones((CHUNK, CHUNK), np.float32))
    tri2 = np.stack([np.kron(np.eye(2), tri), np.kron(np.eye(2), tri.T)])
    args = [z, xbc, dt_raw, dtr, conv_w, conv_b.reshape(1, SSM_CONV_DIM), pcol, prow,
            jnp.repeat(d_skip, SSM_HEAD_DIM).reshape(1, SSM_INNER), out_gain.reshape(1, SSM_INNER),
            jnp.asarray(expand, BF16), jnp.asarray(tri2, BF16)]
    big = dict(pipeline_mode=pl.Buffered(1)) if n > SEQ else {}
    in_specs = [
        pl.BlockSpec((n, SSM_INNER), tokb, **big),
        pl.BlockSpec((n, SSM_CONV_DIM), tokb, **big),
        pl.BlockSpec((n, LANES), tokb),
        pl.BlockSpec((None, n_chunks, 2, SSM_TILES, LANES), lambda b: (b, 0, 0, 0, 0)),
        pl.BlockSpec((3, SSM_CONV_DIM), const),
        pl.BlockSpec((1, SSM_CONV_DIM), const),
        pl.BlockSpec((2, LANES), const),
        pl.BlockSpec((2, 2, SSM_TILES, LANES), lambda b: (0, 0, 0, 0)),
        pl.BlockSpec((1, SSM_INNER), const),
        pl.BlockSpec((1, SSM_INNER), const),
        pl.BlockSpec((2, LANES, SSM_INNER), const3),
        pl.BlockSpec((2, LANES, LANES), const3),
    ]
    state_spec = pl.BlockSpec((None, 2, SSM_INNER, SSM_STATE), lambda b: (b, 0, 0, 0))
    if s0 is not None:
        args.append(s0.reshape(n_batch, 2, SSM_INNER, SSM_STATE))
        in_specs.append(state_spec)
    out_shape = [jax.ShapeDtypeStruct((n_batch * n, SSM_INNER), BF16)]
    out_specs = [pl.BlockSpec((n, SSM_INNER), lambda b: (b, 0))]
    if emit_s:
        out_shape.append(jax.ShapeDtypeStruct((n_batch, 2, SSM_INNER, SSM_STATE), F32))
        out_specs.append(state_spec)
    return pl.pallas_call(
        functools.partial(_ssd_kernel, n=n, has_s0=s0 is not None, emit_s=emit_s),
        grid=(n_batch,),
        in_specs=in_specs,
        out_specs=out_specs,
        out_shape=out_shape,
        scratch_shapes=[
            pltpu.VMEM((n, SSM_INNER), F32), pltpu.VMEM((n, 2 * SSM_BC_DIM), BF16),
            pltpu.VMEM((n, SSM_INNER), F32), pltpu.VMEM((n, LANES), F32),
            pltpu.VMEM((2, SSM_STATE, SSM_INNER), F32),
        ],
        compiler_params=_cparams(1),
        name="ssd_latent" if s0 is not None else "ssd_context",
    )(*args)


DN_CAT = DN_HEADS * CHUNK
DN_EXPAND_W = DN_CAT + DN_QK_DIM


def _rep_rows(x, k):
    return jnp.concatenate([x] * k, axis=0)


def _rep_lanes(x, k):
    return jnp.concatenate([x] * k, axis=1)


def _to_stack(x):
    return jnp.concatenate([x[:, h * LANES:(h + 1) * LANES] for h in range(DN_HEADS)], axis=0)


def _delta_kernel(*refs, n, has_s0, emit_s):
    it = iter(refs)
    dqkv_ref, dz_ref, small_ref, convw_ref, pcol_ref, gain_ref, expand_ref = (next(it) for _ in range(7))
    if has_s0:
        s0_ref = next(it)
    o_ref = next(it)
    if emit_s:
        sfin_ref = next(it)
    q_scr, k_scr, v_scr, g_scr, b_scr, of_scr, ob_scr, s_scr = (next(it) for _ in range(8))

    n_chunks = n // CHUNK
    n_hd = 2 * DN_HEADS
    for h in range(DN_HEADS):
        for part, scr in ((0, q_scr), (1, k_scr), (2, v_scr)):
            c0 = part * DN_QK_DIM + h * DN_KEY_DIM
            x = _conv_silu(dqkv_ref[:, c0:c0 + LANES].astype(F32), convw_ref[:, c0:c0 + LANES], None, n)
            if part < 2:
                x = x * lax.rsqrt(jnp.sum(x * x, axis=-1, keepdims=True) + NORM_EPS)
            if part == 0:
                x = x * (DN_KEY_DIM ** -0.5)
            scr[:, h * LANES:(h + 1) * LANES] = x
    small = small_ref[...]
    pcol = pcol_ref[...]
    lane = lax.broadcasted_iota(jnp.int32, (1, LANES), 1)
    b_scr[...] = jnp.where(lane < n_hd, _sigmoid(small), 0.0)
    g_scr[...] = jnp.where(lane < n_hd, 0.0, jnp.where(
        lane < 2 * n_hd, -jnp.exp(pcol[0:1, :]) * _softplus(small + pcol[1:2, :]), 0.0))
    if has_s0:
        s_scr[...] = s0_ref[...]
    else:
        s_scr[...] = jnp.zeros(s_scr.shape, F32)

    ri = lax.broadcasted_iota(jnp.int32, (CHUNK, CHUNK), 0)
    ci = lax.broadcasted_iota(jnp.int32, (CHUNK, CHUNK), 1)
    tri_b = ((ri >= ci).astype(BF16), (ri <= ci).astype(BF16))
    rc = lax.broadcasted_iota(jnp.int32, (CHUNK, DN_CAT), 0)
    cc = lax.broadcasted_iota(jnp.int32, (CHUNK, DN_CAT), 1) & (CHUNK - 1)
    eye_cat = jnp.where(rc == cc, 1.0, 0.0)
    r4 = lax.broadcasted_iota(jnp.int32, (DN_CAT, DN_CAT), 0)
    c4 = lax.broadcasted_iota(jnp.int32, (DN_CAT, DN_CAT), 1)
    blk_sq = jnp.where(_shr(r4, 6) == _shr(c4, 6), 1.0, 0.0)
    r5 = lax.broadcasted_iota(jnp.int32, (DN_CAT, DN_QK_DIM), 0)
    c5 = lax.broadcasted_iota(jnp.int32, (DN_CAT, DN_QK_DIM), 1)
    blk_wide = jnp.where(_shr(r5, 6) == _shr(c5, 7), 1.0, 0.0)

    def block_diag(x_cat):
        return (_rep_rows(x_cat, DN_HEADS) * blk_sq).astype(BF16)

    def block_wide(x_stack):
        return _rep_lanes(x_stack, DN_HEADS) * blk_wide

    def chunk_step(c, d):
        upper = d == 1
        rows = pl.ds(pl.multiple_of(c * CHUNK, CHUNK), CHUNK)
        incl = (rc <= cc) if upper else (rc >= cc)
        strict_f = jnp.where((rc < cc) if upper else (rc > cc), 1.0, 0.0)
        off_masks = []
        for lvl in range(int(math.log2(CHUNK))):
            same_pair = _shr(rc, lvl + 1) == _shr(cc, lvl + 1)
            half_r, half_c = _shr(rc, lvl), _shr(cc, lvl)
            side = (half_r < half_c) if upper else (half_r > half_c)
            off_masks.append(jnp.where(same_pair, jnp.where(side, 1.0, 0.0), 0.0))
        gc_all = _dot_xr(tri_b[d], g_scr[rows, :])
        ex = _dot_xl(jnp.concatenate([gc_all, b_scr[rows, :]], axis=0), expand_ref[d])
        gc_cat = ex[0:CHUNK, 0:DN_CAT]
        gc_wide = ex[0:CHUNK, DN_CAT:]
        beta_wide = ex[CHUNK:2 * CHUNK, DN_CAT:]
        gr_cat = jnp.sum(gc_cat * eye_cat, axis=0, keepdims=True)
        decay = jnp.exp(jnp.where(incl, gc_cat - gr_cat, NEG_BIG))
        g_tot = gc_wide[0:1, :] if upper else gc_wide[CHUNK - 1:CHUNK, :]
        egc = jnp.exp(gc_wide)
        q = q_scr[rows, :]
        k = k_scr[rows, :]
        v = v_scr[rows, :]
        kb = k * beta_wide
        k_bd = (_rep_rows(k, DN_HEADS) * blk_wide).astype(BF16)
        a = _dot_nt(jnp.concatenate([kb, q], axis=0).astype(BF16), k_bd)
        m = a[0:CHUNK] * decay * strict_f
        qk = a[CHUNK:2 * CHUNK] * decay
        p = eye_cat - m * off_masks[0]
        for off in off_masks[1:]:
            t1 = _dot(p.astype(BF16), block_diag(m * off))
            p = p - _dot(t1.astype(BF16), block_diag(p))
        rhs = jnp.concatenate([_to_stack(v * beta_wide), _to_stack(kb * egc)], axis=1).astype(BF16)
        uw = _dot(block_diag(p), rhs)
        u = uw[:, 0:DN_VAL_DIM]
        w = uw[:, DN_VAL_DIM:]
        s_old = s_scr[d]
        lhs = jnp.concatenate([block_wide(w), block_wide(_to_stack(q * egc))], axis=0).astype(BF16)
        wq = _dot(lhs, s_old.astype(BF16))
        v_new = u - wq[0:DN_CAT]
        o = wq[DN_CAT:2 * DN_CAT] + _dot(block_diag(qk), v_new.astype(BF16))
        k_dec = _to_stack(k * jnp.exp(g_tot - gc_wide))
        dec = jnp.concatenate(
            [jnp.broadcast_to(jnp.exp(g_tot[:, h * LANES:(h + 1) * LANES]), (DN_KEY_DIM, LANES))
             for h in range(DN_HEADS)], axis=0)
        s_scr[d] = s_old * dec + _dot_tn(block_wide(k_dec).astype(BF16), v_new.astype(BF16))
        o_scr = ob_scr if upper else of_scr
        for h in range(DN_HEADS):
            o_scr[rows, h * LANES:(h + 1) * LANES] = o[h * CHUNK:(h + 1) * CHUNK]

    def body(c, carry):
        chunk_step(c, 0)
        chunk_step(n_chunks - 1 - c, 1)
        return carry

    lax.fori_loop(0, n_chunks, body, 0)

    if emit_s:
        sfin_ref[...] = s_scr[...]
    for h in range(DN_HEADS):
        cols = slice(h * LANES, (h + 1) * LANES)
        o = of_scr[:, cols] + ob_scr[:, cols]
        o = o * lax.rsqrt(jnp.mean(o * o, axis=-1, keepdims=True) + NORM_EPS) * gain_ref[...]
        o_ref[:, cols] = (o * _silu(dz_ref[:, cols].astype(F32))).astype(o_ref.dtype)


def _delta(dqkv, dz, small, conv_w, a_log, dt_bias, out_gain, *, n_batch, n, tok0, s0=None, emit_s=False):
    b0 = tok0 // n
    n_hd = 2 * DN_HEADS
    tokb = lambda b: (b0 + b, 0)
    const = lambda b: (0, 0)
    pcol = jnp.zeros((2, LANES), F32)
    pcol = pcol.at[0, n_hd:2 * n_hd].set(a_log.reshape(-1)).at[1, n_hd:2 * n_hd].set(dt_bias.reshape(-1))
    expand = np.zeros((2, LANES, DN_EXPAND_W), np.float32)
    for d in range(2):
        for h in range(DN_HEADS):
            for src in (d * DN_HEADS + h, n_hd + d * DN_HEADS + h):
                expand[d, src, h * CHUNK:(h + 1) * CHUNK] = 1.0
                expand[d, src, DN_CAT + h * LANES:DN_CAT + (h + 1) * LANES] = 1.0
    args = [dqkv, dz, small, conv_w, pcol, out_gain.reshape(1, LANES), jnp.asarray(expand, BF16)]
    in_specs = [
        pl.BlockSpec((n, DN_CONV_DIM), tokb),
        pl.BlockSpec((n, DN_V_DIM), tokb),
        pl.BlockSpec((n, LANES), tokb),
        pl.BlockSpec((3, DN_CONV_DIM), const),
        pl.BlockSpec((2, LANES), const),
        pl.BlockSpec((1, LANES), const),
        pl.BlockSpec((2, LANES, DN_EXPAND_W), lambda b: (0, 0, 0)),
    ]
    state_shape = (2, DN_HEADS * DN_KEY_DIM, DN_VAL_DIM)
    state_spec = pl.BlockSpec((None,) + state_shape, lambda b: (b, 0, 0, 0))
    if s0 is not None:
        args.append(s0.reshape((n_batch,) + state_shape))
        in_specs.append(state_spec)
    out_shape = [jax.ShapeDtypeStruct((n_batch * n, DN_V_DIM), BF16)]
    out_specs = [pl.BlockSpec((n, DN_V_DIM), lambda b: (b, 0))]
    if emit_s:
        out_shape.append(jax.ShapeDtypeStruct((n_batch,) + state_shape, F32))
        out_specs.append(state_spec)
    return pl.pallas_call(
        functools.partial(_delta_kernel, n=n, has_s0=s0 is not None, emit_s=emit_s),
        grid=(n_batch,),
        in_specs=in_specs,
        out_specs=out_specs,
        out_shape=out_shape,
        scratch_shapes=[
            pltpu.VMEM((n, DN_QK_DIM), F32), pltpu.VMEM((n, DN_QK_DIM), F32), pltpu.VMEM((n, DN_V_DIM), F32),
            pltpu.VMEM((n, LANES), F32), pltpu.VMEM((n, LANES), F32),
            pltpu.VMEM((n, DN_V_DIM), F32), pltpu.VMEM((n, DN_V_DIM), F32),
            pltpu.VMEM(state_shape, F32),
        ],
        compiler_params=_cparams(1),
        name="delta_latent" if s0 is not None else "delta_context",
    )(*args)


SSD_CHUNK = 256
HEADS_PER_TILE = LANES // SSM_HEAD_DIM


def _ssd_kernel(*refs, n, has_s0, emit_s):
    it = iter(refs)
    z_ref, xbc_ref, dtc_ref, convw_ref, convb_ref, pcol_ref = (next(it) for _ in range(6))
    dskip_ref, gain_ref, expand_ref = next(it), next(it), next(it)
    if has_s0:
        s0_ref = next(it)
    o_ref = next(it)
    if emit_s:
        sfin_ref = next(it)
    xs_scr, bc_scr, y_scr, dt_scr, st_scr = (next(it) for _ in range(5))

    q_len = SSD_CHUNK
    n_chunks = n // q_len
    carry_state = has_s0 or n_chunks > 1

    def conv_tile(t):
        cols = pl.ds(pl.multiple_of(t * LANES, LANES), LANES)
        return _conv_silu(xbc_ref[:, cols].astype(F32), convw_ref[:, cols], convb_ref[:, cols], n)

    def prep_x(t, carry):
        cols = pl.ds(pl.multiple_of(t * LANES, LANES), LANES)
        x = conv_tile(t)
        xs_scr[:, cols] = x
        y_scr[:, cols] = x * dskip_ref[:, cols]
        return carry

    def prep_bc(t, carry):
        cols = pl.ds(pl.multiple_of(t * LANES, LANES), LANES)
        bc_scr[:, cols] = conv_tile(t + SSM_INNER // LANES).astype(BF16)
        return carry

    lax.fori_loop(0, SSM_INNER // LANES, prep_x, 0)
    lax.fori_loop(0, 2 * SSM_BC_DIM // LANES, prep_bc, 0)
    pcol = pcol_ref[...]
    dt_scr[...] = _softplus(dtc_ref[...] + pcol[1:2, :])
    neg_a_col = -jnp.exp(pcol[0:1, :])
    for d in range(2):
        for g in range(SSM_GROUPS):
            gs = slice(g * SSM_GROUP_W, (g + 1) * SSM_GROUP_W)
            if has_s0:
                st_scr[d, :, gs] = s0_ref[d, gs, :].T
            else:
                st_scr[d, :, gs] = jnp.zeros((SSM_STATE, SSM_GROUP_W), F32)

    ri = lax.broadcasted_iota(jnp.int32, (q_len, q_len), 0)
    ci = lax.broadcasted_iota(jnp.int32, (q_len, q_len), 1)
    eye_b = (ri == ci).astype(BF16)
    tri_b = ((ri >= ci).astype(BF16), (ri <= ci).astype(BF16))
    low = lax.broadcasted_iota(jnp.int32, (1, LANES), 1) < SSM_HEAD_DIM
    heads_per_group = SSM_HEADS // SSM_GROUPS

    def chunk_step(c, d):
        upper = d == 1
        rows = pl.ds(pl.multiple_of(c * q_len, q_len), q_len)
        incl = (ri <= ci) if upper else (ri >= ci)
        dt = dt_scr[rows, :]
        acum = _dot_xr(tri_b[d], dt * neg_a_col)
        a_tot = acum[0:1, :] if upper else acum[q_len - 1:q_len, :]
        hi, mid, lo = _split3(jnp.concatenate([acum, dt], axis=1))
        rowf = _dot_tn(hi, eye_b) + _dot_tn(mid, eye_b) + _dot_tn(lo, eye_b)
        acum_r = rowf[0:LANES]
        dt_r = rowf[LANES:2 * LANES]
        tot_r = acum_r[:, 0:1] if upper else acum_r[:, q_len - 1:q_len]
        w_r = dt_r * jnp.exp(tot_r - acum_r)
        dec_e = _dot_xl(jnp.broadcast_to(jnp.exp(a_tot), (8, LANES)), expand_ref[d])[0:1, :]
        eac = jnp.exp(acum)
        for g in range(SSM_GROUPS):
            bm_g = bc_scr[rows, g * SSM_STATE:(g + 1) * SSM_STATE]
            cm_g = bc_scr[rows, SSM_BC_DIM + g * SSM_STATE:SSM_BC_DIM + (g + 1) * SSM_STATE]
            cb = _dot_nt(cm_g, bm_g)
            bm_t = bm_g.astype(F32).T
            for tt in range(heads_per_group // HEADS_PER_TILE):
                h0 = g * heads_per_group + tt * HEADS_PER_TILE
                ls = slice(g * SSM_GROUP_W + tt * LANES, g * SSM_GROUP_W + (tt + 1) * LANES)
                w_parts, b_parts = [], []
                for hh in range(HEADS_PER_TILE):
                    l = d * SSM_HEADS + h0 + hh
                    lmat = jnp.exp(jnp.where(incl, acum[:, l:l + 1] - acum_r[l:l + 1, :], NEG_BIG))
                    w_parts.append(cb * (lmat * dt_r[l:l + 1, :]))
                    b_parts.append(bm_t * w_r[l:l + 1, :])
                lhs = jnp.concatenate([jnp.concatenate(w_parts, axis=1),
                                       jnp.concatenate(b_parts, axis=1)], axis=0).astype(BF16)
                xt = xs_scr[rows, ls]
                bd = jnp.concatenate([jnp.where(low, xt, 0.0), jnp.where(low, 0.0, xt)], axis=0).astype(BF16)
                res = _dot(lhs, bd)
                y_new = y_scr[rows, ls] + res[0:q_len]
                st_inc = res[q_len:q_len + SSM_STATE]
                if carry_state:
                    st = st_scr[d, :, ls]
                    l0 = d * SSM_HEADS + h0
                    scale = jnp.where(low, eac[:, l0:l0 + 1], eac[:, l0 + 1:l0 + 2])
                    y_new = y_new + _dot(cm_g, st.astype(BF16)) * scale
                    st_inc = st * dec_e[:, ls] + st_inc
                y_scr[rows, ls] = y_new
                st_scr[d, :, ls] = st_inc

    if n_chunks == 1:
        chunk_step(0, 0)
        chunk_step(0, 1)
    else:
        def body(c, carry):
            chunk_step(c, 0)
            chunk_step(n_chunks - 1 - c, 1)
            return carry

        lax.fori_loop(0, n_chunks, body, 0)

    if emit_s:
        for d in range(2):
            for g in range(SSM_GROUPS):
                gs = slice(g * SSM_GROUP_W, (g + 1) * SSM_GROUP_W)
                sfin_ref[d, gs, :] = st_scr[d, :, gs].T
    row_tile = 128

    def finish(r, carry):
        rows = pl.ds(pl.multiple_of(r * row_tile, row_tile), row_tile)
        y = y_scr[rows, :] * _silu(z_ref[rows, :].astype(F32))
        y = y * lax.rsqrt(jnp.mean(y * y, axis=-1, keepdims=True) + NORM_EPS) * gain_ref[...]
        o_ref[rows, :] = y.astype(o_ref.dtype)
        return carry

    lax.fori_loop(0, n // row_tile, finish, 0)


def _ssd(z, xbc, dt_raw, conv_w, conv_b, a_log, dt_bias, d_skip, out_gain, *, n_batch, n, tok0,
         s0=None, emit_s=False):
    b0 = tok0 // n
    tokb = lambda b: (b0 + b, 0)
    const = lambda b: (0, 0)
    pcol = jnp.zeros((2, LANES), F32)
    pcol = pcol.at[0, 0:2 * SSM_HEADS].set(a_log.reshape(-1)).at[1, 0:2 * SSM_HEADS].set(dt_bias.reshape(-1))
    expand = np.zeros((2, LANES, SSM_INNER), np.float32)
    for d in range(2):
        for h in range(SSM_HEADS):
            expand[d, d * SSM_HEADS + h, h * SSM_HEAD_DIM:(h + 1) * SSM_HEAD_DIM] = 1.0
    args = [z, xbc, dt_raw, conv_w, conv_b.reshape(1, SSM_CONV_DIM), pcol,
            jnp.repeat(d_skip, SSM_HEAD_DIM).reshape(1, SSM_INNER), out_gain.reshape(1, SSM_INNER),
            jnp.asarray(expand, BF16)]
    big = dict(pipeline_mode=pl.Buffered(1)) if n > SEQ else {}
    in_specs = [
        pl.BlockSpec((n, SSM_INNER), tokb, **big),
        pl.BlockSpec((n, SSM_CONV_DIM), tokb, **big),
        pl.BlockSpec((n, LANES), tokb),
        pl.BlockSpec((3, SSM_CONV_DIM), const),
        pl.BlockSpec((1, SSM_CONV_DIM), const),
        pl.BlockSpec((2, LANES), const),
        pl.BlockSpec((1, SSM_INNER), const),
        pl.BlockSpec((1, SSM_INNER), const),
        pl.BlockSpec((2, LANES, SSM_INNER), lambda b: (0, 0, 0)),
    ]
    state_spec = pl.BlockSpec((None, 2, SSM_INNER, SSM_STATE), lambda b: (b, 0, 0, 0))
    if s0 is not None:
        args.append(s0.reshape(n_batch, 2, SSM_INNER, SSM_STATE))
        in_specs.append(state_spec)
    out_shape = [jax.ShapeDtypeStruct((n_batch * n, SSM_INNER), BF16)]
    out_specs = [pl.BlockSpec((n, SSM_INNER), lambda b: (b, 0))]
    if emit_s:
        out_shape.append(jax.ShapeDtypeStruct((n_batch, 2, SSM_INNER, SSM_STATE), F32))
        out_specs.append(state_spec)
    return pl.pallas_call(
        functools.partial(_ssd_kernel, n=n, has_s0=s0 is not None, emit_s=emit_s),
        grid=(n_batch,),
        in_specs=in_specs,
        out_specs=out_specs,
        out_shape=out_shape,
        scratch_shapes=[
            pltpu.VMEM((n, SSM_INNER), F32), pltpu.VMEM((n, 2 * SSM_BC_DIM), BF16),
            pltpu.VMEM((n, SSM_INNER), F32), pltpu.VMEM((n, LANES), F32),
            pltpu.VMEM((2, SSM_STATE, SSM_INNER), F32),
        ],
        compiler_params=_cparams(1),
        name="ssd_latent" if s0 is not None else "ssd_context",
    )(*args)


EVEN_MAIN = ATT_Q_DIM + 2 * ATT_KV_DIM + DN_CONV_DIM + DN_V_DIM
EVEN_SPLITS = ((0, ATT_Q_DIM), (ATT_Q_DIM, ATT_Q_DIM + ATT_KV_DIM),
               (ATT_Q_DIM + ATT_KV_DIM, ATT_Q_DIM + 2 * ATT_KV_DIM),
               (ATT_Q_DIM + 2 * ATT_KV_DIM, ATT_Q_DIM + 2 * ATT_KV_DIM + DN_CONV_DIM),
               (ATT_Q_DIM + 2 * ATT_KV_DIM + DN_CONV_DIM, EVEN_MAIN), (EVEN_MAIN, EVEN_MAIN + LANES))
EVEN_DTYPES = (F32, F32, F32, BF16, BF16, F32)
ODD_MAIN = SSM_INNER + SSM_CONV_DIM
ODD_SPLITS = ((0, SSM_INNER), (SSM_INNER, ODD_MAIN), (ODD_MAIN, ODD_MAIN + LANES))
ODD_DTYPES = (BF16, BF16, F32)


def _pad_cols(w, n_total):
    return jnp.pad(w, ((0, 0), (0, n_total - w.shape[1]))).astype(BF16)


def kernel(x_prompt, x_sample, c, cache_attn_k, cache_attn_v, state_delta, state_ssm, c_ctx, norm_mix_g, norm_mlp_g, w_mod, b_mod, w_mlp_in, w_mlp_out, w_in_even, attn_q_norm_g, attn_k_norm_g, delta_conv_w, delta_a_log, delta_dt_bias, delta_norm_g, w_out_even, w_in_odd, ssm_conv_w, ssm_conv_b, ssm_a_log, ssm_dt_bias, ssm_d, ssm_norm_g, w_out_odd, final_norm_g):
    x = jnp.concatenate([x_prompt.reshape(N_PROMPT_TOK, D_MODEL), x_sample.reshape(N_SAMPLE_TOK, D_MODEL)], axis=0)
    cond = jnp.zeros((N_COND, D_MODEL), F32).at[0].set(c_ctx).at[1:1 + DEC_BATCH].set(c)
    mod = _modulation(cond, w_mod, b_mod).reshape(DEPTH, N_COND, 1, 6 * D_MODEL)

    ks, vs, sds, sss = [], [], [], []
    for layer in range(DEPTH):
        j = layer // 2
        if layer % 2 == 0:
            w_in = _pad_cols(w_in_even[j], EVEN_MAIN + LANES)
            q, k, v, dqkv, dz, small = _inproj(x, norm_mix_g[layer], mod[layer], w_in, EVEN_SPLITS,
                                               EVEN_DTYPES, "inproj_even")
            ctx_k = cache_attn_k[:, j].reshape(DEC_BATCH, PAST_LEN, ATT_KV_DIM)
            ctx_v = cache_attn_v[:, j].reshape(DEC_BATCH, PAST_LEN, ATT_KV_DIM)
            o_att_p, k_norm = _attention(q, k, v, attn_q_norm_g[j], attn_k_norm_g[j],
                                         n_batch=BATCH, n=SEQ, tok0=0, emit_k=True)
            (o_att_s,) = _attention(q, k, v, attn_q_norm_g[j], attn_k_norm_g[j],
                                    n_batch=DEC_BATCH, n=DEC_SEQ, tok0=N_PROMPT_TOK, ctx=(ctx_k, ctx_v))
            dn_args = (dqkv, dz, small, delta_conv_w[j], delta_a_log[j], delta_dt_bias[j], delta_norm_g[j])
            o_dn_p, s_delta = _delta(*dn_args, n_batch=BATCH, n=SEQ, tok0=0, emit_s=True)
            (o_dn_s,) = _delta(*dn_args, n_batch=DEC_BATCH, n=DEC_SEQ, tok0=N_PROMPT_TOK, s0=state_delta[:, j])
            y = jnp.concatenate([jnp.concatenate([o_att_p, o_dn_p], axis=1),
                                 jnp.concatenate([o_att_s, o_dn_s], axis=1)], axis=0)
            w_out = w_out_even[j].astype(BF16)
            ks.append(k_norm.reshape(BATCH, SEQ, ATT_KV_HEADS, ATT_HEAD_DIM))
            vs.append(v[:N_PROMPT_TOK].reshape(BATCH, SEQ, ATT_KV_HEADS, ATT_HEAD_DIM))
            sds.append(s_delta.reshape(BATCH, 2, DN_HEADS, DN_KEY_DIM, DN_VAL_DIM))
        else:
            w_in = _pad_cols(w_in_odd[j], ODD_MAIN + LANES)
            z, xbc, dt_raw = _inproj(x, norm_mix_g[layer], mod[layer], w_in, ODD_SPLITS, ODD_DTYPES, "inproj_odd")
            ssd_args = (z, xbc, dt_raw, ssm_conv_w[j], ssm_conv_b[j], ssm_a_log[j], ssm_dt_bias[j], ssm_d[j],
                        ssm_norm_g[j])
            y_p, s_ssm = _ssd(*ssd_args, n_batch=BATCH, n=SEQ, tok0=0, emit_s=True)
            (y_s,) = _ssd(*ssd_args, n_batch=DEC_BATCH, n=DEC_SEQ, tok0=N_PROMPT_TOK, s0=state_ssm[:, j])
            y = jnp.concatenate([y_p, y_s], axis=0)
            w_out = w_out_odd[j].astype(BF16)
            sss.append(s_ssm.reshape(BATCH, 2, SSM_HEADS, SSM_HEAD_DIM, SSM_STATE))
        x = _outproj_mlp(x, y, w_out, norm_mlp_g[layer], mod[layer], w_mlp_in[layer].astype(BF16),
                         w_mlp_out[layer].astype(BF16), "outproj_mlp_even" if layer % 2 == 0 else "outproj_mlp_odd")

    out = _final_norm(x, final_norm_g)
    y_prompt = out[:N_PROMPT_TOK].reshape(BATCH, SEQ, D_MODEL)
    y_sample = out[N_PROMPT_TOK:].reshape(DEC_BATCH, DEC_SEQ, D_MODEL)
    return (y_prompt, y_sample, jnp.stack(ks, axis=1), jnp.stack(vs, axis=1),
            jnp.stack(sds, axis=1), jnp.stack(sss, axis=1))
```

```python
import functools
import math

import jax
import jax.numpy as jnp
import numpy as np
from jax import lax
from jax.experimental import pallas as pl
from jax.experimental.pallas import tpu as pltpu

F32 = jnp.float32
BF16 = jnp.bfloat16

D_MODEL = 1024
BATCH = 32
SEQ = 256
DEPTH = 4
DEC_BATCH = 4
DEC_SEQ = 1024
PAST_LEN = 512
GRID_W = 64
N_EVEN = (DEPTH + 1) // 2
N_ODD = DEPTH // 2
ATT_HEAD_DIM = 64
ATT_HEADS = 8
ATT_KV_HEADS = 2
ATT_Q_DIM = ATT_HEADS * ATT_HEAD_DIM
ATT_KV_DIM = ATT_KV_HEADS * ATT_HEAD_DIM
DN_KEY_DIM = 128
DN_VAL_DIM = 128
DN_HEADS = 4
DN_QK_DIM = DN_HEADS * DN_KEY_DIM
DN_V_DIM = DN_HEADS * DN_VAL_DIM
DN_CONV_DIM = 2 * DN_QK_DIM + DN_V_DIM
SSM_INNER = 2 * D_MODEL
SSM_HEAD_DIM = 64
SSM_HEADS = SSM_INNER // SSM_HEAD_DIM
SSM_GROUPS = 8
SSM_STATE = 128
SSM_BC_DIM = SSM_GROUPS * SSM_STATE
SSM_CONV_DIM = SSM_INNER + 2 * SSM_BC_DIM
MLP_HIDDEN = 4 * D_MODEL
CHUNK = 64
ROPE_THETA = 10000.0
NORM_EPS = 1e-6

LANES = 128
N_PROMPT_TOK = BATCH * SEQ
N_SAMPLE_TOK = DEC_BATCH * DEC_SEQ
N_TOK = N_PROMPT_TOK + N_SAMPLE_TOK
N_COND = 8
TOKEN_TILE = 512
MLP_TILE = 256
N_PROMPT_TILES = N_PROMPT_TOK // TOKEN_TILE
TILES_PER_DEC_SEQ = DEC_SEQ // TOKEN_TILE
COL_CHUNK = 512
NEG_BIG = -1e30
VMEM_LIMIT = 56 * 1024 * 1024


def _cparams(n_grid):
    return pltpu.CompilerParams(dimension_semantics=("arbitrary",) * n_grid,
                                vmem_limit_bytes=VMEM_LIMIT)


def _silu(x):
    return x / (1.0 + jnp.exp(-x))


def _sigmoid(x):
    return 1.0 / (1.0 + jnp.exp(-x))


def _softplus(x):
    return jnp.maximum(x, 0.0) + jnp.log1p(jnp.exp(-jnp.abs(x)))


def _dot(a, b):
    return jnp.dot(a, b, preferred_element_type=F32)


def _dot_nt(a, b):
    return lax.dot_general(a, b, (((1,), (1,)), ((), ())), preferred_element_type=F32)


def _dot_tn(a, b):
    return lax.dot_general(a, b, (((0,), (0,)), ((), ())), preferred_element_type=F32)


def _split3(a):
    hi = a.astype(BF16)
    r = a - hi.astype(F32)
    mid = r.astype(BF16)
    lo = (r - mid.astype(F32)).astype(BF16)
    return hi, mid, lo


def _dot_xl(a, b_exact):
    hi, mid, lo = _split3(a)
    return _dot(hi, b_exact) + _dot(mid, b_exact) + _dot(lo, b_exact)


def _dot_xr(a_exact, b):
    hi, mid, lo = _split3(b)
    return _dot(a_exact, hi) + _dot(a_exact, mid) + _dot(a_exact, lo)


def _mod_norm(x, gain, shift, scale):
    y = x * lax.rsqrt(jnp.mean(x * x, axis=-1, keepdims=True) + NORM_EPS) * gain
    return y * (1.0 + scale) + shift


def _mod_row(i):
    return jnp.where(i < N_PROMPT_TILES, 0, 1 + (i - N_PROMPT_TILES) // TILES_PER_DEC_SEQ)


def _shr(i, k):
    return lax.shift_right_logical(i, jnp.int32(k))


def _rep_rows(x, k):
    return jnp.concatenate([x] * k, axis=0)


def _rep_lanes(x, k):
    return jnp.concatenate([x] * k, axis=1)


def _cast_kernel(x_ref, o_ref):
    o_ref[...] = x_ref[...].astype(o_ref.dtype)


def _cast_bf16(w):
    n_l, n_r, n_c = w.shape
    rows = 256
    return pl.pallas_call(
        _cast_kernel,
        grid=(n_l, n_r // rows),
        in_specs=[pl.BlockSpec((None, rows, n_c), lambda l, r: (l, r, 0))],
        out_specs=pl.BlockSpec((None, rows, n_c), lambda l, r: (l, r, 0)),
        out_shape=jax.ShapeDtypeStruct(w.shape, BF16),
        compiler_params=_cparams(2),
        name="weight_cast",
    )(w)


def _mod_kernel(c_ref, w_ref, b_ref, o_ref):
    s = _silu(c_ref[...]).astype(BF16)
    o_ref[...] = _dot(s, w_ref[...].astype(BF16)) + b_ref[...]


def _modulation(cond, w_mod, b_mod):
    n_col = 6 * D_MODEL // D_MODEL
    return pl.pallas_call(
        _mod_kernel,
        grid=(DEPTH, n_col),
        in_specs=[
            pl.BlockSpec((N_COND, D_MODEL), lambda l, j: (0, 0)),
            pl.BlockSpec((None, D_MODEL, D_MODEL), lambda l, j: (l, 0, j)),
            pl.BlockSpec((None, 1, D_MODEL), lambda l, j: (l, 0, j)),
        ],
        out_specs=pl.BlockSpec((None, N_COND, D_MODEL), lambda l, j: (l, 0, j)),
        out_shape=jax.ShapeDtypeStruct((DEPTH, N_COND, 6 * D_MODEL), F32),
        compiler_params=_cparams(2),
        name="modulation",
    )(cond, w_mod, b_mod.reshape(DEPTH, 1, 6 * D_MODEL))


def _inproj_kernel(x_ref, gain_ref, mod_ref, w_ref, wt_ref, *out_refs, splits):
    m = mod_ref[...]
    h = _mod_norm(x_ref[...], gain_ref[...], m[:, 0:D_MODEL], m[:, D_MODEL:2 * D_MODEL]).astype(BF16)
    for o_ref, (a, b) in zip(out_refs[:-1], splits):
        for c0 in range(a, b, COL_CHUNK):
            c1 = min(c0 + COL_CHUNK, b)
            o_ref[:, c0 - a:c1 - a] = _dot(h, w_ref[:, c0:c1]).astype(o_ref.dtype)
    out_refs[-1][...] = _dot(h, wt_ref[...])


def _inproj(x, gain, mod_l, w_all, layer, w_tail, splits, dtypes, name):
    n_c = w_all.shape[2]
    tok = lambda i: (i, 0)
    const = lambda i: (0, 0)
    widths = [b - a for a, b in splits] + [LANES]
    return pl.pallas_call(
        functools.partial(_inproj_kernel, splits=splits),
        grid=(N_TOK // TOKEN_TILE,),
        in_specs=[
            pl.BlockSpec((TOKEN_TILE, D_MODEL), tok),
            pl.BlockSpec((1, D_MODEL), const),
            pl.BlockSpec((None, 1, 6 * D_MODEL), lambda i: (_mod_row(i), 0, 0)),
            pl.BlockSpec((None, D_MODEL, n_c), lambda i: (layer, 0, 0)),
            pl.BlockSpec((D_MODEL, LANES), const),
        ],
        out_specs=[pl.BlockSpec((TOKEN_TILE, w), tok) for w in widths],
        out_shape=[jax.ShapeDtypeStruct((N_TOK, w), dt) for w, dt in zip(widths, tuple(dtypes) + (F32,))],
        compiler_params=_cparams(1),
        name=name,
    )(x, gain.reshape(1, D_MODEL), mod_l, w_all, w_tail)


def _outproj_mlp_kernel(*refs, k_sizes, n_ctx_tiles):
    x_ref = refs[0]
    y_refs = refs[1:1 + 2 * len(k_sizes)]
    wo_ref, gain_ref, mod_ref, w1_ref, w2_ref, o_ref = refs[1 + 2 * len(k_sizes):]
    m = mod_ref[...]
    g1 = m[:, 2 * D_MODEL:3 * D_MODEL]
    sh2 = m[:, 3 * D_MODEL:4 * D_MODEL]
    sc2 = m[:, 4 * D_MODEL:5 * D_MODEL]
    g2 = m[:, 5 * D_MODEL:6 * D_MODEL]
    is_ctx = pl.program_id(0) < n_ctx_tiles
    proj = None
    off = 0
    for idx, k in enumerate(k_sizes):
        y = jnp.where(is_ctx, y_refs[2 * idx][...], y_refs[2 * idx + 1][...])
        part = _dot(y, wo_ref[off:off + k, :])
        proj = part if proj is None else proj + part
        off += k
    x1 = x_ref[...] + g1 * proj
    h = _mod_norm(x1, gain_ref[...], sh2, sc2).astype(BF16)
    acc = jnp.zeros(x1.shape, F32)
    for c0 in range(0, MLP_HIDDEN, COL_CHUNK):
        a = jnp.maximum(_dot(h, w1_ref[:, c0:c0 + COL_CHUNK]), 0.0)
        acc = acc + _dot((a * a).astype(BF16), w2_ref[c0:c0 + COL_CHUNK, :])
    o_ref[...] = x1 + g2 * acc


def _outproj_mlp(x, ys, w_out_all, j, gain, mod_l, w1_all, w2_all, layer, name):
    k_sizes = tuple(yp.shape[1] for yp, _ in ys)
    k_in = sum(k_sizes)
    tile = MLP_TILE
    per_row = TOKEN_TILE // tile
    n_ctx_tiles = N_PROMPT_TOK // tile
    tok = lambda i: (i, 0)
    const = lambda i: (0, 0)
    y_args, y_specs = [], []
    for yp, ysm in ys:
        k = yp.shape[1]
        y_args += [yp, ysm]
        y_specs += [pl.BlockSpec((tile, k), lambda i: (jnp.minimum(i, n_ctx_tiles - 1), 0)),
                    pl.BlockSpec((tile, k), lambda i: (jnp.maximum(i - n_ctx_tiles, 0), 0))]
    return pl.pallas_call(
        functools.partial(_outproj_mlp_kernel, k_sizes=k_sizes, n_ctx_tiles=n_ctx_tiles),
        grid=(N_TOK // tile,),
        in_specs=[pl.BlockSpec((tile, D_MODEL), tok)] + y_specs + [
            pl.BlockSpec((None, k_in, D_MODEL), lambda i: (j, 0, 0)),
            pl.BlockSpec((1, D_MODEL), const),
            pl.BlockSpec((None, 1, 6 * D_MODEL), lambda i: (_mod_row(i // per_row), 0, 0)),
            pl.BlockSpec((None, D_MODEL, MLP_HIDDEN), lambda i: (layer, 0, 0)),
            pl.BlockSpec((None, MLP_HIDDEN, D_MODEL), lambda i: (layer, 0, 0)),
        ],
        out_specs=pl.BlockSpec((tile, D_MODEL), tok),
        out_shape=jax.ShapeDtypeStruct((N_TOK, D_MODEL), F32),
        compiler_params=_cparams(1),
        name=name,
    )(x, *y_args, w_out_all, gain.reshape(1, D_MODEL), mod_l, w1_all, w2_all)


def _final_norm_kernel(x_ref, g_ref, o_ref):
    x = x_ref[...]
    o_ref[...] = x * lax.rsqrt(jnp.mean(x * x, axis=-1, keepdims=True) + NORM_EPS) * g_ref[...]


def _final_norm(x, gain, tok0, n_tok):
    t0 = tok0 // TOKEN_TILE
    return pl.pallas_call(
        _final_norm_kernel,
        grid=(n_tok // TOKEN_TILE,),
        in_specs=[pl.BlockSpec((TOKEN_TILE, D_MODEL), lambda i: (t0 + i, 0)),
                  pl.BlockSpec((1, D_MODEL), lambda i: (0, 0))],
        out_specs=pl.BlockSpec((TOKEN_TILE, D_MODEL), lambda i: (i, 0)),
        out_shape=jax.ShapeDtypeStruct((n_tok, D_MODEL), F32),
        compiler_params=_cparams(1),
        name="final_norm",
    )(x, gain.reshape(1, D_MODEL))


def _rope_arrays(n_tok):
    axis_dim = ATT_HEAD_DIM // 2
    quarter = ATT_HEAD_DIM // 4
    inv_freq = ROPE_THETA ** (-jnp.arange(0, axis_dim, 2, dtype=F32) / axis_dim)
    t = jnp.arange(n_tok)
    row = (t // GRID_W).astype(F32)
    col = (t % GRID_W).astype(F32)
    d = np.arange(LANES) % ATT_HEAD_DIM
    part = d // axis_dim
    within = d % axis_dim
    freq = inv_freq[jnp.asarray(within % quarter)]
    pos = jnp.where(jnp.asarray(part)[None, :] == 0, row[:, None], col[:, None])
    ang = pos * freq[None, :]
    cos = jnp.cos(ang)
    sin = jnp.sin(ang)
    fh = jnp.asarray(within < quarter)[None, :]
    return cos, jnp.where(fh, -sin, 0.0), jnp.where(fh, 0.0, sin)


def _apply_rope(x, cos, sin_a, sin_b):
    quarter = ATT_HEAD_DIM // 4
    return (x * cos + pltpu.roll(x, LANES - quarter, axis=1) * sin_a
            + pltpu.roll(x, quarter, axis=1) * sin_b)


def _head_rms(x, gmat, gain):
    ss = _dot_xl(x * x, gmat)
    return x * lax.rsqrt(ss * (1.0 / ATT_HEAD_DIM) + NORM_EPS) * gain


def _attn_kernel(*refs, n, n_ctx, rope, emit_k, tq):
    it = iter(refs)
    q_ref, k_ref, v_ref = next(it), next(it), next(it)
    qg_ref, kg_ref, gmat_ref = next(it), next(it), next(it)
    if n_ctx:
        ck_ref, cv_ref = next(it), next(it)
    if rope:
        cos_ref, sa_ref, sb_ref = next(it), next(it), next(it)
    o_ref = next(it)
    if emit_k:
        ko_ref = next(it)
    keys_ref, vals_ref = next(it), next(it)

    gmat = gmat_ref[...]
    kn = _head_rms(k_ref[...], gmat, kg_ref[...])
    if emit_k:
        ko_ref[...] = kn
    if rope:
        kn = _apply_rope(kn, cos_ref[...], sa_ref[...], sb_ref[...])
    keys_ref[0:n, :] = kn.astype(BF16)
    vals_ref[0:n, :] = v_ref[...].astype(BF16)
    if n_ctx:
        keys_ref[n:n + n_ctx, :] = ck_ref[...].astype(BF16)
        vals_ref[n:n + n_ctx, :] = cv_ref[...].astype(BF16)
    low = lax.broadcasted_iota(jnp.int32, (1, LANES), 1) < ATT_HEAD_DIM
    scale = ATT_HEAD_DIM ** -0.5
    heads_per_tile = LANES // ATT_HEAD_DIM
    tiles_per_kv = (ATT_HEADS // ATT_KV_HEADS) // heads_per_tile

    def attend(qm):
        s = _dot_nt(qm.astype(BF16), keys_ref[...])
        p = jnp.exp(s - jnp.max(s, axis=-1, keepdims=True))
        l = jnp.sum(p, axis=-1, keepdims=True)
        return _dot(p.astype(BF16), vals_ref[...]) * (1.0 / l)

    def q_tile(qi, carry):
        rows = pl.ds(pl.multiple_of(qi * tq, tq), tq)
        for t in range(ATT_Q_DIM // LANES):
            cols = slice(t * LANES, (t + 1) * LANES)
            qn = _head_rms(q_ref[rows, cols], gmat, qg_ref[...])
            if rope:
                qn = _apply_rope(qn, cos_ref[rows, :], sa_ref[rows, :], sb_ref[rows, :])
            qn = qn * scale
            qs = pltpu.roll(qn, ATT_HEAD_DIM, axis=1)
            if t // tiles_per_kv == 0:
                r_e = attend(jnp.where(low, qn, 0.0))
                r_o = attend(jnp.where(low, qs, 0.0))
                o = jnp.where(low, r_e, pltpu.roll(r_o, ATT_HEAD_DIM, axis=1))
            else:
                r_e = attend(jnp.where(low, 0.0, qs))
                r_o = attend(jnp.where(low, 0.0, qn))
                o = jnp.where(low, pltpu.roll(r_e, ATT_HEAD_DIM, axis=1), r_o)
            o_ref[rows, cols] = o.astype(o_ref.dtype)
        return carry

    lax.fori_loop(0, n // tq, q_tile, 0)


def _attention(q, k, v, q_gain, k_gain, *, n_batch, n, tok0, ctx=None, emit_k=False):
    rope = ctx is not None
    n_ctx = ctx[0].shape[1] if rope else 0
    b0 = tok0 // n
    tq = 128 if rope else n
    tokb = lambda b: (b0 + b, 0)
    const = lambda b: (0, 0)
    heads_per_tile = LANES // ATT_HEAD_DIM
    gmat = jnp.asarray(np.kron(np.eye(heads_per_tile), np.ones((ATT_HEAD_DIM, ATT_HEAD_DIM))), BF16)
    args = [q, k, v, jnp.tile(q_gain, heads_per_tile).reshape(1, LANES),
            jnp.tile(k_gain, heads_per_tile).reshape(1, LANES), gmat]
    in_specs = [
        pl.BlockSpec((n, ATT_Q_DIM), tokb),
        pl.BlockSpec((n, ATT_KV_DIM), tokb),
        pl.BlockSpec((n, ATT_KV_DIM), tokb),
        pl.BlockSpec((1, LANES), const),
        pl.BlockSpec((1, LANES), const),
        pl.BlockSpec((LANES, LANES), const),
    ]
    if rope:
        args += [ctx[0], ctx[1]]
        in_specs += [pl.BlockSpec((None, n_ctx, ATT_KV_DIM), lambda b: (b, 0, 0))] * 2
        args += list(_rope_arrays(n))
        in_specs += [pl.BlockSpec((n, LANES), const)] * 3
    out_shape = [jax.ShapeDtypeStruct((n_batch * n, ATT_Q_DIM), BF16)]
    out_specs = [pl.BlockSpec((n, ATT_Q_DIM), lambda b: (b, 0))]
    if emit_k:
        out_shape.append(jax.ShapeDtypeStruct((n_batch * n, ATT_KV_DIM), F32))
        out_specs.append(pl.BlockSpec((n, ATT_KV_DIM), lambda b: (b, 0)))
    return pl.pallas_call(
        functools.partial(_attn_kernel, n=n, n_ctx=n_ctx, rope=rope, emit_k=emit_k, tq=tq),
        grid=(n_batch,),
        in_specs=in_specs,
        out_specs=out_specs,
        out_shape=out_shape,
        scratch_shapes=[pltpu.VMEM((n + n_ctx, ATT_KV_DIM), BF16),
                        pltpu.VMEM((n + n_ctx, ATT_KV_DIM), BF16)],
        compiler_params=_cparams(1),
        name="attention_latent" if rope else "attention_context",
    )(*args)


def _conv_silu(x, w, bias, n):
    row = lax.broadcasted_iota(jnp.int32, x.shape, 0)
    prev = jnp.where(row == 0, 0.0, pltpu.roll(x, 1, axis=0))
    nxt = jnp.where(row == n - 1, 0.0, pltpu.roll(x, n - 1, axis=0))
    y = prev * w[0:1, :] + x * w[1:2, :] + nxt * w[2:3, :]
    if bias is not None:
        y = y + bias
    return _silu(y)


DN_CAT = DN_HEADS * CHUNK
DN_EXPAND_W = DN_CAT + DN_QK_DIM
DN_SEQ_PER_STEP = 2


def _to_stack(x):
    return jnp.concatenate([x[:, h * LANES:(h + 1) * LANES] for h in range(DN_HEADS)], axis=0)


def _delta_kernel(*refs, n, n_seq, has_s0, emit_s):
    it = iter(refs)
    dqkv_ref, dz_ref, small_ref, convw_ref, pcol_ref, gain_ref, expand_ref, cum_ref = (next(it) for _ in range(8))
    if has_s0:
        s0_ref = next(it)
    o_ref = next(it)
    if emit_s:
        sfin_ref = next(it)
    q_scr, k_scr, v_scr, g_scr, b_scr, of_scr, ob_scr, s_scr = (next(it) for _ in range(8))

    n_chunks = n // CHUNK
    n_hd = 2 * DN_HEADS
    for s in range(n_seq):
        seq = slice(s * n, (s + 1) * n)
        for h in range(DN_HEADS):
            for part, scr in ((0, q_scr), (1, k_scr), (2, v_scr)):
                c0 = part * DN_QK_DIM + h * DN_KEY_DIM
                x = _conv_silu(dqkv_ref[seq, c0:c0 + LANES].astype(F32), convw_ref[:, c0:c0 + LANES], None, n)
                if part < 2:
                    x = x * lax.rsqrt(jnp.sum(x * x, axis=-1, keepdims=True) + NORM_EPS)
                if part == 0:
                    x = x * (DN_KEY_DIM ** -0.5)
                scr[seq, h * LANES:(h + 1) * LANES] = x
    small = small_ref[...]
    pcol = pcol_ref[...]
    lane = lax.broadcasted_iota(jnp.int32, (1, LANES), 1)
    b_scr[...] = jnp.where(lane < n_hd, _sigmoid(small), 0.0)
    g_scr[...] = jnp.where(lane < n_hd, 0.0, jnp.where(
        lane < 2 * n_hd, -jnp.exp(pcol[0:1, :]) * _softplus(small + pcol[1:2, :]), 0.0))
    if has_s0:
        s_scr[...] = s0_ref[...]
    else:
        s_scr[...] = jnp.zeros(s_scr.shape, F32)

    rc = lax.broadcasted_iota(jnp.int32, (CHUNK, DN_CAT), 0)
    cc = lax.broadcasted_iota(jnp.int32, (CHUNK, DN_CAT), 1) & (CHUNK - 1)
    eye_cat = jnp.where(rc == cc, 1.0, 0.0)
    r4 = lax.broadcasted_iota(jnp.int32, (DN_CAT, DN_CAT), 0)
    c4 = lax.broadcasted_iota(jnp.int32, (DN_CAT, DN_CAT), 1)
    blk_sq = jnp.where(_shr(r4, 6) == _shr(c4, 6), 1.0, 0.0).astype(BF16)
    r5 = lax.broadcasted_iota(jnp.int32, (DN_CAT, DN_QK_DIM), 0)
    c5 = lax.broadcasted_iota(jnp.int32, (DN_CAT, DN_QK_DIM), 1)
    blk_wide = jnp.where(_shr(r5, 6) == _shr(c5, 7), 1.0, 0.0).astype(BF16)
    n_levels = int(math.log2(CHUNK))
    incl, strict_f, off_masks = [], [], []
    for d in range(2):
        upper = d == 1
        incl.append((rc <= cc) if upper else (rc >= cc))
        strict_f.append(jnp.where((rc < cc) if upper else (rc > cc), 1.0, 0.0))
        masks = []
        for lvl in range(n_levels):
            same_pair = _shr(rc, lvl + 1) == _shr(cc, lvl + 1)
            half_r, half_c = _shr(rc, lvl), _shr(cc, lvl)
            side = (half_r < half_c) if upper else (half_r > half_c)
            masks.append(jnp.where(same_pair, jnp.where(side, 1.0, 0.0), 0.0))
        off_masks.append(masks)

    def block_diag(x_cat):
        return _rep_rows(x_cat.astype(BF16), DN_HEADS) * blk_sq

    def block_wide(x_stack):
        return _rep_lanes(x_stack.astype(BF16), DN_HEADS) * blk_wide

    chains = [(s, d) for s in range(n_seq) for d in range(2)]

    def chunk_step(c):
        rows = {}
        for s, d in chains:
            chunk = c if d == 0 else n_chunks - 1 - c
            rows[s, d] = pl.ds(pl.multiple_of(s * n + chunk * CHUNK, CHUNK), CHUNK)
        ex = {}
        for ch in chains:
            hi, mid, lo = _split3(jnp.concatenate([g_scr[rows[ch], :], b_scr[rows[ch], :]], axis=0))
            pieces = jnp.concatenate([hi, mid, lo], axis=0)
            ex[ch] = _dot(pieces, expand_ref[ch[1]]).astype(BF16)
        for ch in chains:
            ex[ch] = _dot(cum_ref[ch[1]], ex[ch])
        a, kb, k, q, egc, gc_wide, g_tot, decay, beta_wide = ({} for _ in range(9))
        for ch in chains:
            upper = ch[1] == 1
            gc_cat = ex[ch][0:CHUNK, 0:DN_CAT]
            gc_wide[ch] = ex[ch][0:CHUNK, DN_CAT:]
            beta_wide[ch] = ex[ch][CHUNK:2 * CHUNK, DN_CAT:]
            gr_cat = jnp.sum(gc_cat * eye_cat, axis=0, keepdims=True)
            decay[ch] = jnp.exp(jnp.where(incl[ch[1]], gc_cat - gr_cat, NEG_BIG))
            g_tot[ch] = gc_wide[ch][0:1, :] if upper else gc_wide[ch][CHUNK - 1:CHUNK, :]
            egc[ch] = jnp.exp(gc_wide[ch])
            q[ch] = q_scr[rows[ch], :]
            k[ch] = k_scr[rows[ch], :]
            kb[ch] = k[ch] * beta_wide[ch]
            k_bd = _rep_rows(k[ch].astype(BF16), DN_HEADS) * blk_wide
            a[ch] = _dot_nt(jnp.concatenate([kb[ch], q[ch]], axis=0).astype(BF16), k_bd)
        m, qk, p = {}, {}, {}
        for ch in chains:
            m[ch] = a[ch][0:CHUNK] * decay[ch] * strict_f[ch[1]]
            qk[ch] = a[ch][CHUNK:2 * CHUNK] * decay[ch]
            p[ch] = eye_cat - m[ch] * off_masks[ch[1]][0]
        for lvl in range(1, n_levels):
            t1 = {ch: _dot(p[ch].astype(BF16), block_diag(m[ch] * off_masks[ch[1]][lvl])) for ch in chains}
            for ch in chains:
                p[ch] = p[ch] - _dot(t1[ch].astype(BF16), block_diag(p[ch]))
        uw, s_old, wq = {}, {}, {}
        for ch in chains:
            v = v_scr[rows[ch], :]
            rhs = jnp.concatenate([_to_stack(v * beta_wide[ch]), _to_stack(kb[ch] * egc[ch])], axis=1)
            uw[ch] = _dot(block_diag(p[ch]), rhs.astype(BF16))
        for ch in chains:
            s_old[ch] = s_scr[ch[0], ch[1]]
            lhs = jnp.concatenate([block_wide(uw[ch][:, DN_VAL_DIM:]),
                                   block_wide(_to_stack(q[ch] * egc[ch]))], axis=0)
            wq[ch] = _dot(lhs, s_old[ch].astype(BF16))
        for ch in chains:
            v_new = (uw[ch][:, 0:DN_VAL_DIM] - wq[ch][0:DN_CAT]).astype(BF16)
            o = wq[ch][DN_CAT:2 * DN_CAT] + _dot(block_diag(qk[ch]), v_new)
            k_dec = _to_stack(k[ch] * jnp.exp(g_tot[ch] - gc_wide[ch]))
            dec = jnp.concatenate(
                [jnp.broadcast_to(jnp.exp(g_tot[ch][:, h * LANES:(h + 1) * LANES]), (DN_KEY_DIM, LANES))
                 for h in range(DN_HEADS)], axis=0)
            s_scr[ch[0], ch[1]] = s_old[ch] * dec + _dot_tn(block_wide(k_dec), v_new)
            o_scr = ob_scr if ch[1] == 1 else of_scr
            for h in range(DN_HEADS):
                o_scr[rows[ch], h * LANES:(h + 1) * LANES] = o[h * CHUNK:(h + 1) * CHUNK]

    def body(c, carry):
        chunk_step(c)
        return carry

    lax.fori_loop(0, n_chunks, body, 0)

    if emit_s:
        sfin_ref[...] = s_scr[...]
    for h in range(DN_HEADS):
        cols = slice(h * LANES, (h + 1) * LANES)
        o = of_scr[:, cols] + ob_scr[:, cols]
        o = o * lax.rsqrt(jnp.mean(o * o, axis=-1, keepdims=True) + NORM_EPS) * gain_ref[...]
        o_ref[:, cols] = (o * _silu(dz_ref[:, cols].astype(F32))).astype(o_ref.dtype)


def _delta(dqkv, dz, small, conv_w, a_log, dt_bias, out_gain, *, n_batch, n, tok0, s0=None, emit_s=False):
    n_seq = DN_SEQ_PER_STEP
    rows = n_seq * n
    b0 = tok0 // rows
    n_hd = 2 * DN_HEADS
    tokb = lambda b: (b0 + b, 0)
    const = lambda b: (0, 0)
    const3 = lambda b: (0, 0, 0)
    pcol = jnp.zeros((2, LANES), F32)
    pcol = pcol.at[0, n_hd:2 * n_hd].set(a_log.reshape(-1)).at[1, n_hd:2 * n_hd].set(dt_bias.reshape(-1))
    expand = np.zeros((2, LANES, DN_EXPAND_W), np.float32)
    for d in range(2):
        for h in range(DN_HEADS):
            for src in (d * DN_HEADS + h, n_hd + d * DN_HEADS + h):
                expand[d, src, h * CHUNK:(h + 1) * CHUNK] = 1.0
                expand[d, src, DN_CAT + h * LANES:DN_CAT + (h + 1) * LANES] = 1.0
    tri = np.tril(np.ones((CHUNK, CHUNK), np.float32))
    eye = np.eye(CHUNK, dtype=np.float32)
    zero = np.zeros((CHUNK, CHUNK), np.float32)
    cum = np.stack([np.block([[t, zero] * 3, [zero, eye] * 3]) for t in (tri, tri.T)])
    args = [dqkv, dz, small, conv_w, pcol, out_gain.reshape(1, LANES), jnp.asarray(expand, BF16),
            jnp.asarray(cum, BF16)]
    in_specs = [
        pl.BlockSpec((rows, DN_CONV_DIM), tokb),
        pl.BlockSpec((rows, DN_V_DIM), tokb),
        pl.BlockSpec((rows, LANES), tokb),
        pl.BlockSpec((3, DN_CONV_DIM), const),
        pl.BlockSpec((2, LANES), const),
        pl.BlockSpec((1, LANES), const),
        pl.BlockSpec((2, LANES, DN_EXPAND_W), const3),
        pl.BlockSpec((2, 2 * CHUNK, 6 * CHUNK), const3),
    ]
    state_shape = (2, DN_HEADS * DN_KEY_DIM, DN_VAL_DIM)
    state_spec = pl.BlockSpec((n_seq,) + state_shape, lambda b: (b, 0, 0, 0))
    if s0 is not None:
        args.append(s0.reshape((n_batch,) + state_shape))
        in_specs.append(state_spec)
    out_shape = [jax.ShapeDtypeStruct((n_batch * n, DN_V_DIM), BF16)]
    out_specs = [pl.BlockSpec((rows, DN_V_DIM), lambda b: (b, 0))]
    if emit_s:
        out_shape.append(jax.ShapeDtypeStruct((n_batch,) + state_shape, F32))
        out_specs.append(state_spec)
    return pl.pallas_call(
        functools.partial(_delta_kernel, n=n, n_seq=n_seq, has_s0=s0 is not None, emit_s=emit_s),
        grid=(n_batch // n_seq,),
        in_specs=in_specs,
        out_specs=out_specs,
        out_shape=out_shape,
        scratch_shapes=[
            pltpu.VMEM((rows, DN_QK_DIM), F32), pltpu.VMEM((rows, DN_QK_DIM), F32),
            pltpu.VMEM((rows, DN_V_DIM), F32),
            pltpu.VMEM((rows, LANES), F32), pltpu.VMEM((rows, LANES), F32),
            pltpu.VMEM((rows, DN_V_DIM), F32), pltpu.VMEM((rows, DN_V_DIM), F32),
            pltpu.VMEM((n_seq,) + state_shape, F32),
        ],
        compiler_params=_cparams(1),
        name="delta_latent" if s0 is not None else "delta_context",
    )(*args)


SSD_CHUNK = 256
SSM_GROUP_W = SSM_INNER // SSM_GROUPS
HEADS_PER_TILE = LANES // SSM_HEAD_DIM


def _ssd_kernel(*refs, n, has_s0, emit_s):
    it = iter(refs)
    z_ref, xbc_ref, dtc_ref, convw_ref, convb_ref, pcol_ref = (next(it) for _ in range(6))
    dskip_ref, gain_ref, expand_ref = next(it), next(it), next(it)
    if has_s0:
        s0_ref = next(it)
    o_ref = next(it)
    if emit_s:
        sfin_ref = next(it)
    xs_scr, bc_scr, y_scr, dt_scr, st_scr = (next(it) for _ in range(5))

    q_len = SSD_CHUNK
    n_chunks = n // q_len
    carry_state = has_s0 or n_chunks > 1

    def conv_tile(t):
        cols = pl.ds(pl.multiple_of(t * LANES, LANES), LANES)
        return _conv_silu(xbc_ref[:, cols].astype(F32), convw_ref[:, cols], convb_ref[:, cols], n)

    def prep_x(t, carry):
        cols = pl.ds(pl.multiple_of(t * LANES, LANES), LANES)
        x = conv_tile(t)
        xs_scr[:, cols] = x
        y_scr[:, cols] = x * dskip_ref[:, cols]
        return carry

    def prep_bc(t, carry):
        cols = pl.ds(pl.multiple_of(t * LANES, LANES), LANES)
        bc_scr[:, cols] = conv_tile(t + SSM_INNER // LANES).astype(BF16)
        return carry

    lax.fori_loop(0, SSM_INNER // LANES, prep_x, 0)
    lax.fori_loop(0, 2 * SSM_BC_DIM // LANES, prep_bc, 0)
    pcol = pcol_ref[...]
    dt_scr[...] = _softplus(dtc_ref[...] + pcol[1:2, :])
    neg_a_col = -jnp.exp(pcol[0:1, :])
    for d in range(2):
        for g in range(SSM_GROUPS):
            gs = slice(g * SSM_GROUP_W, (g + 1) * SSM_GROUP_W)
            if has_s0:
                st_scr[d, :, gs] = s0_ref[d, gs, :].T
            else:
                st_scr[d, :, gs] = jnp.zeros((SSM_STATE, SSM_GROUP_W), F32)

    ri = lax.broadcasted_iota(jnp.int32, (q_len, q_len), 0)
    ci = lax.broadcasted_iota(jnp.int32, (q_len, q_len), 1)
    eye_b = (ri == ci).astype(BF16)
    tri_b = ((ri >= ci).astype(BF16), (ri <= ci).astype(BF16))
    low = lax.broadcasted_iota(jnp.int32, (1, LANES), 1) < SSM_HEAD_DIM
    heads_per_group = SSM_HEADS // SSM_GROUPS

    def decays(c, d):
        upper = d == 1
        rows = pl.ds(pl.multiple_of(c * q_len, q_len), q_len)
        dt = dt_scr[rows, :]
        acum = _dot_xr(tri_b[d], dt * neg_a_col)
        a_tot = acum[0:1, :] if upper else acum[q_len - 1:q_len, :]
        hi, mid, lo = _split3(jnp.concatenate([acum, dt], axis=1))
        rowf = _dot_tn(hi, eye_b) + _dot_tn(mid, eye_b) + _dot_tn(lo, eye_b)
        acum_r = rowf[0:LANES]
        dt_r = rowf[LANES:2 * LANES]
        tot_r = acum_r[:, 0:1] if upper else acum_r[:, q_len - 1:q_len]
        w_r = dt_r * jnp.exp(tot_r - acum_r)
        dec_e = _dot_xl(jnp.broadcast_to(jnp.exp(a_tot), (8, LANES)), expand_ref[d])[0:1, :]
        return dict(rows=rows, acum=acum, acum_r=acum_r, dt_r=dt_r, w_r=w_r, dec_e=dec_e, eac=jnp.exp(acum),
                    incl=(ri <= ci) if upper else (ri >= ci))

    def chunk_step(c_fwd, c_bwd, same_chunk):
        pre = (decays(c_fwd, 0), decays(c_bwd, 1))
        for g in range(SSM_GROUPS):
            shared = None
            for d in range(2):
                pd = pre[d]
                rows = pd["rows"]
                if shared is None or not same_chunk:
                    bm_g = bc_scr[rows, g * SSM_STATE:(g + 1) * SSM_STATE]
                    cm_g = bc_scr[rows, SSM_BC_DIM + g * SSM_STATE:SSM_BC_DIM + (g + 1) * SSM_STATE]
                    shared = (cm_g, _dot_nt(cm_g, bm_g), bm_g.astype(F32).T)
                cm_g, cb, bm_t = shared
                for tt in range(heads_per_group // HEADS_PER_TILE):
                    h0 = g * heads_per_group + tt * HEADS_PER_TILE
                    ls = slice(g * SSM_GROUP_W + tt * LANES, g * SSM_GROUP_W + (tt + 1) * LANES)
                    w_parts, b_parts = [], []
                    for hh in range(HEADS_PER_TILE):
                        l = d * SSM_HEADS + h0 + hh
                        lmat = jnp.exp(jnp.where(pd["incl"], pd["acum"][:, l:l + 1] - pd["acum_r"][l:l + 1, :],
                                                 NEG_BIG))
                        w_parts.append(cb * (lmat * pd["dt_r"][l:l + 1, :]))
                        b_parts.append(bm_t * pd["w_r"][l:l + 1, :])
                    lhs = jnp.concatenate([jnp.concatenate(w_parts, axis=1),
                                           jnp.concatenate(b_parts, axis=1)], axis=0).astype(BF16)
                    xt = xs_scr[rows, ls]
                    bd = jnp.concatenate([jnp.where(low, xt, 0.0), jnp.where(low, 0.0, xt)], axis=0)
                    res = _dot(lhs, bd.astype(BF16))
                    y_new = y_scr[rows, ls] + res[0:q_len]
                    st_inc = res[q_len:q_len + SSM_STATE]
                    if carry_state:
                        st = st_scr[d, :, ls]
                        l0 = d * SSM_HEADS + h0
                        scale = jnp.where(low, pd["eac"][:, l0:l0 + 1], pd["eac"][:, l0 + 1:l0 + 2])
                        y_new = y_new + _dot(cm_g, st.astype(BF16)) * scale
                        st_inc = st * pd["dec_e"][:, ls] + st_inc
                    y_scr[rows, ls] = y_new
                    st_scr[d, :, ls] = st_inc

    if n_chunks == 1:
        chunk_step(0, 0, True)
    else:
        def body(c, carry):
            chunk_step(c, n_chunks - 1 - c, False)
            return carry

        lax.fori_loop(0, n_chunks, body, 0)

    if emit_s:
        for d in range(2):
            for g in range(SSM_GROUPS):
                gs = slice(g * SSM_GROUP_W, (g + 1) * SSM_GROUP_W)
                sfin_ref[d, gs, :] = st_scr[d, :, gs].T
    row_tile = 128

    def finish(r, carry):
        rows = pl.ds(pl.multiple_of(r * row_tile, row_tile), row_tile)
        y = y_scr[rows, :] * _silu(z_ref[rows, :].astype(F32))
        y = y * lax.rsqrt(jnp.mean(y * y, axis=-1, keepdims=True) + NORM_EPS) * gain_ref[...]
        o_ref[rows, :] = y.astype(o_ref.dtype)
        return carry

    lax.fori_loop(0, n // row_tile, finish, 0)


def _ssd(z, xbc, dt_raw, conv_w, conv_b, a_log, dt_bias, d_skip, out_gain, *, n_batch, n, tok0,
         s0=None, emit_s=False):
    b0 = tok0 // n
    tokb = lambda b: (b0 + b, 0)
    const = lambda b: (0, 0)
    pcol = jnp.zeros((2, LANES), F32)
    pcol = pcol.at[0, 0:2 * SSM_HEADS].set(a_log.reshape(-1)).at[1, 0:2 * SSM_HEADS].set(dt_bias.reshape(-1))
    expand = np.zeros((2, LANES, SSM_INNER), np.float32)
    for d in range(2):
        for h in range(SSM_HEADS):
            expand[d, d * SSM_HEADS + h, h * SSM_HEAD_DIM:(h + 1) * SSM_HEAD_DIM] = 1.0
    args = [z, xbc, dt_raw, conv_w, conv_b.reshape(1, SSM_CONV_DIM), pcol,
            jnp.repeat(d_skip, SSM_HEAD_DIM).reshape(1, SSM_INNER), out_gain.reshape(1, SSM_INNER),
            jnp.asarray(expand, BF16)]
    big = dict(pipeline_mode=pl.Buffered(1)) if n > SEQ else {}
    in_specs = [
        pl.BlockSpec((n, SSM_INNER), tokb, **big),
        pl.BlockSpec((n, SSM_CONV_DIM), tokb, **big),
        pl.BlockSpec((n, LANES), tokb),
        pl.BlockSpec((3, SSM_CONV_DIM), const),
        pl.BlockSpec((1, SSM_CONV_DIM), const),
        pl.BlockSpec((2, LANES), const),
        pl.BlockSpec((1, SSM_INNER), const),
        pl.BlockSpec((1, SSM_INNER), const),
        pl.BlockSpec((2, LANES, SSM_INNER), lambda b: (0, 0, 0)),
    ]
    state_spec = pl.BlockSpec((None, 2, SSM_INNER, SSM_STATE), lambda b: (b, 0, 0, 0))
    if s0 is not None:
        args.append(s0.reshape(n_batch, 2, SSM_INNER, SSM_STATE))
        in_specs.append(state_spec)
    out_shape = [jax.ShapeDtypeStruct((n_batch * n, SSM_INNER), BF16)]
    out_specs = [pl.BlockSpec((n, SSM_INNER), lambda b: (b, 0))]
    if emit_s:
        out_shape.append(jax.ShapeDtypeStruct((n_batch, 2, SSM_INNER, SSM_STATE), F32))
        out_specs.append(state_spec)
    return pl.pallas_call(
        functools.partial(_ssd_kernel, n=n, has_s0=s0 is not None, emit_s=emit_s),
        grid=(n_batch,),
        in_specs=in_specs,
        out_specs=out_specs,
        out_shape=out_shape,
        scratch_shapes=[
            pltpu.VMEM((n, SSM_INNER), F32), pltpu.VMEM((n, 2 * SSM_BC_DIM), BF16),
            pltpu.VMEM((n, SSM_INNER), F32), pltpu.VMEM((n, LANES), F32),
            pltpu.VMEM((2, SSM_STATE, SSM_INNER), F32),
        ],
        compiler_params=_cparams(1),
        name="ssd_latent" if s0 is not None else "ssd_context",
    )(*args)


EVEN_MAIN = ATT_Q_DIM + 2 * ATT_KV_DIM + DN_CONV_DIM + DN_V_DIM
EVEN_SPLITS = ((0, ATT_Q_DIM), (ATT_Q_DIM, ATT_Q_DIM + ATT_KV_DIM),
               (ATT_Q_DIM + ATT_KV_DIM, ATT_Q_DIM + 2 * ATT_KV_DIM),
               (ATT_Q_DIM + 2 * ATT_KV_DIM, ATT_Q_DIM + 2 * ATT_KV_DIM + DN_CONV_DIM),
               (ATT_Q_DIM + 2 * ATT_KV_DIM + DN_CONV_DIM, EVEN_MAIN))
EVEN_DTYPES = (F32, F32, F32, BF16, BF16)
ODD_MAIN = SSM_INNER + SSM_CONV_DIM
ODD_SPLITS = ((0, SSM_INNER), (SSM_INNER, ODD_MAIN))
ODD_DTYPES = (BF16, BF16)


def _tail_cols(w, start):
    tail = w[:, :, start:]
    return jnp.pad(tail, ((0, 0), (0, 0), (0, LANES - tail.shape[2]))).astype(BF16)


def kernel(x_prompt, x_sample, c, cache_attn_k, cache_attn_v, state_delta, state_ssm, c_ctx, norm_mix_g, norm_mlp_g, w_mod, b_mod, w_mlp_in, w_mlp_out, w_in_even, attn_q_norm_g, attn_k_norm_g, delta_conv_w, delta_a_log, delta_dt_bias, delta_norm_g, w_out_even, w_in_odd, ssm_conv_w, ssm_conv_b, ssm_a_log, ssm_dt_bias, ssm_d, ssm_norm_g, w_out_odd, final_norm_g):
    x = jnp.concatenate([x_prompt.reshape(N_PROMPT_TOK, D_MODEL), x_sample.reshape(N_SAMPLE_TOK, D_MODEL)], axis=0)
    cond = jnp.zeros((N_COND, D_MODEL), F32).at[0].set(c_ctx).at[1:1 + DEC_BATCH].set(c)
    mod = _modulation(cond, w_mod, b_mod).reshape(DEPTH, N_COND, 1, 6 * D_MODEL)
    w_in_even_b, w_in_odd_b = _cast_bf16(w_in_even), _cast_bf16(w_in_odd)
    w_out_even_b, w_out_odd_b = _cast_bf16(w_out_even), _cast_bf16(w_out_odd)
    w_mlp_in_b, w_mlp_out_b = _cast_bf16(w_mlp_in), _cast_bf16(w_mlp_out)
    tail_even, tail_odd = _tail_cols(w_in_even, EVEN_MAIN), _tail_cols(w_in_odd, ODD_MAIN)

    ks, vs, sds, sss = [], [], [], []
    for layer in range(DEPTH):
        j = layer // 2
        if layer % 2 == 0:
            q, k, v, dqkv, dz, small = _inproj(x, norm_mix_g[layer], mod[layer], w_in_even_b, j, tail_even[j],
                                               EVEN_SPLITS, EVEN_DTYPES, "inproj_even")
            ctx_k = cache_attn_k[:, j].reshape(DEC_BATCH, PAST_LEN, ATT_KV_DIM)
            ctx_v = cache_attn_v[:, j].reshape(DEC_BATCH, PAST_LEN, ATT_KV_DIM)
            o_att_p, k_norm = _attention(q, k, v, attn_q_norm_g[j], attn_k_norm_g[j],
                                         n_batch=BATCH, n=SEQ, tok0=0, emit_k=True)
            (o_att_s,) = _attention(q, k, v, attn_q_norm_g[j], attn_k_norm_g[j],
                                    n_batch=DEC_BATCH, n=DEC_SEQ, tok0=N_PROMPT_TOK, ctx=(ctx_k, ctx_v))
            dn_args = (dqkv, dz, small, delta_conv_w[j], delta_a_log[j], delta_dt_bias[j], delta_norm_g[j])
            o_dn_p, s_delta = _delta(*dn_args, n_batch=BATCH, n=SEQ, tok0=0, emit_s=True)
            (o_dn_s,) = _delta(*dn_args, n_batch=DEC_BATCH, n=DEC_SEQ, tok0=N_PROMPT_TOK, s0=state_delta[:, j])
            ys = [(o_att_p, o_att_s), (o_dn_p, o_dn_s)]
            w_out_b = w_out_even_b
            ks.append(k_norm.reshape(BATCH, SEQ, ATT_KV_HEADS, ATT_HEAD_DIM))
            vs.append(v[:N_PROMPT_TOK].reshape(BATCH, SEQ, ATT_KV_HEADS, ATT_HEAD_DIM))
            sds.append(s_delta.reshape(BATCH, 2, DN_HEADS, DN_KEY_DIM, DN_VAL_DIM))
        else:
            z, xbc, dt_raw = _inproj(x, norm_mix_g[layer], mod[layer], w_in_odd_b, j, tail_odd[j],
                                     ODD_SPLITS, ODD_DTYPES, "inproj_odd")
            ssd_args = (z, xbc, dt_raw, ssm_conv_w[j], ssm_conv_b[j], ssm_a_log[j], ssm_dt_bias[j], ssm_d[j],
                        ssm_norm_g[j])
            y_p, s_ssm = _ssd(*ssd_args, n_batch=BATCH, n=SEQ, tok0=0, emit_s=True)
            (y_s,) = _ssd(*ssd_args, n_batch=DEC_BATCH, n=DEC_SEQ, tok0=N_PROMPT_TOK, s0=state_ssm[:, j])
            ys = [(y_p, y_s)]
            w_out_b = w_out_odd_b
            sss.append(s_ssm.reshape(BATCH, 2, SSM_HEADS, SSM_HEAD_DIM, SSM_STATE))
        x = _outproj_mlp(x, ys, w_out_b, j, norm_mlp_g[layer], mod[layer], w_mlp_in_b, w_mlp_out_b, layer,
                         "outproj_mlp_even" if layer % 2 == 0 else "outproj_mlp_odd")

    y_prompt = _final_norm(x, final_norm_g, 0, N_PROMPT_TOK).reshape(BATCH, SEQ, D_MODEL)
    y_sample = _final_norm(x, final_norm_g, N_PROMPT_TOK, N_SAMPLE_TOK).reshape(DEC_BATCH, DEC_SEQ, D_MODEL)
    return (y_prompt, y_sample, jnp.stack(ks, axis=1), jnp.stack(vs, axis=1),
            jnp.stack(sds, axis=1), jnp.stack(sss, axis=1))
```

```python
import functools
import math

import jax
import jax.numpy as jnp
import numpy as np
from jax import lax
from jax.experimental import pallas as pl
from jax.experimental.pallas import tpu as pltpu

F32 = jnp.float32
BF16 = jnp.bfloat16

D_MODEL = 1024
BATCH = 32
SEQ = 256
DEPTH = 4
DEC_BATCH = 4
DEC_SEQ = 1024
PAST_LEN = 512
GRID_W = 64
N_EVEN = (DEPTH + 1) // 2
N_ODD = DEPTH // 2
ATT_HEAD_DIM = 64
ATT_HEADS = 8
ATT_KV_HEADS = 2
ATT_Q_DIM = ATT_HEADS * ATT_HEAD_DIM
ATT_KV_DIM = ATT_KV_HEADS * ATT_HEAD_DIM
DN_KEY_DIM = 128
DN_VAL_DIM = 128
DN_HEADS = 4
DN_QK_DIM = DN_HEADS * DN_KEY_DIM
DN_V_DIM = DN_HEADS * DN_VAL_DIM
DN_CONV_DIM = 2 * DN_QK_DIM + DN_V_DIM
SSM_INNER = 2 * D_MODEL
SSM_HEAD_DIM = 64
SSM_HEADS = SSM_INNER // SSM_HEAD_DIM
SSM_GROUPS = 8
SSM_STATE = 128
SSM_BC_DIM = SSM_GROUPS * SSM_STATE
SSM_CONV_DIM = SSM_INNER + 2 * SSM_BC_DIM
MLP_HIDDEN = 4 * D_MODEL
CHUNK = 64
ROPE_THETA = 10000.0
NORM_EPS = 1e-6

LANES = 128
N_PROMPT_TOK = BATCH * SEQ
N_SAMPLE_TOK = DEC_BATCH * DEC_SEQ
N_TOK = N_PROMPT_TOK + N_SAMPLE_TOK
N_COND = 8
TOKEN_TILE = 512
MLP_TILE = 256
N_PROMPT_TILES = N_PROMPT_TOK // TOKEN_TILE
TILES_PER_DEC_SEQ = DEC_SEQ // TOKEN_TILE
COL_CHUNK = 512
NEG_BIG = -1e30
VMEM_LIMIT = 56 * 1024 * 1024


def _cparams(n_grid):
    return pltpu.CompilerParams(dimension_semantics=("arbitrary",) * n_grid,
                                vmem_limit_bytes=VMEM_LIMIT)


def _silu(x):
    return x / (1.0 + jnp.exp(-x))


def _sigmoid(x):
    return 1.0 / (1.0 + jnp.exp(-x))


def _softplus(x):
    return jnp.maximum(x, 0.0) + jnp.log1p(jnp.exp(-jnp.abs(x)))


def _dot(a, b):
    return jnp.dot(a, b, preferred_element_type=F32)


def _dot_nt(a, b):
    return lax.dot_general(a, b, (((1,), (1,)), ((), ())), preferred_element_type=F32)


def _dot_tn(a, b):
    return lax.dot_general(a, b, (((0,), (0,)), ((), ())), preferred_element_type=F32)


def _split3(a):
    hi = a.astype(BF16)
    r = a - hi.astype(F32)
    mid = r.astype(BF16)
    lo = (r - mid.astype(F32)).astype(BF16)
    return hi, mid, lo


def _dot_xl(a, b_exact):
    hi, mid, lo = _split3(a)
    return _dot(hi, b_exact) + _dot(mid, b_exact) + _dot(lo, b_exact)


def _dot_xr(a_exact, b):
    hi, mid, lo = _split3(b)
    return _dot(a_exact, hi) + _dot(a_exact, mid) + _dot(a_exact, lo)


def _mod_norm(x, gain, shift, scale):
    y = x * lax.rsqrt(jnp.mean(x * x, axis=-1, keepdims=True) + NORM_EPS) * gain
    return y * (1.0 + scale) + shift


def _mod_row(i):
    return jnp.where(i < N_PROMPT_TILES, 0, 1 + (i - N_PROMPT_TILES) // TILES_PER_DEC_SEQ)


def _shr(i, k):
    return lax.shift_right_logical(i, jnp.int32(k))


def _rep_rows(x, k):
    return jnp.concatenate([x] * k, axis=0)


def _rep_lanes(x, k):
    return jnp.concatenate([x] * k, axis=1)


def _cast_kernel(x_ref, o_ref):
    o_ref[...] = x_ref[...].astype(o_ref.dtype)


def _cast_bf16(w):
    n_l, n_r, n_c = w.shape
    rows = 256
    return pl.pallas_call(
        _cast_kernel,
        grid=(n_l, n_r // rows),
        in_specs=[pl.BlockSpec((None, rows, n_c), lambda l, r: (l, r, 0))],
        out_specs=pl.BlockSpec((None, rows, n_c), lambda l, r: (l, r, 0)),
        out_shape=jax.ShapeDtypeStruct(w.shape, BF16),
        compiler_params=_cparams(2),
        name="weight_cast",
    )(w)


def _mod_kernel(c_ref, w_ref, b_ref, o_ref):
    s = _silu(c_ref[...]).astype(BF16)
    o_ref[...] = _dot(s, w_ref[...].astype(BF16)) + b_ref[...]


def _modulation(cond, w_mod, b_mod):
    n_col = 6 * D_MODEL // D_MODEL
    return pl.pallas_call(
        _mod_kernel,
        grid=(DEPTH, n_col),
        in_specs=[
            pl.BlockSpec((N_COND, D_MODEL), lambda l, j: (0, 0)),
            pl.BlockSpec((None, D_MODEL, D_MODEL), lambda l, j: (l, 0, j)),
            pl.BlockSpec((None, 1, D_MODEL), lambda l, j: (l, 0, j)),
        ],
        out_specs=pl.BlockSpec((None, N_COND, D_MODEL), lambda l, j: (l, 0, j)),
        out_shape=jax.ShapeDtypeStruct((DEPTH, N_COND, 6 * D_MODEL), F32),
        compiler_params=_cparams(2),
        name="modulation",
    )(cond, w_mod, b_mod.reshape(DEPTH, 1, 6 * D_MODEL))


def _inproj_kernel(x_ref, gain_ref, mod_ref, w_ref, wt_ref, *out_refs, splits):
    m = mod_ref[...]
    h = _mod_norm(x_ref[...], gain_ref[...], m[:, 0:D_MODEL], m[:, D_MODEL:2 * D_MODEL]).astype(BF16)
    for o_ref, (a, b) in zip(out_refs[:-1], splits):
        for c0 in range(a, b, COL_CHUNK):
            c1 = min(c0 + COL_CHUNK, b)
            o_ref[:, c0 - a:c1 - a] = _dot(h, w_ref[:, c0:c1]).astype(o_ref.dtype)
    out_refs[-1][...] = _dot(h, wt_ref[...])


def _inproj(x, gain, mod_l, w_all, layer, w_tail, splits, dtypes, name):
    n_c = w_all.shape[2]
    tok = lambda i: (i, 0)
    const = lambda i: (0, 0)
    widths = [b - a for a, b in splits] + [LANES]
    return pl.pallas_call(
        functools.partial(_inproj_kernel, splits=splits),
        grid=(N_TOK // TOKEN_TILE,),
        in_specs=[
            pl.BlockSpec((TOKEN_TILE, D_MODEL), tok),
            pl.BlockSpec((1, D_MODEL), const),
            pl.BlockSpec((None, 1, 6 * D_MODEL), lambda i: (_mod_row(i), 0, 0)),
            pl.BlockSpec((None, D_MODEL, n_c), lambda i: (layer, 0, 0)),
            pl.BlockSpec((D_MODEL, LANES), const),
        ],
        out_specs=[pl.BlockSpec((TOKEN_TILE, w), tok) for w in widths],
        out_shape=[jax.ShapeDtypeStruct((N_TOK, w), dt) for w, dt in zip(widths, tuple(dtypes) + (F32,))],
        compiler_params=_cparams(1),
        name=name,
    )(x, gain.reshape(1, D_MODEL), mod_l, w_all, w_tail)


def _outproj_mlp_kernel(*refs, k_sizes, n_ctx_tiles):
    x_ref = refs[0]
    y_refs = refs[1:1 + 2 * len(k_sizes)]
    wo_ref, gain_ref, mod_ref, w1_ref, w2_ref, o_ref = refs[1 + 2 * len(k_sizes):]
    m = mod_ref[...]
    g1 = m[:, 2 * D_MODEL:3 * D_MODEL]
    sh2 = m[:, 3 * D_MODEL:4 * D_MODEL]
    sc2 = m[:, 4 * D_MODEL:5 * D_MODEL]
    g2 = m[:, 5 * D_MODEL:6 * D_MODEL]
    is_ctx = pl.program_id(0) < n_ctx_tiles
    proj = None
    off = 0
    for idx, k in enumerate(k_sizes):
        y = jnp.where(is_ctx, y_refs[2 * idx][...], y_refs[2 * idx + 1][...])
        part = _dot(y, wo_ref[off:off + k, :])
        proj = part if proj is None else proj + part
        off += k
    x1 = x_ref[...] + g1 * proj
    h = _mod_norm(x1, gain_ref[...], sh2, sc2).astype(BF16)
    acc = jnp.zeros(x1.shape, F32)
    for c0 in range(0, MLP_HIDDEN, COL_CHUNK):
        a = jnp.maximum(_dot(h, w1_ref[:, c0:c0 + COL_CHUNK]), 0.0)
        acc = acc + _dot((a * a).astype(BF16), w2_ref[c0:c0 + COL_CHUNK, :])
    o_ref[...] = x1 + g2 * acc


def _outproj_mlp(x, ys, w_out_all, j, gain, mod_l, w1_all, w2_all, layer, name):
    k_sizes = tuple(yp.shape[1] for yp, _ in ys)
    k_in = sum(k_sizes)
    tile = MLP_TILE
    per_row = TOKEN_TILE // tile
    n_ctx_tiles = N_PROMPT_TOK // tile
    tok = lambda i: (i, 0)
    const = lambda i: (0, 0)
    y_args, y_specs = [], []
    for yp, ysm in ys:
        k = yp.shape[1]
        y_args += [yp, ysm]
        y_specs += [pl.BlockSpec((tile, k), lambda i: (jnp.minimum(i, n_ctx_tiles - 1), 0)),
                    pl.BlockSpec((tile, k), lambda i: (jnp.maximum(i - n_ctx_tiles, 0), 0))]
    return pl.pallas_call(
        functools.partial(_outproj_mlp_kernel, k_sizes=k_sizes, n_ctx_tiles=n_ctx_tiles),
        grid=(N_TOK // tile,),
        in_specs=[pl.BlockSpec((tile, D_MODEL), tok)] + y_specs + [
            pl.BlockSpec((None, k_in, D_MODEL), lambda i: (j, 0, 0)),
            pl.BlockSpec((1, D_MODEL), const),
            pl.BlockSpec((None, 1, 6 * D_MODEL), lambda i: (_mod_row(i // per_row), 0, 0)),
            pl.BlockSpec((None, D_MODEL, MLP_HIDDEN), lambda i: (layer, 0, 0)),
            pl.BlockSpec((None, MLP_HIDDEN, D_MODEL), lambda i: (layer, 0, 0)),
        ],
        out_specs=pl.BlockSpec((tile, D_MODEL), tok),
        out_shape=jax.ShapeDtypeStruct((N_TOK, D_MODEL), F32),
        compiler_params=_cparams(1),
        name=name,
    )(x, *y_args, w_out_all, gain.reshape(1, D_MODEL), mod_l, w1_all, w2_all)


def _final_norm_kernel(x_ref, g_ref, o_ref):
    x = x_ref[...]
    o_ref[...] = x * lax.rsqrt(jnp.mean(x * x, axis=-1, keepdims=True) + NORM_EPS) * g_ref[...]


def _final_norm(x, gain, tok0, n_tok):
    t0 = tok0 // TOKEN_TILE
    return pl.pallas_call(
        _final_norm_kernel,
        grid=(n_tok // TOKEN_TILE,),
        in_specs=[pl.BlockSpec((TOKEN_TILE, D_MODEL), lambda i: (t0 + i, 0)),
                  pl.BlockSpec((1, D_MODEL), lambda i: (0, 0))],
        out_specs=pl.BlockSpec((TOKEN_TILE, D_MODEL), lambda i: (i, 0)),
        out_shape=jax.ShapeDtypeStruct((n_tok, D_MODEL), F32),
        compiler_params=_cparams(1),
        name="final_norm",
    )(x, gain.reshape(1, D_MODEL))


def _rope_arrays(n_tok):
    axis_dim = ATT_HEAD_DIM // 2
    quarter = ATT_HEAD_DIM // 4
    inv_freq = ROPE_THETA ** (-jnp.arange(0, axis_dim, 2, dtype=F32) / axis_dim)
    t = jnp.arange(n_tok)
    row = (t // GRID_W).astype(F32)
    col = (t % GRID_W).astype(F32)
    d = np.arange(LANES) % ATT_HEAD_DIM
    part = d // axis_dim
    within = d % axis_dim
    freq = inv_freq[jnp.asarray(within % quarter)]
    pos = jnp.where(jnp.asarray(part)[None, :] == 0, row[:, None], col[:, None])
    ang = pos * freq[None, :]
    cos = jnp.cos(ang)
    sin = jnp.sin(ang)
    fh = jnp.asarray(within < quarter)[None, :]
    return cos, jnp.where(fh, -sin, 0.0), jnp.where(fh, 0.0, sin)


def _apply_rope(x, cos, sin_a, sin_b):
    quarter = ATT_HEAD_DIM // 4
    return (x * cos + pltpu.roll(x, LANES - quarter, axis=1) * sin_a
            + pltpu.roll(x, quarter, axis=1) * sin_b)


def _head_rms(x, gmat, gain):
    ss = _dot_xl(x * x, gmat)
    return x * lax.rsqrt(ss * (1.0 / ATT_HEAD_DIM) + NORM_EPS) * gain


def _attn_kernel(*refs, n, n_ctx, rope, emit_k, tq):
    it = iter(refs)
    q_ref, k_ref, v_ref = next(it), next(it), next(it)
    qg_ref, kg_ref, gmat_ref = next(it), next(it), next(it)
    if n_ctx:
        ck_ref, cv_ref = next(it), next(it)
    if rope:
        cos_ref, sa_ref, sb_ref = next(it), next(it), next(it)
    o_ref = next(it)
    if emit_k:
        ko_ref = next(it)
    keys_ref, vals_ref = next(it), next(it)

    gmat = gmat_ref[...]
    kn = _head_rms(k_ref[...], gmat, kg_ref[...])
    if emit_k:
        ko_ref[...] = kn
    if rope:
        kn = _apply_rope(kn, cos_ref[...], sa_ref[...], sb_ref[...])
    keys_ref[0:n, :] = kn.astype(BF16)
    vals_ref[0:n, :] = v_ref[...].astype(BF16)
    if n_ctx:
        keys_ref[n:n + n_ctx, :] = ck_ref[...].astype(BF16)
        vals_ref[n:n + n_ctx, :] = cv_ref[...].astype(BF16)
    low = lax.broadcasted_iota(jnp.int32, (1, LANES), 1) < ATT_HEAD_DIM
    scale = ATT_HEAD_DIM ** -0.5
    heads_per_tile = LANES // ATT_HEAD_DIM
    tiles_per_kv = (ATT_HEADS // ATT_KV_HEADS) // heads_per_tile

    n_tiles = ATT_Q_DIM // LANES

    def q_tile(qi, carry):
        rows = pl.ds(pl.multiple_of(qi * tq, tq), tq)
        qm = []
        for t in range(n_tiles):
            qn = _head_rms(q_ref[rows, t * LANES:(t + 1) * LANES], gmat, qg_ref[...])
            if rope:
                qn = _apply_rope(qn, cos_ref[rows, :], sa_ref[rows, :], sb_ref[rows, :])
            qn = qn * scale
            qs = pltpu.roll(qn, ATT_HEAD_DIM, axis=1)
            if t // tiles_per_kv == 0:
                qm += [jnp.where(low, qn, 0.0), jnp.where(low, qs, 0.0)]
            else:
                qm += [jnp.where(low, 0.0, qs), jnp.where(low, 0.0, qn)]
        keys = keys_ref[...]
        vals = vals_ref[...]
        scores = [_dot_nt(x.astype(BF16), keys) for x in qm]
        probs, inv_l = [], []
        for s in scores:
            p = jnp.exp(s - jnp.max(s, axis=-1, keepdims=True))
            inv_l.append(1.0 / jnp.sum(p, axis=-1, keepdims=True))
            probs.append(p.astype(BF16))
        res = [_dot(p, vals) * il for p, il in zip(probs, inv_l)]
        for t in range(n_tiles):
            r_e, r_o = res[2 * t], res[2 * t + 1]
            if t // tiles_per_kv == 0:
                o = jnp.where(low, r_e, pltpu.roll(r_o, ATT_HEAD_DIM, axis=1))
            else:
                o = jnp.where(low, pltpu.roll(r_e, ATT_HEAD_DIM, axis=1), r_o)
            o_ref[rows, t * LANES:(t + 1) * LANES] = o.astype(o_ref.dtype)
        return carry

    lax.fori_loop(0, n // tq, q_tile, 0)


def _attention(q, k, v, q_gain, k_gain, *, n_batch, n, tok0, ctx=None, emit_k=False):
    rope = ctx is not None
    n_ctx = ctx[0].shape[1] if rope else 0
    b0 = tok0 // n
    tq = 128 if rope else n
    tokb = lambda b: (b0 + b, 0)
    const = lambda b: (0, 0)
    heads_per_tile = LANES // ATT_HEAD_DIM
    gmat = jnp.asarray(np.kron(np.eye(heads_per_tile), np.ones((ATT_HEAD_DIM, ATT_HEAD_DIM))), BF16)
    args = [q, k, v, jnp.tile(q_gain, heads_per_tile).reshape(1, LANES),
            jnp.tile(k_gain, heads_per_tile).reshape(1, LANES), gmat]
    in_specs = [
        pl.BlockSpec((n, ATT_Q_DIM), tokb),
        pl.BlockSpec((n, ATT_KV_DIM), tokb),
        pl.BlockSpec((n, ATT_KV_DIM), tokb),
        pl.BlockSpec((1, LANES), const),
        pl.BlockSpec((1, LANES), const),
        pl.BlockSpec((LANES, LANES), const),
    ]
    if rope:
        args += [ctx[0], ctx[1]]
        in_specs += [pl.BlockSpec((None, n_ctx, ATT_KV_DIM), lambda b: (b, 0, 0))] * 2
        args += list(_rope_arrays(n))
        in_specs += [pl.BlockSpec((n, LANES), const)] * 3
    out_shape = [jax.ShapeDtypeStruct((n_batch * n, ATT_Q_DIM), BF16)]
    out_specs = [pl.BlockSpec((n, ATT_Q_DIM), lambda b: (b, 0))]
    if emit_k:
        out_shape.append(jax.ShapeDtypeStruct((n_batch * n, ATT_KV_DIM), F32))
        out_specs.append(pl.BlockSpec((n, ATT_KV_DIM), lambda b: (b, 0)))
    return pl.pallas_call(
        functools.partial(_attn_kernel, n=n, n_ctx=n_ctx, rope=rope, emit_k=emit_k, tq=tq),
        grid=(n_batch,),
        in_specs=in_specs,
        out_specs=out_specs,
        out_shape=out_shape,
        scratch_shapes=[pltpu.VMEM((n + n_ctx, ATT_KV_DIM), BF16),
                        pltpu.VMEM((n + n_ctx, ATT_KV_DIM), BF16)],
        compiler_params=_cparams(1),
        name="attention_latent" if rope else "attention_context",
    )(*args)


def _conv_silu(x, w, bias, n):
    row = lax.broadcasted_iota(jnp.int32, x.shape, 0)
    prev = jnp.where(row == 0, 0.0, pltpu.roll(x, 1, axis=0))
    nxt = jnp.where(row == n - 1, 0.0, pltpu.roll(x, n - 1, axis=0))
    y = prev * w[0:1, :] + x * w[1:2, :] + nxt * w[2:3, :]
    if bias is not None:
        y = y + bias
    return _silu(y)


DN_CAT = DN_HEADS * CHUNK
DN_EXPAND_W = DN_CAT + DN_QK_DIM
DN_SEQ_PER_STEP = 2


def _to_stack(x):
    return jnp.concatenate([x[:, h * LANES:(h + 1) * LANES] for h in range(DN_HEADS)], axis=0)


def _delta_kernel(*refs, n, n_seq, has_s0, emit_s, n_prev):
    it = iter(refs)
    dqkv_ref, dz_ref, small_ref, convw_ref, pcol_ref, gain_ref, expand_ref, cum_ref = (next(it) for _ in range(8))
    if has_s0:
        s0_ref = next(it)
    if n_prev:
        prev_ref = next(it)
    o_ref = next(it)
    if emit_s:
        sfin_ref = next(it)
    q_scr, k_scr, v_scr, g_scr, b_scr, of_scr, ob_scr, s_scr = (next(it) for _ in range(8))

    n_chunks = n // CHUNK
    n_hd = 2 * DN_HEADS
    for s in range(n_seq):
        seq = slice(s * n, (s + 1) * n)
        for h in range(DN_HEADS):
            for part, scr in ((0, q_scr), (1, k_scr), (2, v_scr)):
                c0 = part * DN_QK_DIM + h * DN_KEY_DIM
                x = _conv_silu(dqkv_ref[seq, c0:c0 + LANES].astype(F32), convw_ref[:, c0:c0 + LANES], None, n)
                if part < 2:
                    x = x * lax.rsqrt(jnp.sum(x * x, axis=-1, keepdims=True) + NORM_EPS)
                if part == 0:
                    x = x * (DN_KEY_DIM ** -0.5)
                scr[seq, h * LANES:(h + 1) * LANES] = x
    small = small_ref[...]
    pcol = pcol_ref[...]
    lane = lax.broadcasted_iota(jnp.int32, (1, LANES), 1)
    b_scr[...] = jnp.where(lane < n_hd, _sigmoid(small), 0.0)
    g_scr[...] = jnp.where(lane < n_hd, 0.0, jnp.where(
        lane < 2 * n_hd, -jnp.exp(pcol[0:1, :]) * _softplus(small + pcol[1:2, :]), 0.0))
    if has_s0:
        s_scr[...] = s0_ref[...]
    else:
        s_scr[...] = jnp.zeros(s_scr.shape, F32)

    rc = lax.broadcasted_iota(jnp.int32, (CHUNK, DN_CAT), 0)
    cc = lax.broadcasted_iota(jnp.int32, (CHUNK, DN_CAT), 1) & (CHUNK - 1)
    eye_cat = jnp.where(rc == cc, 1.0, 0.0)
    r4 = lax.broadcasted_iota(jnp.int32, (DN_CAT, DN_CAT), 0)
    c4 = lax.broadcasted_iota(jnp.int32, (DN_CAT, DN_CAT), 1)
    blk_sq = jnp.where(_shr(r4, 6) == _shr(c4, 6), 1.0, 0.0).astype(BF16)
    r5 = lax.broadcasted_iota(jnp.int32, (DN_CAT, DN_QK_DIM), 0)
    c5 = lax.broadcasted_iota(jnp.int32, (DN_CAT, DN_QK_DIM), 1)
    blk_wide = jnp.where(_shr(r5, 6) == _shr(c5, 7), 1.0, 0.0).astype(BF16)
    n_levels = int(math.log2(CHUNK))
    incl, strict_f, off_masks = [], [], []
    for d in range(2):
        upper = d == 1
        incl.append((rc <= cc) if upper else (rc >= cc))
        strict_f.append(jnp.where((rc < cc) if upper else (rc > cc), 1.0, 0.0))
        masks = []
        for lvl in range(n_levels):
            same_pair = _shr(rc, lvl + 1) == _shr(cc, lvl + 1)
            half_r, half_c = _shr(rc, lvl), _shr(cc, lvl)
            side = (half_r < half_c) if upper else (half_r > half_c)
            masks.append(jnp.where(same_pair, jnp.where(side, 1.0, 0.0), 0.0))
        off_masks.append(masks)

    def block_diag(x_cat):
        return _rep_rows(x_cat.astype(BF16), DN_HEADS) * blk_sq

    def block_wide(x_stack):
        return _rep_lanes(x_stack.astype(BF16), DN_HEADS) * blk_wide

    chains = [(s, d) for s in range(n_seq) for d in range(2)]

    def chunk_step(c):
        rows = {}
        for s, d in chains:
            chunk = c if d == 0 else n_chunks - 1 - c
            rows[s, d] = pl.ds(pl.multiple_of(s * n + chunk * CHUNK, CHUNK), CHUNK)
        gb, ex = {}, {}
        for ch in chains:
            pieces = _split3(jnp.concatenate([g_scr[rows[ch], :], b_scr[rows[ch], :]], axis=0))
            gb[ch] = _dot(cum_ref[ch[1]], jnp.concatenate(pieces, axis=0))
        for ch in chains:
            ex[ch] = _dot(jnp.concatenate(_split3(gb[ch]), axis=1), expand_ref[ch[1]])
        a, kb, k, q, egc, gc_wide, g_tot, decay, beta_wide = ({} for _ in range(9))
        for ch in chains:
            upper = ch[1] == 1
            gc_cat = ex[ch][0:CHUNK, 0:DN_CAT]
            gc_wide[ch] = ex[ch][0:CHUNK, DN_CAT:]
            beta_wide[ch] = ex[ch][CHUNK:2 * CHUNK, DN_CAT:]
            gr_cat = jnp.sum(gc_cat * eye_cat, axis=0, keepdims=True)
            decay[ch] = jnp.exp(jnp.where(incl[ch[1]], gc_cat - gr_cat, NEG_BIG))
            g_tot[ch] = gc_wide[ch][0:1, :] if upper else gc_wide[ch][CHUNK - 1:CHUNK, :]
            egc[ch] = jnp.exp(gc_wide[ch])
            q[ch] = q_scr[rows[ch], :]
            k[ch] = k_scr[rows[ch], :]
            kb[ch] = k[ch] * beta_wide[ch]
            k_bd = _rep_rows(k[ch].astype(BF16), DN_HEADS) * blk_wide
            a[ch] = _dot_nt(jnp.concatenate([kb[ch], q[ch]], axis=0).astype(BF16), k_bd)
        m, qk, p = {}, {}, {}
        for ch in chains:
            m[ch] = a[ch][0:CHUNK] * decay[ch] * strict_f[ch[1]]
            qk[ch] = a[ch][CHUNK:2 * CHUNK] * decay[ch]
            p[ch] = eye_cat - m[ch] * off_masks[ch[1]][0]
        for lvl in range(1, n_levels):
            t1 = {ch: _dot(p[ch].astype(BF16), block_diag(m[ch] * off_masks[ch[1]][lvl])) for ch in chains}
            for ch in chains:
                p[ch] = p[ch] - _dot(t1[ch].astype(BF16), block_diag(p[ch]))
        uw, s_old, wq = {}, {}, {}
        for ch in chains:
            v = v_scr[rows[ch], :]
            rhs = jnp.concatenate([_to_stack(v * beta_wide[ch]), _to_stack(kb[ch] * egc[ch])], axis=1)
            uw[ch] = _dot(block_diag(p[ch]), rhs.astype(BF16))
        for ch in chains:
            s_old[ch] = s_scr[ch[0], ch[1]]
            lhs = jnp.concatenate([block_wide(uw[ch][:, DN_VAL_DIM:]),
                                   block_wide(_to_stack(q[ch] * egc[ch]))], axis=0)
            wq[ch] = _dot(lhs, s_old[ch].astype(BF16))
        for ch in chains:
            v_new = (uw[ch][:, 0:DN_VAL_DIM] - wq[ch][0:DN_CAT]).astype(BF16)
            o = wq[ch][DN_CAT:2 * DN_CAT] + _dot(block_diag(qk[ch]), v_new)
            k_dec = _to_stack(k[ch] * jnp.exp(g_tot[ch] - gc_wide[ch]))
            dec = jnp.concatenate(
                [jnp.broadcast_to(jnp.exp(g_tot[ch][:, h * LANES:(h + 1) * LANES]), (DN_KEY_DIM, LANES))
                 for h in range(DN_HEADS)], axis=0)
            s_scr[ch[0], ch[1]] = s_old[ch] * dec + _dot_tn(block_wide(k_dec), v_new)
            o_scr = ob_scr if ch[1] == 1 else of_scr
            for h in range(DN_HEADS):
                o_scr[rows[ch], h * LANES:(h + 1) * LANES] = o[h * CHUNK:(h + 1) * CHUNK]

    def body(c, carry):
        chunk_step(c)
        return carry

    lax.fori_loop(0, n_chunks, body, 0)

    if emit_s:
        if n_prev:
            sfin_ref[:, 0:n_prev] = prev_ref[...]
        sfin_ref[:, n_prev] = s_scr[...]
    for h in range(DN_HEADS):
        cols = slice(h * LANES, (h + 1) * LANES)
        o = of_scr[:, cols] + ob_scr[:, cols]
        o = o * lax.rsqrt(jnp.mean(o * o, axis=-1, keepdims=True) + NORM_EPS) * gain_ref[...]
        o_ref[:, cols] = (o * _silu(dz_ref[:, cols].astype(F32))).astype(o_ref.dtype)


def _delta(dqkv, dz, small, conv_w, a_log, dt_bias, out_gain, *, n_batch, n, tok0, s0=None, emit_s=False,
           prev=None):
    n_seq = DN_SEQ_PER_STEP
    rows = n_seq * n
    b0 = tok0 // rows
    n_hd = 2 * DN_HEADS
    tokb = lambda b: (b0 + b, 0)
    const = lambda b: (0, 0)
    const3 = lambda b: (0, 0, 0)
    pcol = jnp.zeros((2, LANES), F32)
    pcol = pcol.at[0, n_hd:2 * n_hd].set(a_log.reshape(-1)).at[1, n_hd:2 * n_hd].set(dt_bias.reshape(-1))
    expand = np.zeros((2, LANES, DN_EXPAND_W), np.float32)
    for d in range(2):
        for h in range(DN_HEADS):
            for src in (d * DN_HEADS + h, n_hd + d * DN_HEADS + h):
                expand[d, src, h * CHUNK:(h + 1) * CHUNK] = 1.0
                expand[d, src, DN_CAT + h * LANES:DN_CAT + (h + 1) * LANES] = 1.0
    expand = np.tile(expand, (1, 3, 1))
    tri = np.tril(np.ones((CHUNK, CHUNK), np.float32))
    eye = np.eye(CHUNK, dtype=np.float32)
    zero = np.zeros((CHUNK, CHUNK), np.float32)
    cum = np.stack([np.block([[t, zero] * 3, [zero, eye] * 3]) for t in (tri, tri.T)])
    args = [dqkv, dz, small, conv_w, pcol, out_gain.reshape(1, LANES), jnp.asarray(expand, BF16),
            jnp.asarray(cum, BF16)]
    in_specs = [
        pl.BlockSpec((rows, DN_CONV_DIM), tokb),
        pl.BlockSpec((rows, DN_V_DIM), tokb),
        pl.BlockSpec((rows, LANES), tokb),
        pl.BlockSpec((3, DN_CONV_DIM), const),
        pl.BlockSpec((2, LANES), const),
        pl.BlockSpec((1, LANES), const),
        pl.BlockSpec((2, 3 * LANES, DN_EXPAND_W), const3),
        pl.BlockSpec((2, 2 * CHUNK, 6 * CHUNK), const3),
    ]
    state_shape = (2, DN_HEADS * DN_KEY_DIM, DN_VAL_DIM)
    state_spec = pl.BlockSpec((n_seq,) + state_shape, lambda b: (b, 0, 0, 0))
    if s0 is not None:
        args.append(s0.reshape((n_batch,) + state_shape))
        in_specs.append(state_spec)
    n_prev = 0 if prev is None else prev.shape[1]
    stacked_spec = lambda k: pl.BlockSpec((n_seq, k) + state_shape, lambda b: (b, 0, 0, 0, 0))
    if n_prev:
        args.append(prev)
        in_specs.append(stacked_spec(n_prev))
    out_shape = [jax.ShapeDtypeStruct((n_batch * n, DN_V_DIM), BF16)]
    out_specs = [pl.BlockSpec((rows, DN_V_DIM), lambda b: (b, 0))]
    if emit_s:
        out_shape.append(jax.ShapeDtypeStruct((n_batch, n_prev + 1) + state_shape, F32))
        out_specs.append(stacked_spec(n_prev + 1))
    return pl.pallas_call(
        functools.partial(_delta_kernel, n=n, n_seq=n_seq, has_s0=s0 is not None, emit_s=emit_s, n_prev=n_prev),
        grid=(n_batch // n_seq,),
        in_specs=in_specs,
        out_specs=out_specs,
        out_shape=out_shape,
        scratch_shapes=[
            pltpu.VMEM((rows, DN_QK_DIM), F32), pltpu.VMEM((rows, DN_QK_DIM), F32),
            pltpu.VMEM((rows, DN_V_DIM), F32),
            pltpu.VMEM((rows, LANES), F32), pltpu.VMEM((rows, LANES), F32),
            pltpu.VMEM((rows, DN_V_DIM), F32), pltpu.VMEM((rows, DN_V_DIM), F32),
            pltpu.VMEM((n_seq,) + state_shape, F32),
        ],
        compiler_params=_cparams(1),
        name="delta_latent" if s0 is not None else "delta_context",
    )(*args)


SSD_CHUNK = 256
SSM_GROUP_W = SSM_INNER // SSM_GROUPS
HEADS_PER_TILE = LANES // SSM_HEAD_DIM


def _ssd_kernel(*refs, n, has_s0, emit_s, n_prev):
    it = iter(refs)
    z_ref, xbc_ref, dtc_ref, convw_ref, convb_ref, pcol_ref = (next(it) for _ in range(6))
    dskip_ref, gain_ref = next(it), next(it)
    if has_s0:
        s0_ref = next(it)
    if n_prev:
        prev_ref = next(it)
    o_ref = next(it)
    if emit_s:
        sfin_ref = next(it)
    xs_scr, bc_scr, y_scr, dt_scr, st_scr = (next(it) for _ in range(5))

    q_len = SSD_CHUNK
    n_chunks = n // q_len
    carry_state = has_s0 or n_chunks > 1

    def conv_tile(t):
        cols = pl.ds(pl.multiple_of(t * LANES, LANES), LANES)
        return _conv_silu(xbc_ref[:, cols].astype(F32), convw_ref[:, cols], convb_ref[:, cols], n)

    def prep_x(t, carry):
        cols = pl.ds(pl.multiple_of(t * LANES, LANES), LANES)
        x = conv_tile(t)
        xs_scr[:, cols] = x
        y_scr[:, cols] = x * dskip_ref[:, cols]
        return carry

    def prep_bc(t, carry):
        cols = pl.ds(pl.multiple_of(t * LANES, LANES), LANES)
        bc_scr[:, cols] = conv_tile(t + SSM_INNER // LANES).astype(BF16)
        return carry

    lax.fori_loop(0, SSM_INNER // LANES, prep_x, 0)
    lax.fori_loop(0, 2 * SSM_BC_DIM // LANES, prep_bc, 0)
    pcol = pcol_ref[...]
    dt_scr[...] = _softplus(dtc_ref[...] + pcol[1:2, :])
    neg_a_col = -jnp.exp(pcol[0:1, :])
    for d in range(2):
        for g in range(SSM_GROUPS):
            gs = slice(g * SSM_GROUP_W, (g + 1) * SSM_GROUP_W)
            if has_s0:
                st_scr[d, :, gs] = s0_ref[d, gs, :].T
            else:
                st_scr[d, :, gs] = jnp.zeros((SSM_STATE, SSM_GROUP_W), F32)

    ri = lax.broadcasted_iota(jnp.int32, (q_len, q_len), 0)
    ci = lax.broadcasted_iota(jnp.int32, (q_len, q_len), 1)
    tri3_b = (_rep_lanes((ri >= ci).astype(BF16), 3), _rep_lanes((ri <= ci).astype(BF16), 3))
    low = lax.broadcasted_iota(jnp.int32, (1, LANES), 1) < SSM_HEAD_DIM
    heads_per_group = SSM_HEADS // SSM_GROUPS

    def decays(c, d):
        upper = d == 1
        rows = pl.ds(pl.multiple_of(c * q_len, q_len), q_len)
        dt = dt_scr[rows, :]
        acum = _dot(tri3_b[d], jnp.concatenate(_split3(dt * neg_a_col), axis=0))
        a_tot = acum[0:1, :] if upper else acum[q_len - 1:q_len, :]
        w_col = dt * jnp.exp(a_tot - acum)
        return dict(rows=rows, acum=acum, acum_r=acum.T, dt_r=dt.T, w_r=w_col.T, eac=jnp.exp(acum),
                    ea_tot=jnp.exp(a_tot), incl=(ri <= ci) if upper else (ri >= ci))

    def chunk_step(c_fwd, c_bwd, same_chunk):
        pre = (decays(c_fwd, 0), decays(c_bwd, 1))
        for g in range(SSM_GROUPS):
            shared = None
            for d in range(2):
                pd = pre[d]
                rows = pd["rows"]
                if shared is None or not same_chunk:
                    bm_g = bc_scr[rows, g * SSM_STATE:(g + 1) * SSM_STATE]
                    cm_g = bc_scr[rows, SSM_BC_DIM + g * SSM_STATE:SSM_BC_DIM + (g + 1) * SSM_STATE]
                    shared = (cm_g, _dot_nt(cm_g, bm_g), bm_g.astype(F32).T)
                cm_g, cb, bm_t = shared
                for tt in range(heads_per_group // HEADS_PER_TILE):
                    h0 = g * heads_per_group + tt * HEADS_PER_TILE
                    ls = slice(g * SSM_GROUP_W + tt * LANES, g * SSM_GROUP_W + (tt + 1) * LANES)
                    w_parts, b_parts = [], []
                    for hh in range(HEADS_PER_TILE):
                        l = d * SSM_HEADS + h0 + hh
                        lmat = jnp.exp(jnp.where(pd["incl"], pd["acum"][:, l:l + 1] - pd["acum_r"][l:l + 1, :],
                                                 NEG_BIG))
                        w_parts.append(cb * (lmat * pd["dt_r"][l:l + 1, :]))
                        b_parts.append(bm_t * pd["w_r"][l:l + 1, :])
                    lhs = jnp.concatenate([jnp.concatenate(w_parts, axis=1),
                                           jnp.concatenate(b_parts, axis=1)], axis=0).astype(BF16)
                    xt = xs_scr[rows, ls]
                    bd = jnp.concatenate([jnp.where(low, xt, 0.0), jnp.where(low, 0.0, xt)], axis=0)
                    res = _dot(lhs, bd.astype(BF16))
                    y_new = y_scr[rows, ls] + res[0:q_len]
                    st_inc = res[q_len:q_len + SSM_STATE]
                    if carry_state:
                        st = st_scr[d, :, ls]
                        l0 = d * SSM_HEADS + h0

                        def pick(a, l0=l0):
                            return jnp.where(low, a[:, l0:l0 + 1], a[:, l0 + 1:l0 + 2])

                        y_new = y_new + _dot(cm_g, st.astype(BF16)) * pick(pd["eac"])
                        st_inc = st * pick(pd["ea_tot"]) + st_inc
                    y_scr[rows, ls] = y_new
                    st_scr[d, :, ls] = st_inc

    if n_chunks == 1:
        chunk_step(0, 0, True)
    else:
        def body(c, carry):
            chunk_step(c, n_chunks - 1 - c, False)
            return carry

        lax.fori_loop(0, n_chunks, body, 0)

    if emit_s:
        if n_prev:
            sfin_ref[0:n_prev] = prev_ref[...]
        for d in range(2):
            for g in range(SSM_GROUPS):
                gs = slice(g * SSM_GROUP_W, (g + 1) * SSM_GROUP_W)
                sfin_ref[n_prev, d, gs, :] = st_scr[d, :, gs].T
    row_tile = 128

    def finish(r, carry):
        rows = pl.ds(pl.multiple_of(r * row_tile, row_tile), row_tile)
        y = y_scr[rows, :] * _silu(z_ref[rows, :].astype(F32))
        y = y * lax.rsqrt(jnp.mean(y * y, axis=-1, keepdims=True) + NORM_EPS) * gain_ref[...]
        o_ref[rows, :] = y.astype(o_ref.dtype)
        return carry

    lax.fori_loop(0, n // row_tile, finish, 0)


def _ssd(z, xbc, dt_raw, conv_w, conv_b, a_log, dt_bias, d_skip, out_gain, *, n_batch, n, tok0,
         s0=None, emit_s=False, prev=None):
    b0 = tok0 // n
    tokb = lambda b: (b0 + b, 0)
    const = lambda b: (0, 0)
    pcol = jnp.zeros((2, LANES), F32)
    pcol = pcol.at[0, 0:2 * SSM_HEADS].set(a_log.reshape(-1)).at[1, 0:2 * SSM_HEADS].set(dt_bias.reshape(-1))
    args = [z, xbc, dt_raw, conv_w, conv_b.reshape(1, SSM_CONV_DIM), pcol,
            jnp.repeat(d_skip, SSM_HEAD_DIM).reshape(1, SSM_INNER), out_gain.reshape(1, SSM_INNER)]
    big = dict(pipeline_mode=pl.Buffered(1)) if n > SEQ else {}
    in_specs = [
        pl.BlockSpec((n, SSM_INNER), tokb, **big),
        pl.BlockSpec((n, SSM_CONV_DIM), tokb, **big),
        pl.BlockSpec((n, LANES), tokb),
        pl.BlockSpec((3, SSM_CONV_DIM), const),
        pl.BlockSpec((1, SSM_CONV_DIM), const),
        pl.BlockSpec((2, LANES), const),
        pl.BlockSpec((1, SSM_INNER), const),
        pl.BlockSpec((1, SSM_INNER), const),
    ]
    state_spec = pl.BlockSpec((None, 2, SSM_INNER, SSM_STATE), lambda b: (b, 0, 0, 0))
    if s0 is not None:
        args.append(s0.reshape(n_batch, 2, SSM_INNER, SSM_STATE))
        in_specs.append(state_spec)
    n_prev = 0 if prev is None else prev.shape[1]
    stacked_spec = lambda k: pl.BlockSpec((None, k, 2, SSM_INNER, SSM_STATE), lambda b: (b, 0, 0, 0, 0))
    if n_prev:
        args.append(prev)
        in_specs.append(stacked_spec(n_prev))
    out_shape = [jax.ShapeDtypeStruct((n_batch * n, SSM_INNER), BF16)]
    out_specs = [pl.BlockSpec((n, SSM_INNER), lambda b: (b, 0))]
    if emit_s:
        out_shape.append(jax.ShapeDtypeStruct((n_batch, n_prev + 1, 2, SSM_INNER, SSM_STATE), F32))
        out_specs.append(stacked_spec(n_prev + 1))
    return pl.pallas_call(
        functools.partial(_ssd_kernel, n=n, has_s0=s0 is not None, emit_s=emit_s, n_prev=n_prev),
        grid=(n_batch,),
        in_specs=in_specs,
        out_specs=out_specs,
        out_shape=out_shape,
        scratch_shapes=[
            pltpu.VMEM((n, SSM_INNER), F32), pltpu.VMEM((n, 2 * SSM_BC_DIM), BF16),
            pltpu.VMEM((n, SSM_INNER), F32), pltpu.VMEM((n, LANES), F32),
            pltpu.VMEM((2, SSM_STATE, SSM_INNER), F32),
        ],
        compiler_params=_cparams(1),
        name="ssd_latent" if s0 is not None else "ssd_context",
    )(*args)


EVEN_MAIN = ATT_Q_DIM + 2 * ATT_KV_DIM + DN_CONV_DIM + DN_V_DIM
EVEN_SPLITS = ((0, ATT_Q_DIM), (ATT_Q_DIM, ATT_Q_DIM + ATT_KV_DIM),
               (ATT_Q_DIM + ATT_KV_DIM, ATT_Q_DIM + 2 * ATT_KV_DIM),
               (ATT_Q_DIM + 2 * ATT_KV_DIM, ATT_Q_DIM + 2 * ATT_KV_DIM + DN_CONV_DIM),
               (ATT_Q_DIM + 2 * ATT_KV_DIM + DN_CONV_DIM, EVEN_MAIN))
EVEN_DTYPES = (F32, F32, F32, BF16, BF16)
ODD_MAIN = SSM_INNER + SSM_CONV_DIM
ODD_SPLITS = ((0, SSM_INNER), (SSM_INNER, ODD_MAIN))
ODD_DTYPES = (BF16, BF16)


def _tail_cols(w, start):
    tail = w[:, :, start:]
    return jnp.pad(tail, ((0, 0), (0, 0), (0, LANES - tail.shape[2]))).astype(BF16)


def kernel(x_prompt, x_sample, c, cache_attn_k, cache_attn_v, state_delta, state_ssm, c_ctx, norm_mix_g, norm_mlp_g, w_mod, b_mod, w_mlp_in, w_mlp_out, w_in_even, attn_q_norm_g, attn_k_norm_g, delta_conv_w, delta_a_log, delta_dt_bias, delta_norm_g, w_out_even, w_in_odd, ssm_conv_w, ssm_conv_b, ssm_a_log, ssm_dt_bias, ssm_d, ssm_norm_g, w_out_odd, final_norm_g):
    x = jnp.concatenate([x_prompt.reshape(N_PROMPT_TOK, D_MODEL), x_sample.reshape(N_SAMPLE_TOK, D_MODEL)], axis=0)
    cond = jnp.zeros((N_COND, D_MODEL), F32).at[0].set(c_ctx).at[1:1 + DEC_BATCH].set(c)
    mod = _modulation(cond, w_mod, b_mod).reshape(DEPTH, N_COND, 1, 6 * D_MODEL)
    w_in_even_b, w_in_odd_b = _cast_bf16(w_in_even), _cast_bf16(w_in_odd)
    w_out_even_b, w_out_odd_b = _cast_bf16(w_out_even), _cast_bf16(w_out_odd)
    w_mlp_in_b, w_mlp_out_b = _cast_bf16(w_mlp_in), _cast_bf16(w_mlp_out)
    tail_even, tail_odd = _tail_cols(w_in_even, EVEN_MAIN), _tail_cols(w_in_odd, ODD_MAIN)

    ks, vs = [], []
    s_delta = s_ssm = None
    for layer in range(DEPTH):
        j = layer // 2
        if layer % 2 == 0:
            q, k, v, dqkv, dz, small = _inproj(x, norm_mix_g[layer], mod[layer], w_in_even_b, j, tail_even[j],
                                               EVEN_SPLITS, EVEN_DTYPES, "inproj_even")
            ctx_k = cache_attn_k[:, j].reshape(DEC_BATCH, PAST_LEN, ATT_KV_DIM)
            ctx_v = cache_attn_v[:, j].reshape(DEC_BATCH, PAST_LEN, ATT_KV_DIM)
            o_att_p, k_norm = _attention(q, k, v, attn_q_norm_g[j], attn_k_norm_g[j],
                                         n_batch=BATCH, n=SEQ, tok0=0, emit_k=True)
            (o_att_s,) = _attention(q, k, v, attn_q_norm_g[j], attn_k_norm_g[j],
                                    n_batch=DEC_BATCH, n=DEC_SEQ, tok0=N_PROMPT_TOK, ctx=(ctx_k, ctx_v))
            dn_args = (dqkv, dz, small, delta_conv_w[j], delta_a_log[j], delta_dt_bias[j], delta_norm_g[j])
            o_dn_p, s_delta = _delta(*dn_args, n_batch=BATCH, n=SEQ, tok0=0, emit_s=True, prev=s_delta)
            (o_dn_s,) = _delta(*dn_args, n_batch=DEC_BATCH, n=DEC_SEQ, tok0=N_PROMPT_TOK, s0=state_delta[:, j])
            ys = [(o_att_p, o_att_s), (o_dn_p, o_dn_s)]
            w_out_b = w_out_even_b
            ks.append(k_norm.reshape(BATCH, SEQ, ATT_KV_HEADS, ATT_HEAD_DIM))
            vs.append(v[:N_PROMPT_TOK].reshape(BATCH, SEQ, ATT_KV_HEADS, ATT_HEAD_DIM))
        else:
            z, xbc, dt_raw = _inproj(x, norm_mix_g[layer], mod[layer], w_in_odd_b, j, tail_odd[j],
                                     ODD_SPLITS, ODD_DTYPES, "inproj_odd")
            ssd_args = (z, xbc, dt_raw, ssm_conv_w[j], ssm_conv_b[j], ssm_a_log[j], ssm_dt_bias[j], ssm_d[j],
                        ssm_norm_g[j])
            y_p, s_ssm = _ssd(*ssd_args, n_batch=BATCH, n=SEQ, tok0=0, emit_s=True, prev=s_ssm)
            (y_s,) = _ssd(*ssd_args, n_batch=DEC_BATCH, n=DEC_SEQ, tok0=N_PROMPT_TOK, s0=state_ssm[:, j])
            ys = [(y_p, y_s)]
            w_out_b = w_out_odd_b
        x = _outproj_mlp(x, ys, w_out_b, j, norm_mlp_g[layer], mod[layer], w_mlp_in_b, w_mlp_out_b, layer,
                         "outproj_mlp_even" if layer % 2 == 0 else "outproj_mlp_odd")

    y_prompt = _final_norm(x, final_norm_g, 0, N_PROMPT_TOK).reshape(BATCH, SEQ, D_MODEL)
    y_sample = _final_norm(x, final_norm_g, N_PROMPT_TOK, N_SAMPLE_TOK).reshape(DEC_BATCH, DEC_SEQ, D_MODEL)
    return (y_prompt, y_sample, jnp.stack(ks, axis=1), jnp.stack(vs, axis=1),
            s_delta.reshape(BATCH, N_EVEN, 2, DN_HEADS, DN_KEY_DIM, DN_VAL_DIM),
            s_ssm.reshape(BATCH, N_ODD, 2, SSM_HEADS, SSM_HEAD_DIM, SSM_STATE))
```

```python
import functools
import math

import jax
import jax.numpy as jnp
import numpy as np
from jax import lax
from jax.experimental import pallas as pl
from jax.experimental.pallas import tpu as pltpu

F32 = jnp.float32
BF16 = jnp.bfloat16

D_MODEL = 1024
BATCH = 32
SEQ = 256
DEPTH = 4
DEC_BATCH = 4
DEC_SEQ = 1024
PAST_LEN = 512
GRID_W = 64
N_EVEN = (DEPTH + 1) // 2
N_ODD = DEPTH // 2
ATT_HEAD_DIM = 64
ATT_HEADS = 8
ATT_KV_HEADS = 2
ATT_Q_DIM = ATT_HEADS * ATT_HEAD_DIM
ATT_KV_DIM = ATT_KV_HEADS * ATT_HEAD_DIM
DN_KEY_DIM = 128
DN_VAL_DIM = 128
DN_HEADS = 4
DN_QK_DIM = DN_HEADS * DN_KEY_DIM
DN_V_DIM = DN_HEADS * DN_VAL_DIM
DN_CONV_DIM = 2 * DN_QK_DIM + DN_V_DIM
SSM_INNER = 2 * D_MODEL
SSM_HEAD_DIM = 64
SSM_HEADS = SSM_INNER // SSM_HEAD_DIM
SSM_GROUPS = 8
SSM_STATE = 128
SSM_BC_DIM = SSM_GROUPS * SSM_STATE
SSM_CONV_DIM = SSM_INNER + 2 * SSM_BC_DIM
MLP_HIDDEN = 4 * D_MODEL
CHUNK = 64
ROPE_THETA = 10000.0
NORM_EPS = 1e-6

LANES = 128
N_PROMPT_TOK = BATCH * SEQ
N_SAMPLE_TOK = DEC_BATCH * DEC_SEQ
N_TOK = N_PROMPT_TOK + N_SAMPLE_TOK
N_COND = 8
TOKEN_TILE = 512
MLP_TILE = 512
N_PROMPT_TILES = N_PROMPT_TOK // TOKEN_TILE
TILES_PER_DEC_SEQ = DEC_SEQ // TOKEN_TILE
COL_CHUNK = 512
NEG_BIG = -1e30
VMEM_LIMIT = 56 * 1024 * 1024


def _cparams(n_grid):
    return pltpu.CompilerParams(dimension_semantics=("arbitrary",) * n_grid,
                                vmem_limit_bytes=VMEM_LIMIT)


def _silu(x):
    return x / (1.0 + jnp.exp(-x))


def _sigmoid(x):
    return 1.0 / (1.0 + jnp.exp(-x))


def _softplus(x):
    return jnp.maximum(x, 0.0) + jnp.log1p(jnp.exp(-jnp.abs(x)))


def _dot(a, b):
    return jnp.dot(a, b, preferred_element_type=F32)


def _dot_nt(a, b):
    return lax.dot_general(a, b, (((1,), (1,)), ((), ())), preferred_element_type=F32)


def _dot_tn(a, b):
    return lax.dot_general(a, b, (((0,), (0,)), ((), ())), preferred_element_type=F32)


def _split3(a):
    hi = a.astype(BF16)
    r = a - hi.astype(F32)
    mid = r.astype(BF16)
    lo = (r - mid.astype(F32)).astype(BF16)
    return hi, mid, lo


def _dot_xl(a, b_exact):
    hi, mid, lo = _split3(a)
    return _dot(hi, b_exact) + _dot(mid, b_exact) + _dot(lo, b_exact)


def _dot_xr(a_exact, b):
    hi, mid, lo = _split3(b)
    return _dot(a_exact, hi) + _dot(a_exact, mid) + _dot(a_exact, lo)


def _mod_norm(x, gain, shift, scale):
    y = x * lax.rsqrt(jnp.mean(x * x, axis=-1, keepdims=True) + NORM_EPS) * gain
    return y * (1.0 + scale) + shift


def _mod_row(i):
    return jnp.where(i < N_PROMPT_TILES, 0, 1 + (i - N_PROMPT_TILES) // TILES_PER_DEC_SEQ)


def _shr(i, k):
    return lax.shift_right_logical(i, jnp.int32(k))


def _rep_rows(x, k):
    return jnp.concatenate([x] * k, axis=0)


def _rep_lanes(x, k):
    return jnp.concatenate([x] * k, axis=1)


def _cast_kernel(x_ref, o_ref):
    o_ref[...] = x_ref[...].astype(o_ref.dtype)


def _cast_bf16(w):
    n_l, n_r, n_c = w.shape
    rows = 256
    return pl.pallas_call(
        _cast_kernel,
        grid=(n_l, n_r // rows),
        in_specs=[pl.BlockSpec((None, rows, n_c), lambda l, r: (l, r, 0))],
        out_specs=pl.BlockSpec((None, rows, n_c), lambda l, r: (l, r, 0)),
        out_shape=jax.ShapeDtypeStruct(w.shape, BF16),
        compiler_params=_cparams(2),
        name="weight_cast",
    )(w)


def _mod_kernel(c_ref, w_ref, b_ref, o_ref):
    s = _silu(c_ref[...]).astype(BF16)
    o_ref[...] = _dot(s, w_ref[...].astype(BF16)) + b_ref[...]


def _modulation(cond, w_mod, b_mod):
    n_col = 6 * D_MODEL // D_MODEL
    return pl.pallas_call(
        _mod_kernel,
        grid=(DEPTH, n_col),
        in_specs=[
            pl.BlockSpec((N_COND, D_MODEL), lambda l, j: (0, 0)),
            pl.BlockSpec((None, D_MODEL, D_MODEL), lambda l, j: (l, 0, j)),
            pl.BlockSpec((None, 1, D_MODEL), lambda l, j: (l, 0, j)),
        ],
        out_specs=pl.BlockSpec((None, N_COND, D_MODEL), lambda l, j: (l, 0, j)),
        out_shape=jax.ShapeDtypeStruct((DEPTH, N_COND, 6 * D_MODEL), F32),
        compiler_params=_cparams(2),
        name="modulation",
    )(cond, w_mod, b_mod.reshape(DEPTH, 1, 6 * D_MODEL))


RESIDENT = dict(pipeline_mode=pl.Buffered(1))


def _stream_specs(x, tile):
    if not isinstance(x, tuple):
        return [x], [pl.BlockSpec((tile, x.shape[1]), lambda i: (i, 0))]
    n_ctx = x[0].shape[0] // tile
    w = x[0].shape[1]
    return list(x), [pl.BlockSpec((tile, w), lambda i: (jnp.minimum(i, n_ctx - 1), 0)),
                     pl.BlockSpec((tile, w), lambda i: (jnp.maximum(i - n_ctx, 0), 0))]


def _stream_value(refs, is_ctx):
    if len(refs) == 1:
        return refs[0][...]
    return jnp.where(is_ctx, refs[0][...], refs[1][...])


def _inproj_kernel(*refs, splits, x_parts, n_ctx_tiles):
    x_refs = refs[:x_parts]
    gain_ref, mod_ref, w_ref, wt_ref = refs[x_parts:x_parts + 4]
    out_refs = refs[x_parts + 4:]
    m = mod_ref[...]
    x = _stream_value(x_refs, pl.program_id(0) < n_ctx_tiles)
    h = _mod_norm(x, gain_ref[...], m[:, 0:D_MODEL], m[:, D_MODEL:2 * D_MODEL]).astype(BF16)
    for o_ref, (a, b) in zip(out_refs[:-1], splits):
        for c0 in range(a, b, COL_CHUNK):
            c1 = min(c0 + COL_CHUNK, b)
            o_ref[:, c0 - a:c1 - a] = _dot(h, w_ref[:, c0:c1]).astype(o_ref.dtype)
    out_refs[-1][...] = _dot(h, wt_ref[...])


def _inproj(x, gain, mod_l, w_all, layer, w_tail, splits, dtypes, name):
    n_c = w_all.shape[2]
    tok = lambda i: (i, 0)
    const = lambda i: (0, 0)
    widths = [b - a for a, b in splits] + [LANES]
    x_args, x_specs = _stream_specs(x, TOKEN_TILE)
    return pl.pallas_call(
        functools.partial(_inproj_kernel, splits=splits, x_parts=len(x_args), n_ctx_tiles=N_PROMPT_TILES),
        grid=(N_TOK // TOKEN_TILE,),
        in_specs=x_specs + [
            pl.BlockSpec((1, D_MODEL), const),
            pl.BlockSpec((None, 1, 6 * D_MODEL), lambda i: (_mod_row(i), 0, 0)),
            pl.BlockSpec((None, D_MODEL, n_c), lambda i: (layer, 0, 0), **RESIDENT),
            pl.BlockSpec((D_MODEL, LANES), const),
        ],
        out_specs=[pl.BlockSpec((TOKEN_TILE, w), tok) for w in widths],
        out_shape=[jax.ShapeDtypeStruct((N_TOK, w), dt) for w, dt in zip(widths, tuple(dtypes) + (F32,))],
        compiler_params=_cparams(1),
        name=name,
    )(*x_args, gain.reshape(1, D_MODEL), mod_l, w_all, w_tail)


def _outproj_mlp_kernel(*refs, k_sizes, x_parts, n_ctx_tiles):
    x_refs = refs[:x_parts]
    y_refs = refs[x_parts:x_parts + 2 * len(k_sizes)]
    wo_ref, gain_ref, mod_ref, w1_ref, w2_ref, o_ref = refs[x_parts + 2 * len(k_sizes):]
    m = mod_ref[...]
    g1 = m[:, 2 * D_MODEL:3 * D_MODEL]
    sh2 = m[:, 3 * D_MODEL:4 * D_MODEL]
    sc2 = m[:, 4 * D_MODEL:5 * D_MODEL]
    g2 = m[:, 5 * D_MODEL:6 * D_MODEL]
    is_ctx = pl.program_id(0) < n_ctx_tiles
    proj = None
    off = 0
    for idx, k in enumerate(k_sizes):
        y = _stream_value(y_refs[2 * idx:2 * idx + 2], is_ctx)
        part = _dot(y, wo_ref[off:off + k, :])
        proj = part if proj is None else proj + part
        off += k
    x1 = _stream_value(x_refs, is_ctx) + g1 * proj
    h = _mod_norm(x1, gain_ref[...], sh2, sc2).astype(BF16)
    acc = jnp.zeros(x1.shape, F32)
    for c0 in range(0, MLP_HIDDEN, COL_CHUNK):
        a = jnp.maximum(_dot(h, w1_ref[:, c0:c0 + COL_CHUNK]), 0.0)
        acc = acc + _dot((a * a).astype(BF16), w2_ref[c0:c0 + COL_CHUNK, :])
    o_ref[...] = x1 + g2 * acc


def _outproj_mlp(x, ys, w_out_all, j, gain, mod_l, w1_all, w2_all, layer, name):
    k_sizes = tuple(yp.shape[1] for yp, _ in ys)
    k_in = sum(k_sizes)
    tile = MLP_TILE
    per_row = TOKEN_TILE // tile
    n_ctx_tiles = N_PROMPT_TOK // tile
    tok = lambda i: (i, 0)
    const = lambda i: (0, 0)
    x_args, x_specs = _stream_specs(x, tile)
    y_args, y_specs = [], []
    for pair in ys:
        a, s = _stream_specs(pair, tile)
        y_args += a
        y_specs += s
    return pl.pallas_call(
        functools.partial(_outproj_mlp_kernel, k_sizes=k_sizes, x_parts=len(x_args), n_ctx_tiles=n_ctx_tiles),
        grid=(N_TOK // tile,),
        in_specs=x_specs + y_specs + [
            pl.BlockSpec((None, k_in, D_MODEL), lambda i: (j, 0, 0), **RESIDENT),
            pl.BlockSpec((1, D_MODEL), const),
            pl.BlockSpec((None, 1, 6 * D_MODEL), lambda i: (_mod_row(i // per_row), 0, 0)),
            pl.BlockSpec((None, D_MODEL, MLP_HIDDEN), lambda i: (layer, 0, 0), **RESIDENT),
            pl.BlockSpec((None, MLP_HIDDEN, D_MODEL), lambda i: (layer, 0, 0), **RESIDENT),
        ],
        out_specs=pl.BlockSpec((tile, D_MODEL), tok),
        out_shape=jax.ShapeDtypeStruct((N_TOK, D_MODEL), F32),
        compiler_params=_cparams(1),
        name=name,
    )(*x_args, *y_args, w_out_all, gain.reshape(1, D_MODEL), mod_l, w1_all, w2_all)


def _final_norm_kernel(x_ref, g_ref, o_ref):
    x = x_ref[...]
    o_ref[...] = x * lax.rsqrt(jnp.mean(x * x, axis=-1, keepdims=True) + NORM_EPS) * g_ref[...]


def _final_norm(x, gain, tok0, n_tok):
    t0 = tok0 // TOKEN_TILE
    return pl.pallas_call(
        _final_norm_kernel,
        grid=(n_tok // TOKEN_TILE,),
        in_specs=[pl.BlockSpec((TOKEN_TILE, D_MODEL), lambda i: (t0 + i, 0)),
                  pl.BlockSpec((1, D_MODEL), lambda i: (0, 0))],
        out_specs=pl.BlockSpec((TOKEN_TILE, D_MODEL), lambda i: (i, 0)),
        out_shape=jax.ShapeDtypeStruct((n_tok, D_MODEL), F32),
        compiler_params=_cparams(1),
        name="final_norm",
    )(x, gain.reshape(1, D_MODEL))


def _rope_arrays(n_tok):
    axis_dim = ATT_HEAD_DIM // 2
    quarter = ATT_HEAD_DIM // 4
    inv_freq = ROPE_THETA ** (-jnp.arange(0, axis_dim, 2, dtype=F32) / axis_dim)
    t = jnp.arange(n_tok)
    row = (t // GRID_W).astype(F32)
    col = (t % GRID_W).astype(F32)
    d = np.arange(LANES) % ATT_HEAD_DIM
    part = d // axis_dim
    within = d % axis_dim
    freq = inv_freq[jnp.asarray(within % quarter)]
    pos = jnp.where(jnp.asarray(part)[None, :] == 0, row[:, None], col[:, None])
    ang = pos * freq[None, :]
    cos = jnp.cos(ang)
    sin = jnp.sin(ang)
    fh = jnp.asarray(within < quarter)[None, :]
    return cos, jnp.where(fh, -sin, 0.0), jnp.where(fh, 0.0, sin)


def _apply_rope(x, cos, sin_a, sin_b):
    quarter = ATT_HEAD_DIM // 4
    return (x * cos + pltpu.roll(x, LANES - quarter, axis=1) * sin_a
            + pltpu.roll(x, quarter, axis=1) * sin_b)


def _head_rms(x, gmat, gain):
    ss = _dot_xl(x * x, gmat)
    return x * lax.rsqrt(ss * (1.0 / ATT_HEAD_DIM) + NORM_EPS) * gain


def _attn_kernel(*refs, n, n_ctx, rope, emit_k, tq):
    it = iter(refs)
    q_ref, k_ref, v_ref = next(it), next(it), next(it)
    qg_ref, kg_ref, gmat_ref = next(it), next(it), next(it)
    if n_ctx:
        ck_ref, cv_ref = next(it), next(it)
    if rope:
        cos_ref, sa_ref, sb_ref = next(it), next(it), next(it)
    o_ref = next(it)
    if emit_k:
        ko_ref = next(it)
    keys_ref, vals_ref = next(it), next(it)

    gmat = gmat_ref[...]
    kn = _head_rms(k_ref[...], gmat, kg_ref[...])
    if emit_k:
        ko_ref[...] = kn
    if rope:
        kn = _apply_rope(kn, cos_ref[...], sa_ref[...], sb_ref[...])
    keys_ref[0:n, :] = kn.astype(BF16)
    vals_ref[0:n, :] = v_ref[...].astype(BF16)
    if n_ctx:
        keys_ref[n:n + n_ctx, :] = ck_ref[...].astype(BF16)
        vals_ref[n:n + n_ctx, :] = cv_ref[...].astype(BF16)
    low = lax.broadcasted_iota(jnp.int32, (1, LANES), 1) < ATT_HEAD_DIM
    scale = ATT_HEAD_DIM ** -0.5
    heads_per_tile = LANES // ATT_HEAD_DIM
    tiles_per_kv = (ATT_HEADS // ATT_KV_HEADS) // heads_per_tile

    n_tiles = ATT_Q_DIM // LANES

    def q_tile(qi, carry):
        rows = pl.ds(pl.multiple_of(qi * tq, tq), tq)
        qm = []
        for t in range(n_tiles):
            qn = _head_rms(q_ref[rows, t * LANES:(t + 1) * LANES], gmat, qg_ref[...])
            if rope:
                qn = _apply_rope(qn, cos_ref[rows, :], sa_ref[rows, :], sb_ref[rows, :])
            qn = qn * scale
            qs = pltpu.roll(qn, ATT_HEAD_DIM, axis=1)
            if t // tiles_per_kv == 0:
                qm += [jnp.where(low, qn, 0.0), jnp.where(low, qs, 0.0)]
            else:
                qm += [jnp.where(low, 0.0, qs), jnp.where(low, 0.0, qn)]
        keys = keys_ref[...]
        vals = vals_ref[...]
        scores = [_dot_nt(x.astype(BF16), keys) for x in qm]
        probs, inv_l = [], []
        for s in scores:
            p = jnp.exp(s - jnp.max(s, axis=-1, keepdims=True))
            inv_l.append(1.0 / jnp.sum(p, axis=-1, keepdims=True))
            probs.append(p.astype(BF16))
        res = [_dot(p, vals) * il for p, il in zip(probs, inv_l)]
        for t in range(n_tiles):
            r_e, r_o = res[2 * t], res[2 * t + 1]
            if t // tiles_per_kv == 0:
                o = jnp.where(low, r_e, pltpu.roll(r_o, ATT_HEAD_DIM, axis=1))
            else:
                o = jnp.where(low, pltpu.roll(r_e, ATT_HEAD_DIM, axis=1), r_o)
            o_ref[rows, t * LANES:(t + 1) * LANES] = o.astype(o_ref.dtype)
        return carry

    lax.fori_loop(0, n // tq, q_tile, 0)


def _attention(q, k, v, q_gain, k_gain, *, n_batch, n, tok0, ctx=None, emit_k=False):
    rope = ctx is not None
    n_ctx = ctx[0].shape[1] if rope else 0
    b0 = tok0 // n
    tq = 128 if rope else n
    tokb = lambda b: (b0 + b, 0)
    const = lambda b: (0, 0)
    heads_per_tile = LANES // ATT_HEAD_DIM
    gmat = jnp.asarray(np.kron(np.eye(heads_per_tile), np.ones((ATT_HEAD_DIM, ATT_HEAD_DIM))), BF16)
    args = [q, k, v, jnp.tile(q_gain, heads_per_tile).reshape(1, LANES),
            jnp.tile(k_gain, heads_per_tile).reshape(1, LANES), gmat]
    in_specs = [
        pl.BlockSpec((n, ATT_Q_DIM), tokb),
        pl.BlockSpec((n, ATT_KV_DIM), tokb),
        pl.BlockSpec((n, ATT_KV_DIM), tokb),
        pl.BlockSpec((1, LANES), const),
        pl.BlockSpec((1, LANES), const),
        pl.BlockSpec((LANES, LANES), const),
    ]
    if rope:
        args += [ctx[0], ctx[1]]
        in_specs += [pl.BlockSpec((None, n_ctx, ATT_KV_DIM), lambda b: (b, 0, 0))] * 2
        args += list(_rope_arrays(n))
        in_specs += [pl.BlockSpec((n, LANES), const)] * 3
    out_shape = [jax.ShapeDtypeStruct((n_batch * n, ATT_Q_DIM), BF16)]
    out_specs = [pl.BlockSpec((n, ATT_Q_DIM), lambda b: (b, 0))]
    if emit_k:
        out_shape.append(jax.ShapeDtypeStruct((n_batch * n, ATT_KV_DIM), F32))
        out_specs.append(pl.BlockSpec((n, ATT_KV_DIM), lambda b: (b, 0)))
    return pl.pallas_call(
        functools.partial(_attn_kernel, n=n, n_ctx=n_ctx, rope=rope, emit_k=emit_k, tq=tq),
        grid=(n_batch,),
        in_specs=in_specs,
        out_specs=out_specs,
        out_shape=out_shape,
        scratch_shapes=[pltpu.VMEM((n + n_ctx, ATT_KV_DIM), BF16),
                        pltpu.VMEM((n + n_ctx, ATT_KV_DIM), BF16)],
        compiler_params=_cparams(1),
        name="attention_latent" if rope else "attention_context",
    )(*args)


def _conv_silu(x, w, bias, n):
    row = lax.broadcasted_iota(jnp.int32, x.shape, 0)
    prev = jnp.where(row == 0, 0.0, pltpu.roll(x, 1, axis=0))
    nxt = jnp.where(row == n - 1, 0.0, pltpu.roll(x, n - 1, axis=0))
    y = prev * w[0:1, :] + x * w[1:2, :] + nxt * w[2:3, :]
    if bias is not None:
        y = y + bias
    return _silu(y)


DN_CAT = DN_HEADS * CHUNK
DN_EXPAND_W = DN_CAT + DN_QK_DIM
DN_ROWS_PER_STEP = 1024
DN_MIN_SEQ_PER_STEP = 2


def _to_stack(x):
    return jnp.concatenate([x[:, h * LANES:(h + 1) * LANES] for h in range(DN_HEADS)], axis=0)


def _delta_kernel(*refs, n, n_seq, has_s0, emit_s, n_prev):
    it = iter(refs)
    dqkv_ref, dz_ref, small_ref, convw_ref, pcol_ref, gain_ref, expand_ref, cum_ref = (next(it) for _ in range(8))
    if has_s0:
        s0_ref = next(it)
    if n_prev:
        prev_ref = next(it)
    o_ref = next(it)
    if emit_s:
        sfin_ref = next(it)
    q_scr, k_scr, v_scr, g_scr, b_scr, of_scr, ob_scr, s_scr = (next(it) for _ in range(8))

    n_chunks = n // CHUNK
    n_hd = 2 * DN_HEADS
    for s in range(n_seq):
        seq = slice(s * n, (s + 1) * n)
        for h in range(DN_HEADS):
            for part, scr in ((0, q_scr), (1, k_scr), (2, v_scr)):
                c0 = part * DN_QK_DIM + h * DN_KEY_DIM
                x = _conv_silu(dqkv_ref[seq, c0:c0 + LANES].astype(F32), convw_ref[:, c0:c0 + LANES], None, n)
                if part < 2:
                    x = x * lax.rsqrt(jnp.sum(x * x, axis=-1, keepdims=True) + NORM_EPS)
                if part == 0:
                    x = x * (DN_KEY_DIM ** -0.5)
                scr[seq, h * LANES:(h + 1) * LANES] = x
    small = small_ref[...]
    pcol = pcol_ref[...]
    lane = lax.broadcasted_iota(jnp.int32, (1, LANES), 1)
    b_scr[...] = jnp.where(lane < n_hd, _sigmoid(small), 0.0)
    g_scr[...] = jnp.where(lane < n_hd, 0.0, jnp.where(
        lane < 2 * n_hd, -jnp.exp(pcol[0:1, :]) * _softplus(small + pcol[1:2, :]), 0.0))
    if has_s0:
        s_scr[...] = s0_ref[...]
    else:
        s_scr[...] = jnp.zeros(s_scr.shape, F32)

    rc = lax.broadcasted_iota(jnp.int32, (CHUNK, DN_CAT), 0)
    cc = lax.broadcasted_iota(jnp.int32, (CHUNK, DN_CAT), 1) & (CHUNK - 1)
    eye_cat = jnp.where(rc == cc, 1.0, 0.0)
    r4 = lax.broadcasted_iota(jnp.int32, (DN_CAT, DN_CAT), 0)
    c4 = lax.broadcasted_iota(jnp.int32, (DN_CAT, DN_CAT), 1)
    blk_sq = jnp.where(_shr(r4, 6) == _shr(c4, 6), 1.0, 0.0).astype(BF16)
    r5 = lax.broadcasted_iota(jnp.int32, (DN_CAT, DN_QK_DIM), 0)
    c5 = lax.broadcasted_iota(jnp.int32, (DN_CAT, DN_QK_DIM), 1)
    blk_wide = jnp.where(_shr(r5, 6) == _shr(c5, 7), 1.0, 0.0).astype(BF16)
    n_levels = int(math.log2(CHUNK))
    incl, strict_f, off_masks = [], [], []
    for d in range(2):
        upper = d == 1
        incl.append((rc <= cc) if upper else (rc >= cc))
        strict_f.append(jnp.where((rc < cc) if upper else (rc > cc), 1.0, 0.0))
        masks = []
        for lvl in range(n_levels):
            same_pair = _shr(rc, lvl + 1) == _shr(cc, lvl + 1)
            half_r, half_c = _shr(rc, lvl), _shr(cc, lvl)
            side = (half_r < half_c) if upper else (half_r > half_c)
            masks.append(jnp.where(same_pair, jnp.where(side, 1.0, 0.0), 0.0))
        off_masks.append(masks)

    def block_diag(x_cat):
        return _rep_rows(x_cat.astype(BF16), DN_HEADS) * blk_sq

    def block_wide(x_stack):
        return _rep_lanes(x_stack.astype(BF16), DN_HEADS) * blk_wide

    chains = [(s, d) for s in range(n_seq) for d in range(2)]

    def chunk_step(c):
        rows = {}
        for s, d in chains:
            chunk = c if d == 0 else n_chunks - 1 - c
            rows[s, d] = pl.ds(pl.multiple_of(s * n + chunk * CHUNK, CHUNK), CHUNK)
        gb, ex = {}, {}
        for ch in chains:
            pieces = _split3(jnp.concatenate([g_scr[rows[ch], :], b_scr[rows[ch], :]], axis=0))
            gb[ch] = _dot(cum_ref[ch[1]], jnp.concatenate(pieces, axis=0))
        for ch in chains:
            ex[ch] = _dot(jnp.concatenate(_split3(gb[ch]), axis=1), expand_ref[ch[1]])
        a, kb, k, q, egc, gc_wide, g_tot, decay, beta_wide = ({} for _ in range(9))
        for ch in chains:
            upper = ch[1] == 1
            gc_cat = ex[ch][0:CHUNK, 0:DN_CAT]
            gc_wide[ch] = ex[ch][0:CHUNK, DN_CAT:]
            beta_wide[ch] = ex[ch][CHUNK:2 * CHUNK, DN_CAT:]
            gr_cat = jnp.sum(gc_cat * eye_cat, axis=0, keepdims=True)
            decay[ch] = jnp.exp(jnp.where(incl[ch[1]], gc_cat - gr_cat, NEG_BIG))
            g_tot[ch] = gc_wide[ch][0:1, :] if upper else gc_wide[ch][CHUNK - 1:CHUNK, :]
            egc[ch] = jnp.exp(gc_wide[ch])
            q[ch] = q_scr[rows[ch], :]
            k[ch] = k_scr[rows[ch], :]
            kb[ch] = k[ch] * beta_wide[ch]
            k_bd = _rep_rows(k[ch].astype(BF16), DN_HEADS) * blk_wide
            a[ch] = _dot_nt(jnp.concatenate([kb[ch], q[ch]], axis=0).astype(BF16), k_bd)
        m, qk, p = {}, {}, {}
        for ch in chains:
            m[ch] = a[ch][0:CHUNK] * decay[ch] * strict_f[ch[1]]
            qk[ch] = a[ch][CHUNK:2 * CHUNK] * decay[ch]
            p[ch] = eye_cat - m[ch] * off_masks[ch[1]][0]
        for lvl in range(1, n_levels):
            t1 = {ch: _dot(p[ch].astype(BF16), block_diag(m[ch] * off_masks[ch[1]][lvl])) for ch in chains}
            for ch in chains:
                p[ch] = p[ch] - _dot(t1[ch].astype(BF16), block_diag(p[ch]))
        uw, s_old, wq = {}, {}, {}
        for ch in chains:
            v = v_scr[rows[ch], :]
            rhs = jnp.concatenate([_to_stack(v * beta_wide[ch]), _to_stack(kb[ch] * egc[ch])], axis=1)
            uw[ch] = _dot(block_diag(p[ch]), rhs.astype(BF16))
        for ch in chains:
            s_old[ch] = s_scr[ch[0], ch[1]]
            lhs = jnp.concatenate([block_wide(uw[ch][:, DN_VAL_DIM:]),
                                   block_wide(_to_stack(q[ch] * egc[ch]))], axis=0)
            wq[ch] = _dot(lhs, s_old[ch].astype(BF16))
        for ch in chains:
            v_new = (uw[ch][:, 0:DN_VAL_DIM] - wq[ch][0:DN_CAT]).astype(BF16)
            o = wq[ch][DN_CAT:2 * DN_CAT] + _dot(block_diag(qk[ch]), v_new)
            k_dec = _to_stack(k[ch] * jnp.exp(g_tot[ch] - gc_wide[ch]))
            dec = jnp.concatenate(
                [jnp.broadcast_to(jnp.exp(g_tot[ch][:, h * LANES:(h + 1) * LANES]), (DN_KEY_DIM, LANES))
                 for h in range(DN_HEADS)], axis=0)
            s_scr[ch[0], ch[1]] = s_old[ch] * dec + _dot_tn(block_wide(k_dec), v_new)
            o_scr = ob_scr if ch[1] == 1 else of_scr
            for h in range(DN_HEADS):
                o_scr[rows[ch], h * LANES:(h + 1) * LANES] = o[h * CHUNK:(h + 1) * CHUNK]

    def body(c, carry):
        chunk_step(c)
        return carry

    lax.fori_loop(0, n_chunks, body, 0)

    if emit_s:
        if n_prev:
            sfin_ref[:, 0:n_prev] = prev_ref[...]
        sfin_ref[:, n_prev] = s_scr[...]
    for h in range(DN_HEADS):
        cols = slice(h * LANES, (h + 1) * LANES)
        o = of_scr[:, cols] + ob_scr[:, cols]
        o = o * lax.rsqrt(jnp.mean(o * o, axis=-1, keepdims=True) + NORM_EPS) * gain_ref[...]
        o_ref[:, cols] = (o * _silu(dz_ref[:, cols].astype(F32))).astype(o_ref.dtype)


def _delta(dqkv, dz, small, conv_w, a_log, dt_bias, out_gain, *, n_batch, n, tok0, s0=None, emit_s=False,
           prev=None):
    n_seq = max(DN_MIN_SEQ_PER_STEP, DN_ROWS_PER_STEP // n)
    rows = n_seq * n
    b0 = tok0 // rows
    n_hd = 2 * DN_HEADS
    tokb = lambda b: (b0 + b, 0)
    const = lambda b: (0, 0)
    const3 = lambda b: (0, 0, 0)
    pcol = jnp.zeros((2, LANES), F32)
    pcol = pcol.at[0, n_hd:2 * n_hd].set(a_log.reshape(-1)).at[1, n_hd:2 * n_hd].set(dt_bias.reshape(-1))
    expand = np.zeros((2, LANES, DN_EXPAND_W), np.float32)
    for d in range(2):
        for h in range(DN_HEADS):
            for src in (d * DN_HEADS + h, n_hd + d * DN_HEADS + h):
                expand[d, src, h * CHUNK:(h + 1) * CHUNK] = 1.0
                expand[d, src, DN_CAT + h * LANES:DN_CAT + (h + 1) * LANES] = 1.0
    expand = np.tile(expand, (1, 3, 1))
    tri = np.tril(np.ones((CHUNK, CHUNK), np.float32))
    eye = np.eye(CHUNK, dtype=np.float32)
    zero = np.zeros((CHUNK, CHUNK), np.float32)
    cum = np.stack([np.block([[t, zero] * 3, [zero, eye] * 3]) for t in (tri, tri.T)])
    args = [dqkv, dz, small, conv_w, pcol, out_gain.reshape(1, LANES), jnp.asarray(expand, BF16),
            jnp.asarray(cum, BF16)]
    in_specs = [
        pl.BlockSpec((rows, DN_CONV_DIM), tokb),
        pl.BlockSpec((rows, DN_V_DIM), tokb),
        pl.BlockSpec((rows, LANES), tokb),
        pl.BlockSpec((3, DN_CONV_DIM), const),
        pl.BlockSpec((2, LANES), const),
        pl.BlockSpec((1, LANES), const),
        pl.BlockSpec((2, 3 * LANES, DN_EXPAND_W), const3),
        pl.BlockSpec((2, 2 * CHUNK, 6 * CHUNK), const3),
    ]
    state_shape = (2, DN_HEADS * DN_KEY_DIM, DN_VAL_DIM)
    state_spec = pl.BlockSpec((n_seq,) + state_shape, lambda b: (b, 0, 0, 0))
    if s0 is not None:
        args.append(s0.reshape((n_batch,) + state_shape))
        in_specs.append(state_spec)
    n_prev = 0 if prev is None else prev.shape[1]
    stacked_spec = lambda k: pl.BlockSpec((n_seq, k) + state_shape, lambda b: (b, 0, 0, 0, 0))
    if n_prev:
        args.append(prev)
        in_specs.append(stacked_spec(n_prev))
    out_shape = [jax.ShapeDtypeStruct((n_batch * n, DN_V_DIM), BF16)]
    out_specs = [pl.BlockSpec((rows, DN_V_DIM), lambda b: (b, 0))]
    if emit_s:
        out_shape.append(jax.ShapeDtypeStruct((n_batch, n_prev + 1) + state_shape, F32))
        out_specs.append(stacked_spec(n_prev + 1))
    return pl.pallas_call(
        functools.partial(_delta_kernel, n=n, n_seq=n_seq, has_s0=s0 is not None, emit_s=emit_s, n_prev=n_prev),
        grid=(n_batch // n_seq,),
        in_specs=in_specs,
        out_specs=out_specs,
        out_shape=out_shape,
        scratch_shapes=[
            pltpu.VMEM((rows, DN_QK_DIM), F32), pltpu.VMEM((rows, DN_QK_DIM), F32),
            pltpu.VMEM((rows, DN_V_DIM), F32),
            pltpu.VMEM((rows, LANES), F32), pltpu.VMEM((rows, LANES), F32),
            pltpu.VMEM((rows, DN_V_DIM), F32), pltpu.VMEM((rows, DN_V_DIM), F32),
            pltpu.VMEM((n_seq,) + state_shape, F32),
        ],
        compiler_params=_cparams(1),
        name="delta_latent" if s0 is not None else "delta_context",
    )(*args)


SSD_CHUNK = 256
SSM_GROUP_W = SSM_INNER // SSM_GROUPS
HEADS_PER_TILE = LANES // SSM_HEAD_DIM


def _ssd_kernel(*refs, n, has_s0, emit_s, n_prev):
    it = iter(refs)
    z_ref, xbc_ref, dtc_ref, convw_ref, convb_ref, pcol_ref = (next(it) for _ in range(6))
    dskip_ref, gain_ref = next(it), next(it)
    if has_s0:
        s0_ref = next(it)
    if n_prev:
        prev_ref = next(it)
    o_ref = next(it)
    if emit_s:
        sfin_ref = next(it)
    xs_scr, bc_scr, y_scr, dt_scr, st_scr = (next(it) for _ in range(5))

    q_len = SSD_CHUNK
    n_chunks = n // q_len
    carry_state = has_s0 or n_chunks > 1

    def conv_tile(t):
        cols = pl.ds(pl.multiple_of(t * LANES, LANES), LANES)
        return _conv_silu(xbc_ref[:, cols].astype(F32), convw_ref[:, cols], convb_ref[:, cols], n)

    def prep_x(t, carry):
        cols = pl.ds(pl.multiple_of(t * LANES, LANES), LANES)
        x = conv_tile(t)
        xs_scr[:, cols] = x
        y_scr[:, cols] = x * dskip_ref[:, cols]
        return carry

    def prep_bc(t, carry):
        cols = pl.ds(pl.multiple_of(t * LANES, LANES), LANES)
        bc_scr[:, cols] = conv_tile(t + SSM_INNER // LANES).astype(BF16)
        return carry

    lax.fori_loop(0, SSM_INNER // LANES, prep_x, 0)
    lax.fori_loop(0, 2 * SSM_BC_DIM // LANES, prep_bc, 0)
    pcol = pcol_ref[...]
    dt_scr[...] = _softplus(dtc_ref[...] + pcol[1:2, :])
    neg_a_col = -jnp.exp(pcol[0:1, :])
    for d in range(2):
        for g in range(SSM_GROUPS):
            gs = slice(g * SSM_GROUP_W, (g + 1) * SSM_GROUP_W)
            if has_s0:
                st_scr[d, :, gs] = s0_ref[d, gs, :].T
            else:
                st_scr[d, :, gs] = jnp.zeros((SSM_STATE, SSM_GROUP_W), F32)

    ri = lax.broadcasted_iota(jnp.int32, (q_len, q_len), 0)
    ci = lax.broadcasted_iota(jnp.int32, (q_len, q_len), 1)
    tri3_b = (_rep_lanes((ri >= ci).astype(BF16), 3), _rep_lanes((ri <= ci).astype(BF16), 3))
    low = lax.broadcasted_iota(jnp.int32, (1, LANES), 1) < SSM_HEAD_DIM
    heads_per_group = SSM_HEADS // SSM_GROUPS

    def decays(c, d):
        upper = d == 1
        rows = pl.ds(pl.multiple_of(c * q_len, q_len), q_len)
        dt = dt_scr[rows, :]
        acum = _dot(tri3_b[d], jnp.concatenate(_split3(dt * neg_a_col), axis=0))
        a_tot = acum[0:1, :] if upper else acum[q_len - 1:q_len, :]
        w_col = dt * jnp.exp(a_tot - acum)
        return dict(rows=rows, acum=acum, acum_r=acum.T, dt_r=dt.T, w_r=w_col.T, eac=jnp.exp(acum),
                    ea_tot=jnp.exp(a_tot))

    half = q_len // 2
    rh = lax.broadcasted_iota(jnp.int32, (half, half), 0)
    ch = lax.broadcasted_iota(jnp.int32, (half, half), 1)
    incl_half = (rh >= ch, rh <= ch)

    def decay_weights(pd, l, cb, upper):
        col = pd["acum"][:, l:l + 1]
        row = pd["acum_r"][l:l + 1, :]
        dt_row = pd["dt_r"][l:l + 1, :]
        first, second = slice(0, half), slice(half, q_len)

        def quarter(r, c, masked):
            diff = col[r] - row[:, c]
            if masked:
                diff = jnp.where(incl_half[1 if upper else 0], diff, NEG_BIG)
            return cb[r, c] * (jnp.exp(diff) * dt_row[:, c])

        zero = jnp.zeros((half, half), F32)
        if upper:
            top = [quarter(first, first, True), quarter(first, second, False)]
            bottom = [zero, quarter(second, second, True)]
        else:
            top = [quarter(first, first, True), zero]
            bottom = [quarter(second, first, False), quarter(second, second, True)]
        return jnp.concatenate([jnp.concatenate(top, axis=1), jnp.concatenate(bottom, axis=1)], axis=0)

    def chunk_step(c_fwd, c_bwd, same_chunk):
        pre = (decays(c_fwd, 0), decays(c_bwd, 1))
        for g in range(SSM_GROUPS):
            shared = None
            for d in range(2):
                pd = pre[d]
                rows = pd["rows"]
                if shared is None or not same_chunk:
                    bm_g = bc_scr[rows, g * SSM_STATE:(g + 1) * SSM_STATE]
                    cm_g = bc_scr[rows, SSM_BC_DIM + g * SSM_STATE:SSM_BC_DIM + (g + 1) * SSM_STATE]
                    shared = (cm_g, _dot_nt(cm_g, bm_g), bm_g.astype(F32).T)
                cm_g, cb, bm_t = shared
                for tt in range(heads_per_group // HEADS_PER_TILE):
                    h0 = g * heads_per_group + tt * HEADS_PER_TILE
                    ls = slice(g * SSM_GROUP_W + tt * LANES, g * SSM_GROUP_W + (tt + 1) * LANES)
                    w_parts, b_parts = [], []
                    for hh in range(HEADS_PER_TILE):
                        l = d * SSM_HEADS + h0 + hh
                        w_parts.append(decay_weights(pd, l, cb, d == 1))
                        b_parts.append(bm_t * pd["w_r"][l:l + 1, :])
                    lhs = jnp.concatenate([jnp.concatenate(w_parts, axis=1),
                                           jnp.concatenate(b_parts, axis=1)], axis=0).astype(BF16)
                    xt = xs_scr[rows, ls]
                    bd = jnp.concatenate([jnp.where(low, xt, 0.0), jnp.where(low, 0.0, xt)], axis=0)
                    res = _dot(lhs, bd.astype(BF16))
                    y_new = y_scr[rows, ls] + res[0:q_len]
                    st_inc = res[q_len:q_len + SSM_STATE]
                    if carry_state:
                        st = st_scr[d, :, ls]
                        l0 = d * SSM_HEADS + h0

                        def pick(a, l0=l0):
                            return jnp.where(low, a[:, l0:l0 + 1], a[:, l0 + 1:l0 + 2])

                        y_new = y_new + _dot(cm_g, st.astype(BF16)) * pick(pd["eac"])
                        st_inc = st * pick(pd["ea_tot"]) + st_inc
                    y_scr[rows, ls] = y_new
                    st_scr[d, :, ls] = st_inc

    if n_chunks == 1:
        chunk_step(0, 0, True)
    else:
        def body(c, carry):
            chunk_step(c, n_chunks - 1 - c, False)
            return carry

        lax.fori_loop(0, n_chunks, body, 0)

    if emit_s:
        if n_prev:
            sfin_ref[0:n_prev] = prev_ref[...]
        for d in range(2):
            for g in range(SSM_GROUPS):
                gs = slice(g * SSM_GROUP_W, (g + 1) * SSM_GROUP_W)
                sfin_ref[n_prev, d, gs, :] = st_scr[d, :, gs].T
    row_tile = 128

    def finish(r, carry):
        rows = pl.ds(pl.multiple_of(r * row_tile, row_tile), row_tile)
        y = y_scr[rows, :] * _silu(z_ref[rows, :].astype(F32))
        y = y * lax.rsqrt(jnp.mean(y * y, axis=-1, keepdims=True) + NORM_EPS) * gain_ref[...]
        o_ref[rows, :] = y.astype(o_ref.dtype)
        return carry

    lax.fori_loop(0, n // row_tile, finish, 0)


def _ssd(z, xbc, dt_raw, conv_w, conv_b, a_log, dt_bias, d_skip, out_gain, *, n_batch, n, tok0,
         s0=None, emit_s=False, prev=None):
    b0 = tok0 // n
    tokb = lambda b: (b0 + b, 0)
    const = lambda b: (0, 0)
    pcol = jnp.zeros((2, LANES), F32)
    pcol = pcol.at[0, 0:2 * SSM_HEADS].set(a_log.reshape(-1)).at[1, 0:2 * SSM_HEADS].set(dt_bias.reshape(-1))
    args = [z, xbc, dt_raw, conv_w, conv_b.reshape(1, SSM_CONV_DIM), pcol,
            jnp.repeat(d_skip, SSM_HEAD_DIM).reshape(1, SSM_INNER), out_gain.reshape(1, SSM_INNER)]
    big = dict(pipeline_mode=pl.Buffered(1)) if n > SEQ else {}
    in_specs = [
        pl.BlockSpec((n, SSM_INNER), tokb, **big),
        pl.BlockSpec((n, SSM_CONV_DIM), tokb, **big),
        pl.BlockSpec((n, LANES), tokb),
        pl.BlockSpec((3, SSM_CONV_DIM), const),
        pl.BlockSpec((1, SSM_CONV_DIM), const),
        pl.BlockSpec((2, LANES), const),
        pl.BlockSpec((1, SSM_INNER), const),
        pl.BlockSpec((1, SSM_INNER), const),
    ]
    state_spec = pl.BlockSpec((None, 2, SSM_INNER, SSM_STATE), lambda b: (b, 0, 0, 0))
    if s0 is not None:
        args.append(s0.reshape(n_batch, 2, SSM_INNER, SSM_STATE))
        in_specs.append(state_spec)
    n_prev = 0 if prev is None else prev.shape[1]
    stacked_spec = lambda k: pl.BlockSpec((None, k, 2, SSM_INNER, SSM_STATE), lambda b: (b, 0, 0, 0, 0))
    if n_prev:
        args.append(prev)
        in_specs.append(stacked_spec(n_prev))
    out_shape = [jax.ShapeDtypeStruct((n_batch * n, SSM_INNER), BF16)]
    out_specs = [pl.BlockSpec((n, SSM_INNER), lambda b: (b, 0))]
    if emit_s:
        out_shape.append(jax.ShapeDtypeStruct((n_batch, n_prev + 1, 2, SSM_INNER, SSM_STATE), F32))
        out_specs.append(stacked_spec(n_prev + 1))
    return pl.pallas_call(
        functools.partial(_ssd_kernel, n=n, has_s0=s0 is not None, emit_s=emit_s, n_prev=n_prev),
        grid=(n_batch,),
        in_specs=in_specs,
        out_specs=out_specs,
        out_shape=out_shape,
        scratch_shapes=[
            pltpu.VMEM((n, SSM_INNER), F32), pltpu.VMEM((n, 2 * SSM_BC_DIM), BF16),
            pltpu.VMEM((n, SSM_INNER), F32), pltpu.VMEM((n, LANES), F32),
            pltpu.VMEM((2, SSM_STATE, SSM_INNER), F32),
        ],
        compiler_params=_cparams(1),
        name="ssd_latent" if s0 is not None else "ssd_context",
    )(*args)


EVEN_MAIN = ATT_Q_DIM + 2 * ATT_KV_DIM + DN_CONV_DIM + DN_V_DIM
EVEN_SPLITS = ((0, ATT_Q_DIM), (ATT_Q_DIM, ATT_Q_DIM + ATT_KV_DIM),
               (ATT_Q_DIM + ATT_KV_DIM, ATT_Q_DIM + 2 * ATT_KV_DIM),
               (ATT_Q_DIM + 2 * ATT_KV_DIM, ATT_Q_DIM + 2 * ATT_KV_DIM + DN_CONV_DIM),
               (ATT_Q_DIM + 2 * ATT_KV_DIM + DN_CONV_DIM, EVEN_MAIN))
EVEN_DTYPES = (F32, F32, F32, BF16, BF16)
ODD_MAIN = SSM_INNER + SSM_CONV_DIM
ODD_SPLITS = ((0, SSM_INNER), (SSM_INNER, ODD_MAIN))
ODD_DTYPES = (BF16, BF16)


def _tail_cols(w, start):
    tail = w[:, :, start:]
    return jnp.pad(tail, ((0, 0), (0, 0), (0, LANES - tail.shape[2]))).astype(BF16)


def kernel(x_prompt, x_sample, c, cache_attn_k, cache_attn_v, state_delta, state_ssm, c_ctx, norm_mix_g, norm_mlp_g, w_mod, b_mod, w_mlp_in, w_mlp_out, w_in_even, attn_q_norm_g, attn_k_norm_g, delta_conv_w, delta_a_log, delta_dt_bias, delta_norm_g, w_out_even, w_in_odd, ssm_conv_w, ssm_conv_b, ssm_a_log, ssm_dt_bias, ssm_d, ssm_norm_g, w_out_odd, final_norm_g):
    x = (x_prompt.reshape(N_PROMPT_TOK, D_MODEL), x_sample.reshape(N_SAMPLE_TOK, D_MODEL))
    cond = jnp.zeros((N_COND, D_MODEL), F32).at[0].set(c_ctx).at[1:1 + DEC_BATCH].set(c)
    mod = _modulation(cond, w_mod, b_mod).reshape(DEPTH, N_COND, 1, 6 * D_MODEL)
    w_in_even_b = w_in_even[:, :, :EVEN_MAIN].astype(BF16)
    w_in_odd_b = w_in_odd[:, :, :ODD_MAIN].astype(BF16)
    w_out_even_b, w_out_odd_b = _cast_bf16(w_out_even), _cast_bf16(w_out_odd)
    w_mlp_in_b, w_mlp_out_b = _cast_bf16(w_mlp_in), _cast_bf16(w_mlp_out)
    tail_even, tail_odd = _tail_cols(w_in_even, EVEN_MAIN), _tail_cols(w_in_odd, ODD_MAIN)

    ks, vs = [], []
    s_delta = s_ssm = None
    for layer in range(DEPTH):
        j = layer // 2
        if layer % 2 == 0:
            q, k, v, dqkv, dz, small = _inproj(x, norm_mix_g[layer], mod[layer], w_in_even_b, j, tail_even[j],
                                               EVEN_SPLITS, EVEN_DTYPES, "inproj_even")
            ctx_k = cache_attn_k[:, j].reshape(DEC_BATCH, PAST_LEN, ATT_KV_DIM)
            ctx_v = cache_attn_v[:, j].reshape(DEC_BATCH, PAST_LEN, ATT_KV_DIM)
            o_att_p, k_norm = _attention(q, k, v, attn_q_norm_g[j], attn_k_norm_g[j],
                                         n_batch=BATCH, n=SEQ, tok0=0, emit_k=True)
            (o_att_s,) = _attention(q, k, v, attn_q_norm_g[j], attn_k_norm_g[j],
                                    n_batch=DEC_BATCH, n=DEC_SEQ, tok0=N_PROMPT_TOK, ctx=(ctx_k, ctx_v))
            dn_args = (dqkv, dz, small, delta_conv_w[j], delta_a_log[j], delta_dt_bias[j], delta_norm_g[j])
            o_dn_p, s_delta = _delta(*dn_args, n_batch=BATCH, n=SEQ, tok0=0, emit_s=True, prev=s_delta)
            (o_dn_s,) = _delta(*dn_args, n_batch=DEC_BATCH, n=DEC_SEQ, tok0=N_PROMPT_TOK, s0=state_delta[:, j])
            ys = [(o_att_p, o_att_s), (o_dn_p, o_dn_s)]
            w_out_b = w_out_even_b
            ks.append(k_norm.reshape(BATCH, SEQ, ATT_KV_HEADS, ATT_HEAD_DIM))
            vs.append(v[:N_PROMPT_TOK].reshape(BATCH, SEQ, ATT_KV_HEADS, ATT_HEAD_DIM))
        else:
            z, xbc, dt_raw = _inproj(x, norm_mix_g[layer], mod[layer], w_in_odd_b, j, tail_odd[j],
                                     ODD_SPLITS, ODD_DTYPES, "inproj_odd")
            ssd_args = (z, xbc, dt_raw, ssm_conv_w[j], ssm_conv_b[j], ssm_a_log[j], ssm_dt_bias[j], ssm_d[j],
                        ssm_norm_g[j])
            y_p, s_ssm = _ssd(*ssd_args, n_batch=BATCH, n=SEQ, tok0=0, emit_s=True, prev=s_ssm)
            (y_s,) = _ssd(*ssd_args, n_batch=DEC_BATCH, n=DEC_SEQ, tok0=N_PROMPT_TOK, s0=state_ssm[:, j])
            ys = [(y_p, y_s)]
            w_out_b = w_out_odd_b
        x = _outproj_mlp(x, ys, w_out_b, j, norm_mlp_g[layer], mod[layer], w_mlp_in_b, w_mlp_out_b, layer,
                         "outproj_mlp_even" if layer % 2 == 0 else "outproj_mlp_odd")

    y_prompt = _final_norm(x, final_norm_g, 0, N_PROMPT_TOK).reshape(BATCH, SEQ, D_MODEL)
    y_sample = _final_norm(x, final_norm_g, N_PROMPT_TOK, N_SAMPLE_TOK).reshape(DEC_BATCH, DEC_SEQ, D_MODEL)
    return (y_prompt, y_sample, jnp.stack(ks, axis=1), jnp.stack(vs, axis=1),
            s_delta.reshape(BATCH, N_EVEN, 2, DN_HEADS, DN_KEY_DIM, DN_VAL_DIM),
            s_ssm.reshape(BATCH, N_ODD, 2, SSM_HEADS, SSM_HEAD_DIM, SSM_STATE))
```

```python
import functools
import math

import jax
import jax.numpy as jnp
import numpy as np
from jax import lax
from jax.experimental import pallas as pl
from jax.experimental.pallas import tpu as pltpu

F32 = jnp.float32
BF16 = jnp.bfloat16

D_MODEL = 1024
BATCH = 32
SEQ = 256
DEPTH = 4
DEC_BATCH = 4
DEC_SEQ = 1024
PAST_LEN = 512
GRID_W = 64
N_EVEN = (DEPTH + 1) // 2
N_ODD = DEPTH // 2
ATT_HEAD_DIM = 64
ATT_HEADS = 8
ATT_KV_HEADS = 2
ATT_Q_DIM = ATT_HEADS * ATT_HEAD_DIM
ATT_KV_DIM = ATT_KV_HEADS * ATT_HEAD_DIM
DN_KEY_DIM = 128
DN_VAL_DIM = 128
DN_HEADS = 4
DN_QK_DIM = DN_HEADS * DN_KEY_DIM
DN_V_DIM = DN_HEADS * DN_VAL_DIM
DN_CONV_DIM = 2 * DN_QK_DIM + DN_V_DIM
SSM_INNER = 2 * D_MODEL
SSM_HEAD_DIM = 64
SSM_HEADS = SSM_INNER // SSM_HEAD_DIM
SSM_GROUPS = 8
SSM_STATE = 128
SSM_BC_DIM = SSM_GROUPS * SSM_STATE
SSM_CONV_DIM = SSM_INNER + 2 * SSM_BC_DIM
MLP_HIDDEN = 4 * D_MODEL
CHUNK = 64
ROPE_THETA = 10000.0
NORM_EPS = 1e-6

LANES = 128
N_PROMPT_TOK = BATCH * SEQ
N_SAMPLE_TOK = DEC_BATCH * DEC_SEQ
N_TOK = N_PROMPT_TOK + N_SAMPLE_TOK
N_COND = 8
TOKEN_TILE = 1024
MLP_TILE = 512
N_PROMPT_TILES = N_PROMPT_TOK // TOKEN_TILE
TILES_PER_DEC_SEQ = DEC_SEQ // TOKEN_TILE
COL_CHUNK = 512
NEG_BIG = -1e30
VMEM_LIMIT = 56 * 1024 * 1024


def _cparams(n_grid):
    return pltpu.CompilerParams(dimension_semantics=("arbitrary",) * n_grid,
                                vmem_limit_bytes=VMEM_LIMIT)


def _silu(x):
    return x / (1.0 + jnp.exp(-x))


def _sigmoid(x):
    return 1.0 / (1.0 + jnp.exp(-x))


def _softplus(x):
    return jnp.maximum(x, 0.0) + jnp.log1p(jnp.exp(-jnp.abs(x)))


def _dot(a, b):
    return jnp.dot(a, b, preferred_element_type=F32)


def _dot_nt(a, b):
    return lax.dot_general(a, b, (((1,), (1,)), ((), ())), preferred_element_type=F32)


def _dot_tn(a, b):
    return lax.dot_general(a, b, (((0,), (0,)), ((), ())), preferred_element_type=F32)


def _split3(a):
    hi = a.astype(BF16)
    r = a - hi.astype(F32)
    mid = r.astype(BF16)
    lo = (r - mid.astype(F32)).astype(BF16)
    return hi, mid, lo


def _dot_xl(a, b_exact):
    hi, mid, lo = _split3(a)
    return _dot(hi, b_exact) + _dot(mid, b_exact) + _dot(lo, b_exact)


def _dot_xr(a_exact, b):
    hi, mid, lo = _split3(b)
    return _dot(a_exact, hi) + _dot(a_exact, mid) + _dot(a_exact, lo)


def _mod_norm(x, gain, shift, scale):
    y = x * lax.rsqrt(jnp.mean(x * x, axis=-1, keepdims=True) + NORM_EPS) * gain
    return y * (1.0 + scale) + shift


def _mod_row(i):
    return jnp.where(i < N_PROMPT_TILES, 0, 1 + (i - N_PROMPT_TILES) // TILES_PER_DEC_SEQ)


def _shr(i, k):
    return lax.shift_right_logical(i, jnp.int32(k))


def _rep_rows(x, k):
    return jnp.concatenate([x] * k, axis=0)


def _rep_lanes(x, k):
    return jnp.concatenate([x] * k, axis=1)


def _cast_kernel(x_ref, o_ref):
    o_ref[...] = x_ref[...].astype(o_ref.dtype)


def _cast_bf16(w):
    n_l, n_r, n_c = w.shape
    rows = 256
    return pl.pallas_call(
        _cast_kernel,
        grid=(n_l, n_r // rows),
        in_specs=[pl.BlockSpec((None, rows, n_c), lambda l, r: (l, r, 0))],
        out_specs=pl.BlockSpec((None, rows, n_c), lambda l, r: (l, r, 0)),
        out_shape=jax.ShapeDtypeStruct(w.shape, BF16),
        compiler_params=_cparams(2),
        name="weight_cast",
    )(w)


def _mod_kernel(c_ref, w_ref, b_ref, o_ref):
    s = _silu(c_ref[...]).astype(BF16)
    o_ref[...] = _dot(s, w_ref[...].astype(BF16)) + b_ref[...]


def _modulation(cond, w_mod, b_mod):
    n_col = 6 * D_MODEL // D_MODEL
    return pl.pallas_call(
        _mod_kernel,
        grid=(DEPTH, n_col),
        in_specs=[
            pl.BlockSpec((N_COND, D_MODEL), lambda l, j: (0, 0)),
            pl.BlockSpec((None, D_MODEL, D_MODEL), lambda l, j: (l, 0, j)),
            pl.BlockSpec((None, 1, D_MODEL), lambda l, j: (l, 0, j)),
        ],
        out_specs=pl.BlockSpec((None, N_COND, D_MODEL), lambda l, j: (l, 0, j)),
        out_shape=jax.ShapeDtypeStruct((DEPTH, N_COND, 6 * D_MODEL), F32),
        compiler_params=_cparams(2),
        name="modulation",
    )(cond, w_mod, b_mod.reshape(DEPTH, 1, 6 * D_MODEL))


RESIDENT = dict(pipeline_mode=pl.Buffered(1))


def _stream_specs(x, tile):
    if not isinstance(x, tuple):
        return [x], [pl.BlockSpec((tile, x.shape[1]), lambda i: (i, 0))]
    n_ctx = x[0].shape[0] // tile
    w = x[0].shape[1]
    return list(x), [pl.BlockSpec((tile, w), lambda i: (jnp.minimum(i, n_ctx - 1), 0)),
                     pl.BlockSpec((tile, w), lambda i: (jnp.maximum(i - n_ctx, 0), 0))]


def _stream_value(refs, is_ctx):
    if len(refs) == 1:
        return refs[0][...]
    return jnp.where(is_ctx, refs[0][...], refs[1][...])


def _inproj_kernel(*refs, splits, x_parts, n_ctx_tiles):
    x_refs = refs[:x_parts]
    gain_ref, mod_ref, w_ref, wt_ref = refs[x_parts:x_parts + 4]
    out_refs = refs[x_parts + 4:]
    m = mod_ref[...]
    x = _stream_value(x_refs, pl.program_id(0) < n_ctx_tiles)
    h = _mod_norm(x, gain_ref[...], m[:, 0:D_MODEL], m[:, D_MODEL:2 * D_MODEL]).astype(BF16)
    for o_ref, (a, b) in zip(out_refs[:-1], splits):
        for c0 in range(a, b, COL_CHUNK):
            c1 = min(c0 + COL_CHUNK, b)
            o_ref[:, c0 - a:c1 - a] = _dot(h, w_ref[:, c0:c1]).astype(o_ref.dtype)
    out_refs[-1][...] = _dot(h, wt_ref[...])


def _inproj(x, gain, mod_l, w_all, layer, w_tail, splits, dtypes, name):
    n_c = w_all.shape[2]
    tok = lambda i: (i, 0)
    const = lambda i: (0, 0)
    widths = [b - a for a, b in splits] + [LANES]
    x_args, x_specs = _stream_specs(x, TOKEN_TILE)
    return pl.pallas_call(
        functools.partial(_inproj_kernel, splits=splits, x_parts=len(x_args), n_ctx_tiles=N_PROMPT_TILES),
        grid=(N_TOK // TOKEN_TILE,),
        in_specs=x_specs + [
            pl.BlockSpec((1, D_MODEL), const),
            pl.BlockSpec((None, 1, 6 * D_MODEL), lambda i: (_mod_row(i), 0, 0)),
            pl.BlockSpec((None, D_MODEL, n_c), lambda i: (layer, 0, 0), **RESIDENT),
            pl.BlockSpec((D_MODEL, LANES), const),
        ],
        out_specs=[pl.BlockSpec((TOKEN_TILE, w), tok) for w in widths],
        out_shape=[jax.ShapeDtypeStruct((N_TOK, w), dt) for w, dt in zip(widths, tuple(dtypes) + (F32,))],
        compiler_params=_cparams(1),
        name=name,
    )(*x_args, gain.reshape(1, D_MODEL), mod_l, w_all, w_tail)


def _outproj_mlp_kernel(*refs, k_sizes, x_parts, n_ctx_tiles):
    x_refs = refs[:x_parts]
    y_refs = refs[x_parts:x_parts + 2 * len(k_sizes)]
    wo_ref, gain_ref, mod_ref, w1_ref, w2_ref, o_ref = refs[x_parts + 2 * len(k_sizes):]
    m = mod_ref[...]
    g1 = m[:, 2 * D_MODEL:3 * D_MODEL]
    sh2 = m[:, 3 * D_MODEL:4 * D_MODEL]
    sc2 = m[:, 4 * D_MODEL:5 * D_MODEL]
    g2 = m[:, 5 * D_MODEL:6 * D_MODEL]
    is_ctx = pl.program_id(0) < n_ctx_tiles
    proj = None
    off = 0
    for idx, k in enumerate(k_sizes):
        y = _stream_value(y_refs[2 * idx:2 * idx + 2], is_ctx)
        part = _dot(y, wo_ref[off:off + k, :])
        proj = part if proj is None else proj + part
        off += k
    x1 = _stream_value(x_refs, is_ctx) + g1 * proj
    h = _mod_norm(x1, gain_ref[...], sh2, sc2).astype(BF16)
    acc = jnp.zeros(x1.shape, F32)
    for c0 in range(0, MLP_HIDDEN, COL_CHUNK):
        a = jnp.maximum(_dot(h, w1_ref[:, c0:c0 + COL_CHUNK]), 0.0)
        acc = acc + _dot((a * a).astype(BF16), w2_ref[c0:c0 + COL_CHUNK, :])
    o_ref[...] = x1 + g2 * acc


def _outproj_mlp(x, ys, w_out_all, j, gain, mod_l, w1_all, w2_all, layer, name):
    k_sizes = tuple(yp.shape[1] for yp, _ in ys)
    k_in = sum(k_sizes)
    tile = MLP_TILE
    per_row = TOKEN_TILE // tile
    n_ctx_tiles = N_PROMPT_TOK // tile
    tok = lambda i: (i, 0)
    const = lambda i: (0, 0)
    x_args, x_specs = _stream_specs(x, tile)
    y_args, y_specs = [], []
    for pair in ys:
        a, s = _stream_specs(pair, tile)
        y_args += a
        y_specs += s
    return pl.pallas_call(
        functools.partial(_outproj_mlp_kernel, k_sizes=k_sizes, x_parts=len(x_args), n_ctx_tiles=n_ctx_tiles),
        grid=(N_TOK // tile,),
        in_specs=x_specs + y_specs + [
            pl.BlockSpec((None, k_in, D_MODEL), lambda i: (j, 0, 0), **RESIDENT),
            pl.BlockSpec((1, D_MODEL), const),
            pl.BlockSpec((None, 1, 6 * D_MODEL), lambda i: (_mod_row(i // per_row), 0, 0)),
            pl.BlockSpec((None, D_MODEL, MLP_HIDDEN), lambda i: (layer, 0, 0), **RESIDENT),
            pl.BlockSpec((None, MLP_HIDDEN, D_MODEL), lambda i: (layer, 0, 0), **RESIDENT),
        ],
        out_specs=pl.BlockSpec((tile, D_MODEL), tok),
        out_shape=jax.ShapeDtypeStruct((N_TOK, D_MODEL), F32),
        compiler_params=_cparams(1),
        name=name,
    )(*x_args, *y_args, w_out_all, gain.reshape(1, D_MODEL), mod_l, w1_all, w2_all)


def _final_norm_kernel(x_ref, g_ref, o_ref):
    x = x_ref[...]
    o_ref[...] = x * lax.rsqrt(jnp.mean(x * x, axis=-1, keepdims=True) + NORM_EPS) * g_ref[...]


def _final_norm(x, gain, tok0, n_tok):
    t0 = tok0 // TOKEN_TILE
    return pl.pallas_call(
        _final_norm_kernel,
        grid=(n_tok // TOKEN_TILE,),
        in_specs=[pl.BlockSpec((TOKEN_TILE, D_MODEL), lambda i: (t0 + i, 0)),
                  pl.BlockSpec((1, D_MODEL), lambda i: (0, 0))],
        out_specs=pl.BlockSpec((TOKEN_TILE, D_MODEL), lambda i: (i, 0)),
        out_shape=jax.ShapeDtypeStruct((n_tok, D_MODEL), F32),
        compiler_params=_cparams(1),
        name="final_norm",
    )(x, gain.reshape(1, D_MODEL))


def _rope_arrays(n_tok):
    axis_dim = ATT_HEAD_DIM // 2
    quarter = ATT_HEAD_DIM // 4
    inv_freq = ROPE_THETA ** (-jnp.arange(0, axis_dim, 2, dtype=F32) / axis_dim)
    t = jnp.arange(n_tok)
    row = (t // GRID_W).astype(F32)
    col = (t % GRID_W).astype(F32)
    d = np.arange(LANES) % ATT_HEAD_DIM
    part = d // axis_dim
    within = d % axis_dim
    freq = inv_freq[jnp.asarray(within % quarter)]
    pos = jnp.where(jnp.asarray(part)[None, :] == 0, row[:, None], col[:, None])
    ang = pos * freq[None, :]
    cos = jnp.cos(ang)
    sin = jnp.sin(ang)
    fh = jnp.asarray(within < quarter)[None, :]
    return cos, jnp.where(fh, -sin, 0.0), jnp.where(fh, 0.0, sin)


def _apply_rope(x, cos, sin_a, sin_b):
    quarter = ATT_HEAD_DIM // 4
    return (x * cos + pltpu.roll(x, LANES - quarter, axis=1) * sin_a
            + pltpu.roll(x, quarter, axis=1) * sin_b)


def _head_rms(x, gmat, gain):
    ss = _dot_xl(x * x, gmat)
    return x * lax.rsqrt(ss * (1.0 / ATT_HEAD_DIM) + NORM_EPS) * gain


def _attn_kernel(*refs, n, n_ctx, rope, emit_k, tq):
    it = iter(refs)
    q_ref, k_ref, v_ref = next(it), next(it), next(it)
    qg_ref, kg_ref, gmat_ref = next(it), next(it), next(it)
    if n_ctx:
        ck_ref, cv_ref = next(it), next(it)
    if rope:
        cos_ref, sa_ref, sb_ref = next(it), next(it), next(it)
    o_ref = next(it)
    if emit_k:
        ko_ref = next(it)
    keys_ref, vals_ref = next(it), next(it)

    gmat = gmat_ref[...]
    kn = _head_rms(k_ref[...], gmat, kg_ref[...])
    if emit_k:
        ko_ref[...] = kn
    if rope:
        kn = _apply_rope(kn, cos_ref[...], sa_ref[...], sb_ref[...])
    keys_ref[0:n, :] = kn.astype(BF16)
    vals_ref[0:n, :] = v_ref[...].astype(BF16)
    if n_ctx:
        keys_ref[n:n + n_ctx, :] = ck_ref[...].astype(BF16)
        vals_ref[n:n + n_ctx, :] = cv_ref[...].astype(BF16)
    low = lax.broadcasted_iota(jnp.int32, (1, LANES), 1) < ATT_HEAD_DIM
    scale = ATT_HEAD_DIM ** -0.5
    heads_per_tile = LANES // ATT_HEAD_DIM
    tiles_per_kv = (ATT_HEADS // ATT_KV_HEADS) // heads_per_tile

    n_tiles = ATT_Q_DIM // LANES

    def q_tile(qi, carry):
        rows = pl.ds(pl.multiple_of(qi * tq, tq), tq)
        qm = []
        for t in range(n_tiles):
            qn = _head_rms(q_ref[rows, t * LANES:(t + 1) * LANES], gmat, qg_ref[...])
            if rope:
                qn = _apply_rope(qn, cos_ref[rows, :], sa_ref[rows, :], sb_ref[rows, :])
            qn = qn * scale
            qs = pltpu.roll(qn, ATT_HEAD_DIM, axis=1)
            if t // tiles_per_kv == 0:
                qm += [jnp.where(low, qn, 0.0), jnp.where(low, qs, 0.0)]
            else:
                qm += [jnp.where(low, 0.0, qs), jnp.where(low, 0.0, qn)]
        keys = keys_ref[...]
        vals = vals_ref[...]
        scores = [_dot_nt(x.astype(BF16), keys) for x in qm]
        probs, inv_l = [], []
        for s in scores:
            p = jnp.exp(s - jnp.max(s, axis=-1, keepdims=True))
            inv_l.append(1.0 / jnp.sum(p, axis=-1, keepdims=True))
            probs.append(p.astype(BF16))
        res = [_dot(p, vals) * il for p, il in zip(probs, inv_l)]
        for t in range(n_tiles):
            r_e, r_o = res[2 * t], res[2 * t + 1]
            if t // tiles_per_kv == 0:
                o = jnp.where(low, r_e, pltpu.roll(r_o, ATT_HEAD_DIM, axis=1))
            else:
                o = jnp.where(low, pltpu.roll(r_e, ATT_HEAD_DIM, axis=1), r_o)
            o_ref[rows, t * LANES:(t + 1) * LANES] = o.astype(o_ref.dtype)
        return carry

    lax.fori_loop(0, n // tq, q_tile, 0)


def _attention(q, k, v, q_gain, k_gain, *, n_batch, n, tok0, ctx=None, emit_k=False):
    rope = ctx is not None
    n_ctx = ctx[0].shape[1] if rope else 0
    b0 = tok0 // n
    tq = 128 if rope else n
    tokb = lambda b: (b0 + b, 0)
    const = lambda b: (0, 0)
    heads_per_tile = LANES // ATT_HEAD_DIM
    gmat = jnp.asarray(np.kron(np.eye(heads_per_tile), np.ones((ATT_HEAD_DIM, ATT_HEAD_DIM))), BF16)
    args = [q, k, v, jnp.tile(q_gain, heads_per_tile).reshape(1, LANES),
            jnp.tile(k_gain, heads_per_tile).reshape(1, LANES), gmat]
    in_specs = [
        pl.BlockSpec((n, ATT_Q_DIM), tokb),
        pl.BlockSpec((n, ATT_KV_DIM), tokb),
        pl.BlockSpec((n, ATT_KV_DIM), tokb),
        pl.BlockSpec((1, LANES), const),
        pl.BlockSpec((1, LANES), const),
        pl.BlockSpec((LANES, LANES), const),
    ]
    if rope:
        args += [ctx[0], ctx[1]]
        in_specs += [pl.BlockSpec((None, n_ctx, ATT_KV_DIM), lambda b: (b, 0, 0))] * 2
        args += list(_rope_arrays(n))
        in_specs += [pl.BlockSpec((n, LANES), const)] * 3
    out_shape = [jax.ShapeDtypeStruct((n_batch * n, ATT_Q_DIM), BF16)]
    out_specs = [pl.BlockSpec((n, ATT_Q_DIM), lambda b: (b, 0))]
    if emit_k:
        out_shape.append(jax.ShapeDtypeStruct((n_batch * n, ATT_KV_DIM), F32))
        out_specs.append(pl.BlockSpec((n, ATT_KV_DIM), lambda b: (b, 0)))
    return pl.pallas_call(
        functools.partial(_attn_kernel, n=n, n_ctx=n_ctx, rope=rope, emit_k=emit_k, tq=tq),
        grid=(n_batch,),
        in_specs=in_specs,
        out_specs=out_specs,
        out_shape=out_shape,
        scratch_shapes=[pltpu.VMEM((n + n_ctx, ATT_KV_DIM), BF16),
                        pltpu.VMEM((n + n_ctx, ATT_KV_DIM), BF16)],
        compiler_params=_cparams(1),
        name="attention_latent" if rope else "attention_context",
    )(*args)


def _conv_silu(x, w, bias, n):
    row = lax.broadcasted_iota(jnp.int32, x.shape, 0)
    prev = jnp.where(row == 0, 0.0, pltpu.roll(x, 1, axis=0))
    nxt = jnp.where(row == n - 1, 0.0, pltpu.roll(x, n - 1, axis=0))
    y = prev * w[0:1, :] + x * w[1:2, :] + nxt * w[2:3, :]
    if bias is not None:
        y = y + bias
    return _silu(y)


DN_CAT = DN_HEADS * CHUNK
DN_EXPAND_W = DN_CAT + DN_QK_DIM
DN_ROWS_PER_STEP = 1024
DN_MIN_SEQ_PER_STEP = 2


def _to_stack(x):
    return jnp.concatenate([x[:, h * LANES:(h + 1) * LANES] for h in range(DN_HEADS)], axis=0)


def _delta_kernel(*refs, n, n_seq, has_s0, emit_s, n_prev):
    it = iter(refs)
    dqkv_ref, dz_ref, small_ref, convw_ref, pcol_ref, gain_ref, expand_ref, cum_ref = (next(it) for _ in range(8))
    if has_s0:
        s0_ref = next(it)
    if n_prev:
        prev_ref = next(it)
    o_ref = next(it)
    if emit_s:
        sfin_ref = next(it)
    q_scr, k_scr, v_scr, g_scr, b_scr, of_scr, ob_scr, s_scr = (next(it) for _ in range(8))

    n_chunks = n // CHUNK
    n_hd = 2 * DN_HEADS
    for s in range(n_seq):
        seq = slice(s * n, (s + 1) * n)
        for h in range(DN_HEADS):
            for part, scr in ((0, q_scr), (1, k_scr), (2, v_scr)):
                c0 = part * DN_QK_DIM + h * DN_KEY_DIM
                x = _conv_silu(dqkv_ref[seq, c0:c0 + LANES].astype(F32), convw_ref[:, c0:c0 + LANES], None, n)
                if part < 2:
                    x = x * lax.rsqrt(jnp.sum(x * x, axis=-1, keepdims=True) + NORM_EPS)
                if part == 0:
                    x = x * (DN_KEY_DIM ** -0.5)
                scr[seq, h * LANES:(h + 1) * LANES] = x
    small = small_ref[...]
    pcol = pcol_ref[...]
    lane = lax.broadcasted_iota(jnp.int32, (1, LANES), 1)
    b_scr[...] = jnp.where(lane < n_hd, _sigmoid(small), 0.0)
    g_scr[...] = jnp.where(lane < n_hd, 0.0, jnp.where(
        lane < 2 * n_hd, -jnp.exp(pcol[0:1, :]) * _softplus(small + pcol[1:2, :]), 0.0))
    if has_s0:
        s_scr[...] = s0_ref[...]
    else:
        s_scr[...] = jnp.zeros(s_scr.shape, F32)

    rc = lax.broadcasted_iota(jnp.int32, (CHUNK, DN_CAT), 0)
    cc = lax.broadcasted_iota(jnp.int32, (CHUNK, DN_CAT), 1) & (CHUNK - 1)
    eye_cat = jnp.where(rc == cc, 1.0, 0.0)
    r4 = lax.broadcasted_iota(jnp.int32, (DN_CAT, DN_CAT), 0)
    c4 = lax.broadcasted_iota(jnp.int32, (DN_CAT, DN_CAT), 1)
    blk_sq = jnp.where(_shr(r4, 6) == _shr(c4, 6), 1.0, 0.0).astype(BF16)
    r5 = lax.broadcasted_iota(jnp.int32, (DN_CAT, DN_QK_DIM), 0)
    c5 = lax.broadcasted_iota(jnp.int32, (DN_CAT, DN_QK_DIM), 1)
    blk_wide = jnp.where(_shr(r5, 6) == _shr(c5, 7), 1.0, 0.0).astype(BF16)
    n_levels = int(math.log2(CHUNK))
    incl, strict_f, off_masks = [], [], []
    for d in range(2):
        upper = d == 1
        incl.append((rc <= cc) if upper else (rc >= cc))
        strict_f.append(jnp.where((rc < cc) if upper else (rc > cc), 1.0, 0.0))
        masks = []
        for lvl in range(n_levels):
            same_pair = _shr(rc, lvl + 1) == _shr(cc, lvl + 1)
            half_r, half_c = _shr(rc, lvl), _shr(cc, lvl)
            side = (half_r < half_c) if upper else (half_r > half_c)
            masks.append(jnp.where(same_pair, jnp.where(side, 1.0, 0.0), 0.0))
        off_masks.append(masks)

    def block_diag(x_cat):
        return _rep_rows(x_cat.astype(BF16), DN_HEADS) * blk_sq

    def block_wide(x_stack):
        return _rep_lanes(x_stack.astype(BF16), DN_HEADS) * blk_wide

    chains = [(s, d) for s in range(n_seq) for d in range(2)]

    def chunk_step(c):
        rows = {}
        for s, d in chains:
            chunk = c if d == 0 else n_chunks - 1 - c
            rows[s, d] = pl.ds(pl.multiple_of(s * n + chunk * CHUNK, CHUNK), CHUNK)
        gb, ex = {}, {}
        for ch in chains:
            pieces = _split3(jnp.concatenate([g_scr[rows[ch], :], b_scr[rows[ch], :]], axis=0))
            gb[ch] = _dot(cum_ref[ch[1]], jnp.concatenate(pieces, axis=0))
        for ch in chains:
            ex[ch] = _dot(jnp.concatenate(_split3(gb[ch]), axis=1), expand_ref[ch[1]])
        a, kb, k, q, egc, gc_wide, g_tot, decay, beta_wide = ({} for _ in range(9))
        for ch in chains:
            upper = ch[1] == 1
            gc_cat = ex[ch][0:CHUNK, 0:DN_CAT]
            gc_wide[ch] = ex[ch][0:CHUNK, DN_CAT:]
            beta_wide[ch] = ex[ch][CHUNK:2 * CHUNK, DN_CAT:]
            gr_cat = jnp.sum(gc_cat * eye_cat, axis=0, keepdims=True)
            decay[ch] = jnp.exp(jnp.where(incl[ch[1]], gc_cat - gr_cat, NEG_BIG))
            g_tot[ch] = gc_wide[ch][0:1, :] if upper else gc_wide[ch][CHUNK - 1:CHUNK, :]
            egc[ch] = jnp.exp(gc_wide[ch])
            q[ch] = q_scr[rows[ch], :]
            k[ch] = k_scr[rows[ch], :]
            kb[ch] = k[ch] * beta_wide[ch]
            k_bd = _rep_rows(k[ch].astype(BF16), DN_HEADS) * blk_wide
            a[ch] = _dot_nt(jnp.concatenate([kb[ch], q[ch]], axis=0).astype(BF16), k_bd)
        m, qk, p = {}, {}, {}
        for ch in chains:
            m[ch] = a[ch][0:CHUNK] * decay[ch] * strict_f[ch[1]]
            qk[ch] = a[ch][CHUNK:2 * CHUNK] * decay[ch]
            p[ch] = eye_cat - m[ch] * off_masks[ch[1]][0]
        for lvl in range(1, n_levels):
            t1 = {ch: _dot(p[ch].astype(BF16), block_diag(m[ch] * off_masks[ch[1]][lvl])) for ch in chains}
            for ch in chains:
                p[ch] = p[ch] - _dot(t1[ch].astype(BF16), block_diag(p[ch]))
        uw, s_old, wq = {}, {}, {}
        for ch in chains:
            v = v_scr[rows[ch], :]
            rhs = jnp.concatenate([_to_stack(v * beta_wide[ch]), _to_stack(kb[ch] * egc[ch])], axis=1)
            uw[ch] = _dot(block_diag(p[ch]), rhs.astype(BF16))
        for ch in chains:
            s_old[ch] = s_scr[ch[0], ch[1]]
            lhs = jnp.concatenate([block_wide(uw[ch][:, DN_VAL_DIM:]),
                                   block_wide(_to_stack(q[ch] * egc[ch]))], axis=0)
            wq[ch] = _dot(lhs, s_old[ch].astype(BF16))
        for ch in chains:
            v_new = (uw[ch][:, 0:DN_VAL_DIM] - wq[ch][0:DN_CAT]).astype(BF16)
            o = wq[ch][DN_CAT:2 * DN_CAT] + _dot(block_diag(qk[ch]), v_new)
            k_dec = _to_stack(k[ch] * jnp.exp(g_tot[ch] - gc_wide[ch]))
            dec = jnp.concatenate(
                [jnp.broadcast_to(jnp.exp(g_tot[ch][:, h * LANES:(h + 1) * LANES]), (DN_KEY_DIM, LANES))
                 for h in range(DN_HEADS)], axis=0)
            s_scr[ch[0], ch[1]] = s_old[ch] * dec + _dot_tn(block_wide(k_dec), v_new)
            o_scr = ob_scr if ch[1] == 1 else of_scr
            for h in range(DN_HEADS):
                o_scr[rows[ch], h * LANES:(h + 1) * LANES] = o[h * CHUNK:(h + 1) * CHUNK]

    def body(c, carry):
        chunk_step(c)
        return carry

    lax.fori_loop(0, n_chunks, body, 0)

    if emit_s:
        if n_prev:
            sfin_ref[:, 0:n_prev] = prev_ref[...]
        sfin_ref[:, n_prev] = s_scr[...]
    for h in range(DN_HEADS):
        cols = slice(h * LANES, (h + 1) * LANES)
        o = of_scr[:, cols] + ob_scr[:, cols]
        o = o * lax.rsqrt(jnp.mean(o * o, axis=-1, keepdims=True) + NORM_EPS) * gain_ref[...]
        o_ref[:, cols] = (o * _silu(dz_ref[:, cols].astype(F32))).astype(o_ref.dtype)


def _delta(dqkv, dz, small, conv_w, a_log, dt_bias, out_gain, *, n_batch, n, tok0, s0=None, emit_s=False,
           prev=None):
    n_seq = max(DN_MIN_SEQ_PER_STEP, DN_ROWS_PER_STEP // n)
    rows = n_seq * n
    b0 = tok0 // rows
    n_hd = 2 * DN_HEADS
    tokb = lambda b: (b0 + b, 0)
    const = lambda b: (0, 0)
    const3 = lambda b: (0, 0, 0)
    pcol = jnp.zeros((2, LANES), F32)
    pcol = pcol.at[0, n_hd:2 * n_hd].set(a_log.reshape(-1)).at[1, n_hd:2 * n_hd].set(dt_bias.reshape(-1))
    expand = np.zeros((2, LANES, DN_EXPAND_W), np.float32)
    for d in range(2):
        for h in range(DN_HEADS):
            for src in (d * DN_HEADS + h, n_hd + d * DN_HEADS + h):
                expand[d, src, h * CHUNK:(h + 1) * CHUNK] = 1.0
                expand[d, src, DN_CAT + h * LANES:DN_CAT + (h + 1) * LANES] = 1.0
    expand = np.tile(expand, (1, 3, 1))
    tri = np.tril(np.ones((CHUNK, CHUNK), np.float32))
    eye = np.eye(CHUNK, dtype=np.float32)
    zero = np.zeros((CHUNK, CHUNK), np.float32)
    cum = np.stack([np.block([[t, zero] * 3, [zero, eye] * 3]) for t in (tri, tri.T)])
    args = [dqkv, dz, small, conv_w, pcol, out_gain.reshape(1, LANES), jnp.asarray(expand, BF16),
            jnp.asarray(cum, BF16)]
    in_specs = [
        pl.BlockSpec((rows, DN_CONV_DIM), tokb),
        pl.BlockSpec((rows, DN_V_DIM), tokb),
        pl.BlockSpec((rows, LANES), tokb),
        pl.BlockSpec((3, DN_CONV_DIM), const),
        pl.BlockSpec((2, LANES), const),
        pl.BlockSpec((1, LANES), const),
        pl.BlockSpec((2, 3 * LANES, DN_EXPAND_W), const3),
        pl.BlockSpec((2, 2 * CHUNK, 6 * CHUNK), const3),
    ]
    state_shape = (2, DN_HEADS * DN_KEY_DIM, DN_VAL_DIM)
    state_spec = pl.BlockSpec((n_seq,) + state_shape, lambda b: (b, 0, 0, 0))
    if s0 is not None:
        args.append(s0.reshape((n_batch,) + state_shape))
        in_specs.append(state_spec)
    n_prev = 0 if prev is None else prev.shape[1]
    stacked_spec = lambda k: pl.BlockSpec((n_seq, k) + state_shape, lambda b: (b, 0, 0, 0, 0))
    if n_prev:
        args.append(prev)
        in_specs.append(stacked_spec(n_prev))
    out_shape = [jax.ShapeDtypeStruct((n_batch * n, DN_V_DIM), BF16)]
    out_specs = [pl.BlockSpec((rows, DN_V_DIM), lambda b: (b, 0))]
    if emit_s:
        out_shape.append(jax.ShapeDtypeStruct((n_batch, n_prev + 1) + state_shape, F32))
        out_specs.append(stacked_spec(n_prev + 1))
    return pl.pallas_call(
        functools.partial(_delta_kernel, n=n, n_seq=n_seq, has_s0=s0 is not None, emit_s=emit_s, n_prev=n_prev),
        grid=(n_batch // n_seq,),
        in_specs=in_specs,
        out_specs=out_specs,
        out_shape=out_shape,
        scratch_shapes=[
            pltpu.VMEM((rows, DN_QK_DIM), F32), pltpu.VMEM((rows, DN_QK_DIM), F32),
            pltpu.VMEM((rows, DN_V_DIM), F32),
            pltpu.VMEM((rows, LANES), F32), pltpu.VMEM((rows, LANES), F32),
            pltpu.VMEM((rows, DN_V_DIM), F32), pltpu.VMEM((rows, DN_V_DIM), F32),
            pltpu.VMEM((n_seq,) + state_shape, F32),
        ],
        compiler_params=_cparams(1),
        name="delta_latent" if s0 is not None else "delta_context",
    )(*args)


SSD_CHUNK = 256
SSM_GROUP_W = SSM_INNER // SSM_GROUPS
HEADS_PER_TILE = LANES // SSM_HEAD_DIM


def _ssd_kernel(*refs, n, has_s0, emit_s, n_prev):
    it = iter(refs)
    z_ref, xbc_ref, dtc_ref, convw_ref, convb_ref, pcol_ref = (next(it) for _ in range(6))
    dskip_ref, gain_ref = next(it), next(it)
    if has_s0:
        s0_ref = next(it)
    if n_prev:
        prev_ref = next(it)
    o_ref = next(it)
    if emit_s:
        sfin_ref = next(it)
    xs_scr, bc_scr, y_scr, dt_scr, st_scr = (next(it) for _ in range(5))

    q_len = SSD_CHUNK
    n_chunks = n // q_len
    carry_state = has_s0 or n_chunks > 1

    def conv_tile(t):
        cols = pl.ds(pl.multiple_of(t * LANES, LANES), LANES)
        return _conv_silu(xbc_ref[:, cols].astype(F32), convw_ref[:, cols], convb_ref[:, cols], n)

    def prep_x(t, carry):
        cols = pl.ds(pl.multiple_of(t * LANES, LANES), LANES)
        x = conv_tile(t)
        xs_scr[:, cols] = x
        y_scr[:, cols] = x * dskip_ref[:, cols]
        return carry

    def prep_bc(t, carry):
        cols = pl.ds(pl.multiple_of(t * LANES, LANES), LANES)
        bc_scr[:, cols] = conv_tile(t + SSM_INNER // LANES).astype(BF16)
        return carry

    lax.fori_loop(0, SSM_INNER // LANES, prep_x, 0)
    lax.fori_loop(0, 2 * SSM_BC_DIM // LANES, prep_bc, 0)
    pcol = pcol_ref[...]
    dt_scr[...] = _softplus(dtc_ref[...] + pcol[1:2, :])
    neg_a_col = -jnp.exp(pcol[0:1, :])
    for d in range(2):
        for g in range(SSM_GROUPS):
            gs = slice(g * SSM_GROUP_W, (g + 1) * SSM_GROUP_W)
            if has_s0:
                st_scr[d, :, gs] = s0_ref[d, gs, :].T
            else:
                st_scr[d, :, gs] = jnp.zeros((SSM_STATE, SSM_GROUP_W), F32)

    ri = lax.broadcasted_iota(jnp.int32, (q_len, q_len), 0)
    ci = lax.broadcasted_iota(jnp.int32, (q_len, q_len), 1)
    tri3_b = (_rep_lanes((ri >= ci).astype(BF16), 3), _rep_lanes((ri <= ci).astype(BF16), 3))
    low = lax.broadcasted_iota(jnp.int32, (1, LANES), 1) < SSM_HEAD_DIM
    heads_per_group = SSM_HEADS // SSM_GROUPS

    def decays(c, d):
        upper = d == 1
        rows = pl.ds(pl.multiple_of(c * q_len, q_len), q_len)
        dt = dt_scr[rows, :]
        acum = _dot(tri3_b[d], jnp.concatenate(_split3(dt * neg_a_col), axis=0))
        a_tot = acum[0:1, :] if upper else acum[q_len - 1:q_len, :]
        w_col = dt * jnp.exp(a_tot - acum)
        return dict(rows=rows, acum=acum, acum_r=acum.T, dt_r=dt.T, w_r=w_col.T, eac=jnp.exp(acum),
                    ea_tot=jnp.exp(a_tot))

    half = q_len // 2
    rh = lax.broadcasted_iota(jnp.int32, (half, half), 0)
    ch = lax.broadcasted_iota(jnp.int32, (half, half), 1)
    incl_half = (rh >= ch, rh <= ch)

    def decay_weights(pd, l, cb, upper):
        col = pd["acum"][:, l:l + 1]
        row = pd["acum_r"][l:l + 1, :]
        dt_row = pd["dt_r"][l:l + 1, :]
        first, second = slice(0, half), slice(half, q_len)

        def quarter(r, c, masked):
            diff = col[r] - row[:, c]
            if masked:
                diff = jnp.where(incl_half[1 if upper else 0], diff, NEG_BIG)
            return cb[r, c] * (jnp.exp(diff) * dt_row[:, c])

        zero = jnp.zeros((half, half), F32)
        if upper:
            top = [quarter(first, first, True), quarter(first, second, False)]
            bottom = [zero, quarter(second, second, True)]
        else:
            top = [quarter(first, first, True), zero]
            bottom = [quarter(second, first, False), quarter(second, second, True)]
        return jnp.concatenate([jnp.concatenate(top, axis=1), jnp.concatenate(bottom, axis=1)], axis=0)

    def chunk_step(c_fwd, c_bwd, same_chunk):
        pre = (decays(c_fwd, 0), decays(c_bwd, 1))
        for g in range(SSM_GROUPS):
            shared = None
            for d in range(2):
                pd = pre[d]
                rows = pd["rows"]
                if shared is None or not same_chunk:
                    bm_g = bc_scr[rows, g * SSM_STATE:(g + 1) * SSM_STATE]
                    cm_g = bc_scr[rows, SSM_BC_DIM + g * SSM_STATE:SSM_BC_DIM + (g + 1) * SSM_STATE]
                    shared = (cm_g, _dot_nt(cm_g, bm_g), bm_g.astype(F32).T)
                cm_g, cb, bm_t = shared
                for tt in range(heads_per_group // HEADS_PER_TILE):
                    h0 = g * heads_per_group + tt * HEADS_PER_TILE
                    ls = slice(g * SSM_GROUP_W + tt * LANES, g * SSM_GROUP_W + (tt + 1) * LANES)
                    w_parts, b_parts = [], []
                    for hh in range(HEADS_PER_TILE):
                        l = d * SSM_HEADS + h0 + hh
                        w_parts.append(decay_weights(pd, l, cb, d == 1))
                        b_parts.append(bm_t * pd["w_r"][l:l + 1, :])
                    lhs = jnp.concatenate([jnp.concatenate(w_parts, axis=1),
                                           jnp.concatenate(b_parts, axis=1)], axis=0).astype(BF16)
                    xt = xs_scr[rows, ls]
                    bd = jnp.concatenate([jnp.where(low, xt, 0.0), jnp.where(low, 0.0, xt)], axis=0)
                    res = _dot(lhs, bd.astype(BF16))
                    y_new = y_scr[rows, ls] + res[0:q_len]
                    st_inc = res[q_len:q_len + SSM_STATE]
                    if carry_state:
                        st = st_scr[d, :, ls]
                        l0 = d * SSM_HEADS + h0

                        def pick(a, l0=l0):
                            return jnp.where(low, a[:, l0:l0 + 1], a[:, l0 + 1:l0 + 2])

                        y_new = y_new + _dot(cm_g, st.astype(BF16)) * pick(pd["eac"])
                        st_inc = st * pick(pd["ea_tot"]) + st_inc
                    y_scr[rows, ls] = y_new
                    st_scr[d, :, ls] = st_inc

    if n_chunks == 1:
        chunk_step(0, 0, True)
    else:
        def body(c, carry):
            chunk_step(c, n_chunks - 1 - c, False)
            return carry

        lax.fori_loop(0, n_chunks, body, 0)

    if emit_s:
        if n_prev:
            sfin_ref[0:n_prev] = prev_ref[...]
        for d in range(2):
            for g in range(SSM_GROUPS):
                gs = slice(g * SSM_GROUP_W, (g + 1) * SSM_GROUP_W)
                sfin_ref[n_prev, d, gs, :] = st_scr[d, :, gs].T
    row_tile = 128

    def finish(r, carry):
        rows = pl.ds(pl.multiple_of(r * row_tile, row_tile), row_tile)
        y = y_scr[rows, :] * _silu(z_ref[rows, :].astype(F32))
        y = y * lax.rsqrt(jnp.mean(y * y, axis=-1, keepdims=True) + NORM_EPS) * gain_ref[...]
        o_ref[rows, :] = y.astype(o_ref.dtype)
        return carry

    lax.fori_loop(0, n // row_tile, finish, 0)


def _ssd(z, xbc, dt_raw, conv_w, conv_b, a_log, dt_bias, d_skip, out_gain, *, n_batch, n, tok0,
         s0=None, emit_s=False, prev=None):
    b0 = tok0 // n
    tokb = lambda b: (b0 + b, 0)
    const = lambda b: (0, 0)
    pcol = jnp.zeros((2, LANES), F32)
    pcol = pcol.at[0, 0:2 * SSM_HEADS].set(a_log.reshape(-1)).at[1, 0:2 * SSM_HEADS].set(dt_bias.reshape(-1))
    args = [z, xbc, dt_raw, conv_w, conv_b.reshape(1, SSM_CONV_DIM), pcol,
            jnp.repeat(d_skip, SSM_HEAD_DIM).reshape(1, SSM_INNER), out_gain.reshape(1, SSM_INNER)]
    big = dict(pipeline_mode=pl.Buffered(1)) if n > SEQ else {}
    in_specs = [
        pl.BlockSpec((n, SSM_INNER), tokb, **big),
        pl.BlockSpec((n, SSM_CONV_DIM), tokb, **big),
        pl.BlockSpec((n, LANES), tokb),
        pl.BlockSpec((3, SSM_CONV_DIM), const),
        pl.BlockSpec((1, SSM_CONV_DIM), const),
        pl.BlockSpec((2, LANES), const),
        pl.BlockSpec((1, SSM_INNER), const),
        pl.BlockSpec((1, SSM_INNER), const),
    ]
    state_spec = pl.BlockSpec((None, 2, SSM_INNER, SSM_STATE), lambda b: (b, 0, 0, 0))
    if s0 is not None:
        args.append(s0.reshape(n_batch, 2, SSM_INNER, SSM_STATE))
        in_specs.append(state_spec)
    n_prev = 0 if prev is None else prev.shape[1]
    stacked_spec = lambda k: pl.BlockSpec((None, k, 2, SSM_INNER, SSM_STATE), lambda b: (b, 0, 0, 0, 0))
    if n_prev:
        args.append(prev)
        in_specs.append(stacked_spec(n_prev))
    out_shape = [jax.ShapeDtypeStruct((n_batch * n, SSM_INNER), BF16)]
    out_specs = [pl.BlockSpec((n, SSM_INNER), lambda b: (b, 0))]
    if emit_s:
        out_shape.append(jax.ShapeDtypeStruct((n_batch, n_prev + 1, 2, SSM_INNER, SSM_STATE), F32))
        out_specs.append(stacked_spec(n_prev + 1))
    return pl.pallas_call(
        functools.partial(_ssd_kernel, n=n, has_s0=s0 is not None, emit_s=emit_s, n_prev=n_prev),
        grid=(n_batch,),
        in_specs=in_specs,
        out_specs=out_specs,
        out_shape=out_shape,
        scratch_shapes=[
            pltpu.VMEM((n, SSM_INNER), F32), pltpu.VMEM((n, 2 * SSM_BC_DIM), BF16),
            pltpu.VMEM((n, SSM_INNER), F32), pltpu.VMEM((n, LANES), F32),
            pltpu.VMEM((2, SSM_STATE, SSM_INNER), F32),
        ],
        compiler_params=_cparams(1),
        name="ssd_latent" if s0 is not None else "ssd_context",
    )(*args)


EVEN_MAIN = ATT_Q_DIM + 2 * ATT_KV_DIM + DN_CONV_DIM + DN_V_DIM
EVEN_SPLITS = ((0, ATT_Q_DIM), (ATT_Q_DIM, ATT_Q_DIM + ATT_KV_DIM),
               (ATT_Q_DIM + ATT_KV_DIM, ATT_Q_DIM + 2 * ATT_KV_DIM),
               (ATT_Q_DIM + 2 * ATT_KV_DIM, ATT_Q_DIM + 2 * ATT_KV_DIM + DN_CONV_DIM),
               (ATT_Q_DIM + 2 * ATT_KV_DIM + DN_CONV_DIM, EVEN_MAIN))
EVEN_DTYPES = (F32, F32, F32, BF16, BF16)
ODD_MAIN = SSM_INNER + SSM_CONV_DIM
ODD_SPLITS = ((0, SSM_INNER), (SSM_INNER, ODD_MAIN))
ODD_DTYPES = (BF16, BF16)


def _tail_cols(w, start):
    tail = w[:, :, start:]
    return jnp.pad(tail, ((0, 0), (0, 0), (0, LANES - tail.shape[2]))).astype(BF16)


def kernel(x_prompt, x_sample, c, cache_attn_k, cache_attn_v, state_delta, state_ssm, c_ctx, norm_mix_g, norm_mlp_g, w_mod, b_mod, w_mlp_in, w_mlp_out, w_in_even, attn_q_norm_g, attn_k_norm_g, delta_conv_w, delta_a_log, delta_dt_bias, delta_norm_g, w_out_even, w_in_odd, ssm_conv_w, ssm_conv_b, ssm_a_log, ssm_dt_bias, ssm_d, ssm_norm_g, w_out_odd, final_norm_g):
    x = (x_prompt.reshape(N_PROMPT_TOK, D_MODEL), x_sample.reshape(N_SAMPLE_TOK, D_MODEL))
    cond = jnp.zeros((N_COND, D_MODEL), F32).at[0].set(c_ctx).at[1:1 + DEC_BATCH].set(c)
    mod = _modulation(cond, w_mod, b_mod).reshape(DEPTH, N_COND, 1, 6 * D_MODEL)
    w_in_even_b = w_in_even.astype(BF16)
    w_in_odd_b = w_in_odd.astype(BF16)
    w_out_even_b, w_out_odd_b = _cast_bf16(w_out_even), _cast_bf16(w_out_odd)
    w_mlp_in_b, w_mlp_out_b = _cast_bf16(w_mlp_in), _cast_bf16(w_mlp_out)
    tail_even, tail_odd = _tail_cols(w_in_even, EVEN_MAIN), _tail_cols(w_in_odd, ODD_MAIN)

    ks, vs = [], []
    s_delta = s_ssm = None
    for layer in range(DEPTH):
        j = layer // 2
        if layer % 2 == 0:
            q, k, v, dqkv, dz, small = _inproj(x, norm_mix_g[layer], mod[layer], w_in_even_b, j, tail_even[j],
                                               EVEN_SPLITS, EVEN_DTYPES, "inproj_even")
            ctx_k = cache_attn_k[:, j].reshape(DEC_BATCH, PAST_LEN, ATT_KV_DIM)
            ctx_v = cache_attn_v[:, j].reshape(DEC_BATCH, PAST_LEN, ATT_KV_DIM)
            o_att_p, k_norm = _attention(q, k, v, attn_q_norm_g[j], attn_k_norm_g[j],
                                         n_batch=BATCH, n=SEQ, tok0=0, emit_k=True)
            (o_att_s,) = _attention(q, k, v, attn_q_norm_g[j], attn_k_norm_g[j],
                                    n_batch=DEC_BATCH, n=DEC_SEQ, tok0=N_PROMPT_TOK, ctx=(ctx_k, ctx_v))
            dn_args = (dqkv, dz, small, delta_conv_w[j], delta_a_log[j], delta_dt_bias[j], delta_norm_g[j])
            o_dn_p, s_delta = _delta(*dn_args, n_batch=BATCH, n=SEQ, tok0=0, emit_s=True, prev=s_delta)
            (o_dn_s,) = _delta(*dn_args, n_batch=DEC_BATCH, n=DEC_SEQ, tok0=N_PROMPT_TOK, s0=state_delta[:, j])
            ys = [(o_att_p, o_att_s), (o_dn_p, o_dn_s)]
            w_out_b = w_out_even_b
            ks.append(k_norm.reshape(BATCH, SEQ, ATT_KV_HEADS, ATT_HEAD_DIM))
            vs.append(v[:N_PROMPT_TOK].reshape(BATCH, SEQ, ATT_KV_HEADS, ATT_HEAD_DIM))
        else:
            z, xbc, dt_raw = _inproj(x, norm_mix_g[layer], mod[layer], w_in_odd_b, j, tail_odd[j],
                                     ODD_SPLITS, ODD_DTYPES, "inproj_odd")
            ssd_args = (z, xbc, dt_raw, ssm_conv_w[j], ssm_conv_b[j], ssm_a_log[j], ssm_dt_bias[j], ssm_d[j],
                        ssm_norm_g[j])
            y_p, s_ssm = _ssd(*ssd_args, n_batch=BATCH, n=SEQ, tok0=0, emit_s=True, prev=s_ssm)
            (y_s,) = _ssd(*ssd_args, n_batch=DEC_BATCH, n=DEC_SEQ, tok0=N_PROMPT_TOK, s0=state_ssm[:, j])
            ys = [(y_p, y_s)]
            w_out_b = w_out_odd_b
        x = _outproj_mlp(x, ys, w_out_b, j, norm_mlp_g[layer], mod[layer], w_mlp_in_b, w_mlp_out_b, layer,
                         "outproj_mlp_even" if layer % 2 == 0 else "outproj_mlp_odd")

    y_prompt = _final_norm(x, final_norm_g, 0, N_PROMPT_TOK).reshape(BATCH, SEQ, D_MODEL)
    y_sample = _final_norm(x, final_norm_g, N_PROMPT_TOK, N_SAMPLE_TOK).reshape(DEC_BATCH, DEC_SEQ, D_MODEL)
    return (y_prompt, y_sample, jnp.stack(ks, axis=1), jnp.stack(vs, axis=1),
            s_delta.reshape(BATCH, N_EVEN, 2, DN_HEADS, DN_KEY_DIM, DN_VAL_DIM),
            s_ssm.reshape(BATCH, N_ODD, 2, SSM_HEADS, SSM_HEAD_DIM, SSM_STATE))
```

```python
import functools
import math

import jax
import jax.numpy as jnp
import numpy as np
from jax import lax
from jax.experimental import pallas as pl
from jax.experimental.pallas import tpu as pltpu

F32 = jnp.float32
BF16 = jnp.bfloat16

D_MODEL = 1024
BATCH = 32
SEQ = 256
DEPTH = 4
DEC_BATCH = 4
DEC_SEQ = 1024
PAST_LEN = 512
GRID_W = 64
N_EVEN = (DEPTH + 1) // 2
N_ODD = DEPTH // 2
ATT_HEAD_DIM = 64
ATT_HEADS = 8
ATT_KV_HEADS = 2
ATT_Q_DIM = ATT_HEADS * ATT_HEAD_DIM
ATT_KV_DIM = ATT_KV_HEADS * ATT_HEAD_DIM
DN_KEY_DIM = 128
DN_VAL_DIM = 128
DN_HEADS = 4
DN_QK_DIM = DN_HEADS * DN_KEY_DIM
DN_V_DIM = DN_HEADS * DN_VAL_DIM
DN_CONV_DIM = 2 * DN_QK_DIM + DN_V_DIM
SSM_INNER = 2 * D_MODEL
SSM_HEAD_DIM = 64
SSM_HEADS = SSM_INNER // SSM_HEAD_DIM
SSM_GROUPS = 8
SSM_STATE = 128
SSM_BC_DIM = SSM_GROUPS * SSM_STATE
SSM_CONV_DIM = SSM_INNER + 2 * SSM_BC_DIM
MLP_HIDDEN = 4 * D_MODEL
CHUNK = 64
ROPE_THETA = 10000.0
NORM_EPS = 1e-6

LANES = 128
N_PROMPT_TOK = BATCH * SEQ
N_SAMPLE_TOK = DEC_BATCH * DEC_SEQ
N_TOK = N_PROMPT_TOK + N_SAMPLE_TOK
N_COND = 8
TOKEN_TILE = 1024
MLP_TILE = 512
N_PROMPT_TILES = N_PROMPT_TOK // TOKEN_TILE
TILES_PER_DEC_SEQ = DEC_SEQ // TOKEN_TILE
COL_CHUNK = 512
CONV_CHUNK = 256
NEG_BIG = -1e30
VMEM_LIMIT = 56 * 1024 * 1024


def _cparams(n_grid):
    return pltpu.CompilerParams(dimension_semantics=("arbitrary",) * n_grid,
                                vmem_limit_bytes=VMEM_LIMIT)


def _silu(x):
    return x / (1.0 + jnp.exp(-x))


def _sigmoid(x):
    return 1.0 / (1.0 + jnp.exp(-x))


def _softplus(x):
    return jnp.maximum(x, 0.0) + jnp.log1p(jnp.exp(-jnp.abs(x)))


def _dot(a, b):
    return jnp.dot(a, b, preferred_element_type=F32)


def _dot_nt(a, b):
    return lax.dot_general(a, b, (((1,), (1,)), ((), ())), preferred_element_type=F32)


def _dot_tn(a, b):
    return lax.dot_general(a, b, (((0,), (0,)), ((), ())), preferred_element_type=F32)


def _split3(a):
    hi = a.astype(BF16)
    r = a - hi.astype(F32)
    mid = r.astype(BF16)
    lo = (r - mid.astype(F32)).astype(BF16)
    return hi, mid, lo


def _dot_xl(a, b_exact):
    hi, mid, lo = _split3(a)
    return _dot(hi, b_exact) + _dot(mid, b_exact) + _dot(lo, b_exact)


def _dot_xr(a_exact, b):
    hi, mid, lo = _split3(b)
    return _dot(a_exact, hi) + _dot(a_exact, mid) + _dot(a_exact, lo)


def _mod_norm(x, gain, shift, scale):
    y = x * lax.rsqrt(jnp.mean(x * x, axis=-1, keepdims=True) + NORM_EPS) * gain
    return y * (1.0 + scale) + shift


def _mod_row(i):
    return jnp.where(i < N_PROMPT_TILES, 0, 1 + (i - N_PROMPT_TILES) // TILES_PER_DEC_SEQ)


def _shr(i, k):
    return lax.shift_right_logical(i, jnp.int32(k))


def _rep_rows(x, k):
    return jnp.concatenate([x] * k, axis=0)


def _rep_lanes(x, k):
    return jnp.concatenate([x] * k, axis=1)


def _cast_kernel(x_ref, o_ref):
    o_ref[...] = x_ref[...].astype(o_ref.dtype)


def _cast_bf16(w):
    n_l, n_r, n_c = w.shape
    rows = 512
    return pl.pallas_call(
        _cast_kernel,
        grid=(n_l, n_r // rows),
        in_specs=[pl.BlockSpec((None, rows, n_c), lambda l, r: (l, r, 0))],
        out_specs=pl.BlockSpec((None, rows, n_c), lambda l, r: (l, r, 0)),
        out_shape=jax.ShapeDtypeStruct(w.shape, BF16),
        compiler_params=_cparams(2),
        name="weight_cast",
    )(w)


def _mod_kernel(c_ref, w_ref, b_ref, o_ref):
    s = _silu(c_ref[...]).astype(BF16)
    o_ref[...] = _dot(s, w_ref[...].astype(BF16)) + b_ref[...]


def _modulation(cond, w_mod, b_mod):
    n_col = 6 * D_MODEL // D_MODEL
    return pl.pallas_call(
        _mod_kernel,
        grid=(DEPTH, n_col),
        in_specs=[
            pl.BlockSpec((N_COND, D_MODEL), lambda l, j: (0, 0)),
            pl.BlockSpec((None, D_MODEL, D_MODEL), lambda l, j: (l, 0, j)),
            pl.BlockSpec((None, 1, D_MODEL), lambda l, j: (l, 0, j)),
        ],
        out_specs=pl.BlockSpec((None, N_COND, D_MODEL), lambda l, j: (l, 0, j)),
        out_shape=jax.ShapeDtypeStruct((DEPTH, N_COND, 6 * D_MODEL), F32),
        compiler_params=_cparams(2),
        name="modulation",
    )(cond, w_mod, b_mod.reshape(DEPTH, 1, 6 * D_MODEL))


RESIDENT = dict(pipeline_mode=pl.Buffered(1))


def _stream_specs(x, tile):
    if not isinstance(x, tuple):
        return [x], [pl.BlockSpec((tile, x.shape[1]), lambda i: (i, 0))]
    n_ctx = x[0].shape[0] // tile
    w = x[0].shape[1]
    return list(x), [pl.BlockSpec((tile, w), lambda i: (jnp.minimum(i, n_ctx - 1), 0)),
                     pl.BlockSpec((tile, w), lambda i: (jnp.maximum(i - n_ctx, 0), 0))]


def _stream_value(refs, is_ctx):
    if len(refs) == 1:
        return refs[0][...]
    return jnp.where(is_ctx, refs[0][...], refs[1][...])


def _inproj_kernel(*refs, splits, x_parts, n_ctx_tiles, conv_out):
    x_refs = refs[:x_parts]
    gain_ref, mod_ref, w_ref, wt_ref, cw_ref, cb_ref = refs[x_parts:x_parts + 6]
    out_refs = refs[x_parts + 6:]
    m = mod_ref[...]
    is_ctx = pl.program_id(0) < n_ctx_tiles
    x = _stream_value(x_refs, is_ctx)
    h = _mod_norm(x, gain_ref[...], m[:, 0:D_MODEL], m[:, D_MODEL:2 * D_MODEL]).astype(BF16)
    pos = lax.broadcasted_iota(jnp.int32, (TOKEN_TILE, CONV_CHUNK), 0) & (jnp.where(is_ctx, SEQ, DEC_SEQ) - 1)
    seq_first = pos == 0
    seq_last = pos == jnp.where(is_ctx, SEQ, DEC_SEQ) - 1
    for idx, (o_ref, (a, b)) in enumerate(zip(out_refs[:-1], splits)):
        chunk = CONV_CHUNK if idx == conv_out else COL_CHUNK
        for c0 in range(a, b, chunk):
            c1 = min(c0 + chunk, b)
            y = _dot(h, w_ref[:, c0:c1])
            if idx == conv_out:
                cols = slice(c0 - a, c1 - a)
                cw = cw_ref[:, cols]
                prev = jnp.where(seq_first, 0.0, pltpu.roll(y, 1, axis=0))
                nxt = jnp.where(seq_last, 0.0, pltpu.roll(y, TOKEN_TILE - 1, axis=0))
                y = _silu(prev * cw[0:1, :] + y * cw[1:2, :] + nxt * cw[2:3, :] + cb_ref[:, cols])
            o_ref[:, c0 - a:c1 - a] = y.astype(o_ref.dtype)
    out_refs[-1][...] = _dot(h, wt_ref[...])


def _inproj(x, gain, mod_l, w_all, layer, w_tail, splits, dtypes, conv_out, conv_w, conv_b, name):
    n_c = w_all.shape[2]
    tok = lambda i: (i, 0)
    const = lambda i: (0, 0)
    widths = [b - a for a, b in splits] + [LANES]
    n_conv = widths[conv_out]
    assert n_conv % CONV_CHUNK == 0
    x_args, x_specs = _stream_specs(x, TOKEN_TILE)
    return pl.pallas_call(
        functools.partial(_inproj_kernel, splits=splits, x_parts=len(x_args), n_ctx_tiles=N_PROMPT_TILES,
                          conv_out=conv_out),
        grid=(N_TOK // TOKEN_TILE,),
        in_specs=x_specs + [
            pl.BlockSpec((1, D_MODEL), const),
            pl.BlockSpec((None, 1, 6 * D_MODEL), lambda i: (_mod_row(i), 0, 0)),
            pl.BlockSpec((None, D_MODEL, n_c), lambda i: (layer, 0, 0), **RESIDENT),
            pl.BlockSpec((D_MODEL, LANES), const),
            pl.BlockSpec((3, n_conv), const),
            pl.BlockSpec((1, n_conv), const),
        ],
        out_specs=[pl.BlockSpec((TOKEN_TILE, w), tok) for w in widths],
        out_shape=[jax.ShapeDtypeStruct((N_TOK, w), dt) for w, dt in zip(widths, tuple(dtypes) + (F32,))],
        compiler_params=_cparams(1),
        name=name,
    )(*x_args, gain.reshape(1, D_MODEL), mod_l, w_all, w_tail, conv_w, conv_b.reshape(1, n_conv))


def _outproj_mlp_kernel(*refs, k_sizes, x_parts, n_ctx_tiles):
    x_refs = refs[:x_parts]
    y_refs = refs[x_parts:x_parts + 2 * len(k_sizes)]
    wo_ref, gain_ref, mod_ref, w1_ref, w2_ref, o_ref = refs[x_parts + 2 * len(k_sizes):]
    m = mod_ref[...]
    g1 = m[:, 2 * D_MODEL:3 * D_MODEL]
    sh2 = m[:, 3 * D_MODEL:4 * D_MODEL]
    sc2 = m[:, 4 * D_MODEL:5 * D_MODEL]
    g2 = m[:, 5 * D_MODEL:6 * D_MODEL]
    is_ctx = pl.program_id(0) < n_ctx_tiles
    proj = None
    off = 0
    for idx, k in enumerate(k_sizes):
        y = _stream_value(y_refs[2 * idx:2 * idx + 2], is_ctx)
        part = _dot(y, wo_ref[off:off + k, :])
        proj = part if proj is None else proj + part
        off += k
    x1 = _stream_value(x_refs, is_ctx) + g1 * proj
    h = _mod_norm(x1, gain_ref[...], sh2, sc2).astype(BF16)
    acc = jnp.zeros(x1.shape, F32)
    for c0 in range(0, MLP_HIDDEN, COL_CHUNK):
        a = jnp.maximum(_dot(h, w1_ref[:, c0:c0 + COL_CHUNK]), 0.0)
        acc = acc + _dot((a * a).astype(BF16), w2_ref[c0:c0 + COL_CHUNK, :])
    o_ref[...] = x1 + g2 * acc


def _outproj_mlp(x, ys, w_out_all, j, gain, mod_l, w1_all, w2_all, layer, name):
    k_sizes = tuple(yp.shape[1] for yp, _ in ys)
    k_in = sum(k_sizes)
    tile = MLP_TILE
    per_row = TOKEN_TILE // tile
    n_ctx_tiles = N_PROMPT_TOK // tile
    tok = lambda i: (i, 0)
    const = lambda i: (0, 0)
    x_args, x_specs = _stream_specs(x, tile)
    y_args, y_specs = [], []
    for pair in ys:
        a, s = _stream_specs(pair, tile)
        y_args += a
        y_specs += s
    return pl.pallas_call(
        functools.partial(_outproj_mlp_kernel, k_sizes=k_sizes, x_parts=len(x_args), n_ctx_tiles=n_ctx_tiles),
        grid=(N_TOK // tile,),
        in_specs=x_specs + y_specs + [
            pl.BlockSpec((None, k_in, D_MODEL), lambda i: (j, 0, 0), **RESIDENT),
            pl.BlockSpec((1, D_MODEL), const),
            pl.BlockSpec((None, 1, 6 * D_MODEL), lambda i: (_mod_row(i // per_row), 0, 0)),
            pl.BlockSpec((None, D_MODEL, MLP_HIDDEN), lambda i: (layer, 0, 0), **RESIDENT),
            pl.BlockSpec((None, MLP_HIDDEN, D_MODEL), lambda i: (layer, 0, 0), **RESIDENT),
        ],
        out_specs=pl.BlockSpec((tile, D_MODEL), tok),
        out_shape=jax.ShapeDtypeStruct((N_TOK, D_MODEL), F32),
        compiler_params=_cparams(1),
        name=name,
    )(*x_args, *y_args, w_out_all, gain.reshape(1, D_MODEL), mod_l, w1_all, w2_all)


def _final_norm_kernel(x_ref, g_ref, o_ref):
    x = x_ref[...]
    o_ref[...] = x * lax.rsqrt(jnp.mean(x * x, axis=-1, keepdims=True) + NORM_EPS) * g_ref[...]


def _final_norm(x, gain, tok0, n_tok):
    t0 = tok0 // TOKEN_TILE
    return pl.pallas_call(
        _final_norm_kernel,
        grid=(n_tok // TOKEN_TILE,),
        in_specs=[pl.BlockSpec((TOKEN_TILE, D_MODEL), lambda i: (t0 + i, 0)),
                  pl.BlockSpec((1, D_MODEL), lambda i: (0, 0))],
        out_specs=pl.BlockSpec((TOKEN_TILE, D_MODEL), lambda i: (i, 0)),
        out_shape=jax.ShapeDtypeStruct((n_tok, D_MODEL), F32),
        compiler_params=_cparams(1),
        name="final_norm",
    )(x, gain.reshape(1, D_MODEL))


def _rope_arrays(n_tok):
    axis_dim = ATT_HEAD_DIM // 2
    quarter = ATT_HEAD_DIM // 4
    f32 = np.float32
    inv_freq = (f32(ROPE_THETA) ** (-np.arange(0, axis_dim, 2, dtype=f32) / f32(axis_dim))).astype(f32)
    t = np.arange(n_tok)
    row = (t // GRID_W).astype(f32)
    col = (t % GRID_W).astype(f32)
    d = np.arange(LANES) % ATT_HEAD_DIM
    part = d // axis_dim
    within = d % axis_dim
    freq = inv_freq[within % quarter]
    pos = np.where(part[None, :] == 0, row[:, None], col[:, None])
    ang = (pos * freq[None, :]).astype(f32)
    cos = np.cos(ang).astype(f32)
    sin = np.sin(ang).astype(f32)
    fh = (within < quarter)[None, :]
    zero = f32(0.0)
    return jnp.asarray(cos), jnp.asarray(np.where(fh, -sin, zero)), jnp.asarray(np.where(fh, zero, sin))


def _apply_rope(x, cos, sin_a, sin_b):
    quarter = ATT_HEAD_DIM // 4
    return (x * cos + pltpu.roll(x, LANES - quarter, axis=1) * sin_a
            + pltpu.roll(x, quarter, axis=1) * sin_b)


def _head_rms(x, gmat, gain):
    ss = _dot_xl(x * x, gmat)
    return x * lax.rsqrt(ss * (1.0 / ATT_HEAD_DIM) + NORM_EPS) * gain


def _attn_kernel(*refs, n, n_ctx, rope, emit_k, tq):
    it = iter(refs)
    q_ref, k_ref, v_ref = next(it), next(it), next(it)
    qg_ref, kg_ref, gmat_ref = next(it), next(it), next(it)
    if n_ctx:
        ck_ref, cv_ref = next(it), next(it)
    if rope:
        cos_ref, sa_ref, sb_ref = next(it), next(it), next(it)
    o_ref = next(it)
    if emit_k:
        ko_ref = next(it)
    keys_ref, vals_ref = next(it), next(it)

    gmat = gmat_ref[...]
    kn = _head_rms(k_ref[...], gmat, kg_ref[...])
    if emit_k:
        ko_ref[...] = kn
    if rope:
        kn = _apply_rope(kn, cos_ref[...], sa_ref[...], sb_ref[...])
    keys_ref[0:n, :] = kn.astype(BF16)
    vals_ref[0:n, :] = v_ref[...].astype(BF16)
    if n_ctx:
        keys_ref[n:n + n_ctx, :] = ck_ref[...].astype(BF16)
        vals_ref[n:n + n_ctx, :] = cv_ref[...].astype(BF16)
    low = lax.broadcasted_iota(jnp.int32, (1, LANES), 1) < ATT_HEAD_DIM
    scale = ATT_HEAD_DIM ** -0.5
    heads_per_tile = LANES // ATT_HEAD_DIM
    tiles_per_kv = (ATT_HEADS // ATT_KV_HEADS) // heads_per_tile

    n_tiles = ATT_Q_DIM // LANES

    def q_tile(qi, carry):
        rows = pl.ds(pl.multiple_of(qi * tq, tq), tq)
        qm = []
        for t in range(n_tiles):
            qn = _head_rms(q_ref[rows, t * LANES:(t + 1) * LANES], gmat, qg_ref[...])
            if rope:
                qn = _apply_rope(qn, cos_ref[rows, :], sa_ref[rows, :], sb_ref[rows, :])
            qn = qn * scale
            qs = pltpu.roll(qn, ATT_HEAD_DIM, axis=1)
            if t // tiles_per_kv == 0:
                qm += [jnp.where(low, qn, 0.0), jnp.where(low, qs, 0.0)]
            else:
                qm += [jnp.where(low, 0.0, qs), jnp.where(low, 0.0, qn)]
        keys = keys_ref[...]
        vals = vals_ref[...]
        scores = [_dot_nt(x.astype(BF16), keys) for x in qm]
        probs, inv_l = [], []
        for s in scores:
            p = jnp.exp(s - jnp.max(s, axis=-1, keepdims=True))
            inv_l.append(1.0 / jnp.sum(p, axis=-1, keepdims=True))
            probs.append(p.astype(BF16))
        res = [_dot(p, vals) * il for p, il in zip(probs, inv_l)]
        for t in range(n_tiles):
            r_e, r_o = res[2 * t], res[2 * t + 1]
            if t // tiles_per_kv == 0:
                o = jnp.where(low, r_e, pltpu.roll(r_o, ATT_HEAD_DIM, axis=1))
            else:
                o = jnp.where(low, pltpu.roll(r_e, ATT_HEAD_DIM, axis=1), r_o)
            o_ref[rows, t * LANES:(t + 1) * LANES] = o.astype(o_ref.dtype)
        return carry

    lax.fori_loop(0, n // tq, q_tile, 0)


def _attention(q, k, v, q_gain, k_gain, *, n_batch, n, tok0, ctx=None, emit_k=False):
    rope = ctx is not None
    n_ctx = ctx[0].shape[1] if rope else 0
    b0 = tok0 // n
    tq = 128 if rope else n
    tokb = lambda b: (b0 + b, 0)
    const = lambda b: (0, 0)
    heads_per_tile = LANES // ATT_HEAD_DIM
    gmat = jnp.asarray(np.kron(np.eye(heads_per_tile), np.ones((ATT_HEAD_DIM, ATT_HEAD_DIM))), BF16)
    args = [q, k, v, jnp.tile(q_gain, heads_per_tile).reshape(1, LANES),
            jnp.tile(k_gain, heads_per_tile).reshape(1, LANES), gmat]
    in_specs = [
        pl.BlockSpec((n, ATT_Q_DIM), tokb),
        pl.BlockSpec((n, ATT_KV_DIM), tokb),
        pl.BlockSpec((n, ATT_KV_DIM), tokb),
        pl.BlockSpec((1, LANES), const),
        pl.BlockSpec((1, LANES), const),
        pl.BlockSpec((LANES, LANES), const),
    ]
    if rope:
        args += [ctx[0], ctx[1]]
        in_specs += [pl.BlockSpec((None, n_ctx, ATT_KV_DIM), lambda b: (b, 0, 0))] * 2
        args += list(_rope_arrays(n))
        in_specs += [pl.BlockSpec((n, LANES), const)] * 3
    out_shape = [jax.ShapeDtypeStruct((n_batch * n, ATT_Q_DIM), BF16)]
    out_specs = [pl.BlockSpec((n, ATT_Q_DIM), lambda b: (b, 0))]
    if emit_k:
        out_shape.append(jax.ShapeDtypeStruct((n_batch * n, ATT_KV_DIM), F32))
        out_specs.append(pl.BlockSpec((n, ATT_KV_DIM), lambda b: (b, 0)))
    return pl.pallas_call(
        functools.partial(_attn_kernel, n=n, n_ctx=n_ctx, rope=rope, emit_k=emit_k, tq=tq),
        grid=(n_batch,),
        in_specs=in_specs,
        out_specs=out_specs,
        out_shape=out_shape,
        scratch_shapes=[pltpu.VMEM((n + n_ctx, ATT_KV_DIM), BF16),
                        pltpu.VMEM((n + n_ctx, ATT_KV_DIM), BF16)],
        compiler_params=_cparams(1),
        name="attention_latent" if rope else "attention_context",
    )(*args)


DN_CAT = DN_HEADS * CHUNK
DN_EXPAND_W = DN_CAT + DN_QK_DIM
DN_ROWS_PER_STEP = 1024
DN_MIN_SEQ_PER_STEP = 2


def _to_stack(x):
    return jnp.concatenate([x[:, h * LANES:(h + 1) * LANES] for h in range(DN_HEADS)], axis=0)


def _delta_kernel(*refs, n, n_seq, has_s0, emit_s, n_prev):
    it = iter(refs)
    dqkv_ref, dz_ref, small_ref, pcol_ref, gain_ref, expand_ref, cum_ref = (next(it) for _ in range(7))
    if has_s0:
        s0_ref = next(it)
    if n_prev:
        prev_ref = next(it)
    o_ref = next(it)
    if emit_s:
        sfin_ref = next(it)
    q_scr, k_scr, v_scr, g_scr, b_scr, of_scr, ob_scr, s_scr = (next(it) for _ in range(8))

    n_chunks = n // CHUNK
    n_hd = 2 * DN_HEADS
    for s in range(n_seq):
        seq = slice(s * n, (s + 1) * n)
        for h in range(DN_HEADS):
            for part, scr in ((0, q_scr), (1, k_scr), (2, v_scr)):
                c0 = part * DN_QK_DIM + h * DN_KEY_DIM
                x = dqkv_ref[seq, c0:c0 + LANES].astype(F32)
                if part < 2:
                    x = x * lax.rsqrt(jnp.sum(x * x, axis=-1, keepdims=True) + NORM_EPS)
                if part == 0:
                    x = x * (DN_KEY_DIM ** -0.5)
                scr[seq, h * LANES:(h + 1) * LANES] = x
    small = small_ref[...]
    pcol = pcol_ref[...]
    lane = lax.broadcasted_iota(jnp.int32, (1, LANES), 1)
    b_scr[...] = jnp.where(lane < n_hd, _sigmoid(small), 0.0)
    g_scr[...] = jnp.where(lane < n_hd, 0.0, jnp.where(
        lane < 2 * n_hd, -jnp.exp(pcol[0:1, :]) * _softplus(small + pcol[1:2, :]), 0.0))
    if has_s0:
        s_scr[...] = s0_ref[...]
    else:
        s_scr[...] = jnp.zeros(s_scr.shape, F32)

    rc = lax.broadcasted_iota(jnp.int32, (CHUNK, DN_CAT), 0)
    cc = lax.broadcasted_iota(jnp.int32, (CHUNK, DN_CAT), 1) & (CHUNK - 1)
    eye_cat = jnp.where(rc == cc, 1.0, 0.0)
    r4 = lax.broadcasted_iota(jnp.int32, (DN_CAT, DN_CAT), 0)
    c4 = lax.broadcasted_iota(jnp.int32, (DN_CAT, DN_CAT), 1)
    blk_sq = jnp.where(_shr(r4, 6) == _shr(c4, 6), 1.0, 0.0).astype(BF16)
    r5 = lax.broadcasted_iota(jnp.int32, (DN_CAT, DN_QK_DIM), 0)
    c5 = lax.broadcasted_iota(jnp.int32, (DN_CAT, DN_QK_DIM), 1)
    blk_wide = jnp.where(_shr(r5, 6) == _shr(c5, 7), 1.0, 0.0).astype(BF16)
    n_levels = int(math.log2(CHUNK))
    incl, strict_f, off_masks = [], [], []
    for d in range(2):
        upper = d == 1
        incl.append((rc <= cc) if upper else (rc >= cc))
        strict_f.append(jnp.where((rc < cc) if upper else (rc > cc), 1.0, 0.0))
        masks = []
        for lvl in range(n_levels):
            same_pair = _shr(rc, lvl + 1) == _shr(cc, lvl + 1)
            half_r, half_c = _shr(rc, lvl), _shr(cc, lvl)
            side = (half_r < half_c) if upper else (half_r > half_c)
            masks.append(jnp.where(same_pair, jnp.where(side, 1.0, 0.0), 0.0))
        off_masks.append(masks)

    def block_diag(x_cat):
        return _rep_rows(x_cat.astype(BF16), DN_HEADS) * blk_sq

    def block_wide(x_stack):
        return _rep_lanes(x_stack.astype(BF16), DN_HEADS) * blk_wide

    chains = [(s, d) for s in range(n_seq) for d in range(2)]

    def chunk_step(c):
        rows = {}
        for s, d in chains:
            chunk = c if d == 0 else n_chunks - 1 - c
            rows[s, d] = pl.ds(pl.multiple_of(s * n + chunk * CHUNK, CHUNK), CHUNK)
        gb, ex = {}, {}
        for ch in chains:
            pieces = _split3(jnp.concatenate([g_scr[rows[ch], :], b_scr[rows[ch], :]], axis=0))
            gb[ch] = _dot(cum_ref[ch[1]], jnp.concatenate(pieces, axis=0))
        for ch in chains:
            ex[ch] = _dot(jnp.concatenate(_split3(gb[ch]), axis=1), expand_ref[ch[1]])
        a, kb, k, q, egc, gc_wide, g_tot, decay, beta_wide = ({} for _ in range(9))
        for ch in chains:
            upper = ch[1] == 1
            gc_cat = ex[ch][0:CHUNK, 0:DN_CAT]
            gc_wide[ch] = ex[ch][0:CHUNK, DN_CAT:]
            beta_wide[ch] = ex[ch][CHUNK:2 * CHUNK, DN_CAT:]
            gr_cat = jnp.sum(gc_cat * eye_cat, axis=0, keepdims=True)
            decay[ch] = jnp.exp(jnp.where(incl[ch[1]], gc_cat - gr_cat, NEG_BIG))
            g_tot[ch] = gc_wide[ch][0:1, :] if upper else gc_wide[ch][CHUNK - 1:CHUNK, :]
            egc[ch] = jnp.exp(gc_wide[ch])
            q[ch] = q_scr[rows[ch], :]
            k[ch] = k_scr[rows[ch], :]
            kb[ch] = k[ch] * beta_wide[ch]
            k_bd = _rep_rows(k[ch].astype(BF16), DN_HEADS) * blk_wide
            a[ch] = _dot_nt(jnp.concatenate([kb[ch], q[ch]], axis=0).astype(BF16), k_bd)
        m, qk, p = {}, {}, {}
        for ch in chains:
            m[ch] = a[ch][0:CHUNK] * decay[ch] * strict_f[ch[1]]
            qk[ch] = a[ch][CHUNK:2 * CHUNK] * decay[ch]
            p[ch] = eye_cat - m[ch] * off_masks[ch[1]][0]
        for lvl in range(1, n_levels):
            t1 = {ch: _dot(p[ch].astype(BF16), block_diag(m[ch] * off_masks[ch[1]][lvl])) for ch in chains}
            for ch in chains:
                p[ch] = p[ch] - _dot(t1[ch].astype(BF16), block_diag(p[ch]))
        uw, s_old, wq = {}, {}, {}
        for ch in chains:
            v = v_scr[rows[ch], :]
            rhs = jnp.concatenate([_to_stack(v * beta_wide[ch]), _to_stack(kb[ch] * egc[ch])], axis=1)
            uw[ch] = _dot(block_diag(p[ch]), rhs.astype(BF16))
        for ch in chains:
            s_old[ch] = s_scr[ch[0], ch[1]]
            lhs = jnp.concatenate([block_wide(uw[ch][:, DN_VAL_DIM:]),
                                   block_wide(_to_stack(q[ch] * egc[ch]))], axis=0)
            wq[ch] = _dot(lhs, s_old[ch].astype(BF16))
        for ch in chains:
            v_new = (uw[ch][:, 0:DN_VAL_DIM] - wq[ch][0:DN_CAT]).astype(BF16)
            o = wq[ch][DN_CAT:2 * DN_CAT] + _dot(block_diag(qk[ch]), v_new)
            k_dec = _to_stack(k[ch] * jnp.exp(g_tot[ch] - gc_wide[ch]))
            dec = jnp.concatenate(
                [jnp.broadcast_to(jnp.exp(g_tot[ch][:, h * LANES:(h + 1) * LANES]), (DN_KEY_DIM, LANES))
                 for h in range(DN_HEADS)], axis=0)
            s_scr[ch[0], ch[1]] = s_old[ch] * dec + _dot_tn(block_wide(k_dec), v_new)
            o_scr = ob_scr if ch[1] == 1 else of_scr
            for h in range(DN_HEADS):
                o_scr[rows[ch], h * LANES:(h + 1) * LANES] = o[h * CHUNK:(h + 1) * CHUNK]

    def body(c, carry):
        chunk_step(c)
        return carry

    lax.fori_loop(0, n_chunks, body, 0)

    if emit_s:
        if n_prev:
            sfin_ref[:, 0:n_prev] = prev_ref[...]
        sfin_ref[:, n_prev] = s_scr[...]
    for h in range(DN_HEADS):
        cols = slice(h * LANES, (h + 1) * LANES)
        o = of_scr[:, cols] + ob_scr[:, cols]
        o = o * lax.rsqrt(jnp.mean(o * o, axis=-1, keepdims=True) + NORM_EPS) * gain_ref[...]
        o_ref[:, cols] = (o * _silu(dz_ref[:, cols].astype(F32))).astype(o_ref.dtype)


def _delta(dqkv, dz, small, a_log, dt_bias, out_gain, *, n_batch, n, tok0, s0=None, emit_s=False,
           prev=None):
    n_seq = max(DN_MIN_SEQ_PER_STEP, DN_ROWS_PER_STEP // n)
    rows = n_seq * n
    b0 = tok0 // rows
    n_hd = 2 * DN_HEADS
    tokb = lambda b: (b0 + b, 0)
    const = lambda b: (0, 0)
    const3 = lambda b: (0, 0, 0)
    pcol = jnp.zeros((2, LANES), F32)
    pcol = pcol.at[0, n_hd:2 * n_hd].set(a_log.reshape(-1)).at[1, n_hd:2 * n_hd].set(dt_bias.reshape(-1))
    expand = np.zeros((2, LANES, DN_EXPAND_W), np.float32)
    for d in range(2):
        for h in range(DN_HEADS):
            for src in (d * DN_HEADS + h, n_hd + d * DN_HEADS + h):
                expand[d, src, h * CHUNK:(h + 1) * CHUNK] = 1.0
                expand[d, src, DN_CAT + h * LANES:DN_CAT + (h + 1) * LANES] = 1.0
    expand = np.tile(expand, (1, 3, 1))
    tri = np.tril(np.ones((CHUNK, CHUNK), np.float32))
    eye = np.eye(CHUNK, dtype=np.float32)
    zero = np.zeros((CHUNK, CHUNK), np.float32)
    cum = np.stack([np.block([[t, zero] * 3, [zero, eye] * 3]) for t in (tri, tri.T)])
    args = [dqkv, dz, small, pcol, out_gain.reshape(1, LANES), jnp.asarray(expand, BF16),
            jnp.asarray(cum, BF16)]
    in_specs = [
        pl.BlockSpec((rows, DN_CONV_DIM), tokb),
        pl.BlockSpec((rows, DN_V_DIM), tokb),
        pl.BlockSpec((rows, LANES), tokb),
        pl.BlockSpec((2, LANES), const),
        pl.BlockSpec((1, LANES), const),
        pl.BlockSpec((2, 3 * LANES, DN_EXPAND_W), const3),
        pl.BlockSpec((2, 2 * CHUNK, 6 * CHUNK), const3),
    ]
    state_shape = (2, DN_HEADS * DN_KEY_DIM, DN_VAL_DIM)
    state_spec = pl.BlockSpec((n_seq,) + state_shape, lambda b: (b, 0, 0, 0))
    if s0 is not None:
        args.append(s0.reshape((n_batch,) + state_shape))
        in_specs.append(state_spec)
    n_prev = 0 if prev is None else prev.shape[1]
    stacked_spec = lambda k: pl.BlockSpec((n_seq, k) + state_shape, lambda b: (b, 0, 0, 0, 0))
    if n_prev:
        args.append(prev)
        in_specs.append(stacked_spec(n_prev))
    out_shape = [jax.ShapeDtypeStruct((n_batch * n, DN_V_DIM), BF16)]
    out_specs = [pl.BlockSpec((rows, DN_V_DIM), lambda b: (b, 0))]
    if emit_s:
        out_shape.append(jax.ShapeDtypeStruct((n_batch, n_prev + 1) + state_shape, F32))
        out_specs.append(stacked_spec(n_prev + 1))
    return pl.pallas_call(
        functools.partial(_delta_kernel, n=n, n_seq=n_seq, has_s0=s0 is not None, emit_s=emit_s, n_prev=n_prev),
        grid=(n_batch // n_seq,),
        in_specs=in_specs,
        out_specs=out_specs,
        out_shape=out_shape,
        scratch_shapes=[
            pltpu.VMEM((rows, DN_QK_DIM), F32), pltpu.VMEM((rows, DN_QK_DIM), F32),
            pltpu.VMEM((rows, DN_V_DIM), F32),
            pltpu.VMEM((rows, LANES), F32), pltpu.VMEM((rows, LANES), F32),
            pltpu.VMEM((rows, DN_V_DIM), F32), pltpu.VMEM((rows, DN_V_DIM), F32),
            pltpu.VMEM((n_seq,) + state_shape, F32),
        ],
        compiler_params=_cparams(1),
        name="delta_latent" if s0 is not None else "delta_context",
    )(*args)


SSD_CHUNK = 256
SSM_GROUP_W = SSM_INNER // SSM_GROUPS
HEADS_PER_TILE = LANES // SSM_HEAD_DIM


def _ssd_kernel(*refs, n, has_s0, emit_s, n_prev):
    it = iter(refs)
    z_ref, xbc_ref, dtc_ref, pcol_ref, dskip_ref, gain_ref = (next(it) for _ in range(6))
    if has_s0:
        s0_ref = next(it)
    if n_prev:
        prev_ref = next(it)
    o_ref = next(it)
    if emit_s:
        sfin_ref = next(it)
    y_scr, dt_scr, st_scr = (next(it) for _ in range(3))

    q_len = SSD_CHUNK
    n_chunks = n // q_len
    carry_state = has_s0 or n_chunks > 1
    row_tile = 128

    def clear(r, carry):
        y_scr[pl.ds(pl.multiple_of(r * row_tile, row_tile), row_tile), :] = jnp.zeros((row_tile, SSM_INNER), F32)
        return carry

    lax.fori_loop(0, n // row_tile, clear, 0)
    pcol = pcol_ref[...]
    dt_scr[...] = _softplus(dtc_ref[...] + pcol[1:2, :])
    neg_a_col = -jnp.exp(pcol[0:1, :])
    for d in range(2):
        for g in range(SSM_GROUPS):
            gs = slice(g * SSM_GROUP_W, (g + 1) * SSM_GROUP_W)
            if has_s0:
                st_scr[d, :, gs] = s0_ref[d, gs, :].T
            else:
                st_scr[d, :, gs] = jnp.zeros((SSM_STATE, SSM_GROUP_W), F32)

    ri = lax.broadcasted_iota(jnp.int32, (q_len, q_len), 0)
    ci = lax.broadcasted_iota(jnp.int32, (q_len, q_len), 1)
    tri3_b = (_rep_lanes((ri >= ci).astype(BF16), 3), _rep_lanes((ri <= ci).astype(BF16), 3))
    low = lax.broadcasted_iota(jnp.int32, (1, LANES), 1) < SSM_HEAD_DIM
    heads_per_group = SSM_HEADS // SSM_GROUPS

    def decays(c, d):
        upper = d == 1
        rows = pl.ds(pl.multiple_of(c * q_len, q_len), q_len)
        dt = dt_scr[rows, :]
        acum = _dot(tri3_b[d], jnp.concatenate(_split3(dt * neg_a_col), axis=0))
        a_tot = acum[0:1, :] if upper else acum[q_len - 1:q_len, :]
        w_col = dt * jnp.exp(a_tot - acum)
        return dict(rows=rows, acum=acum, acum_r=acum.T, dt_r=dt.T, w_r=w_col.T,
                    ea_tot=jnp.exp(a_tot))

    half = q_len // 2
    assert half == LANES
    rh =lax.broadcasted_iota(jnp.int32, (half, half), 0)
    ch = lax.broadcasted_iota(jnp.int32, (half, half), 1)
    incl_half = (rh >= ch, rh <= ch)

    def decay_weights(pd, l, cb, upper):
        col = jnp.broadcast_to(pd["acum"][:, l:l + 1], (q_len, half))
        row = pd["acum_r"][l:l + 1, :]
        dt_row = pd["dt_r"][l:l + 1, :]
        first, second = slice(0, half), slice(half, q_len)

        def quarter(r, c, masked):
            diff = col[r] - row[:, c]
            if masked:
                diff = jnp.where(incl_half[1 if upper else 0], diff, NEG_BIG)
            return cb[r, c] * (jnp.exp(diff) * dt_row[:, c])

        zero = jnp.zeros((half, half), F32)
        if upper:
            top = [quarter(first, first, True), quarter(first, second, False)]
            bottom = [zero, quarter(second, second, True)]
        else:
            top = [quarter(first, first, True), zero]
            bottom = [quarter(second, first, False), quarter(second, second, True)]
        return jnp.concatenate([jnp.concatenate(top, axis=1), jnp.concatenate(bottom, axis=1)], axis=0), col

    def chunk_step(c_fwd, c_bwd, same_chunk):
        pre = (decays(c_fwd, 0), decays(c_bwd, 1))
        for g in range(SSM_GROUPS):
            shared = None
            for d in range(2):
                pd = pre[d]
                rows = pd["rows"]
                if shared is None or not same_chunk:
                    b0 = SSM_INNER + g * SSM_STATE
                    bm_g = xbc_ref[rows, b0:b0 + SSM_STATE]
                    cm_g = xbc_ref[rows, b0 + SSM_BC_DIM:b0 + SSM_BC_DIM + SSM_STATE]
                    shared = (cm_g, _dot_nt(cm_g, bm_g), bm_g.astype(F32).T)
                cm_g, cb, bm_t = shared
                for tt in range(heads_per_group // HEADS_PER_TILE):
                    h0 = g * heads_per_group + tt * HEADS_PER_TILE
                    ls = slice(g * SSM_GROUP_W + tt * LANES, g * SSM_GROUP_W + (tt + 1) * LANES)
                    w_parts, b_parts, acum_cols = [], [], []
                    for hh in range(HEADS_PER_TILE):
                        l = d * SSM_HEADS + h0 + hh
                        w_h, acum_col = decay_weights(pd, l, cb, d == 1)
                        w_parts.append(w_h)
                        acum_cols.append(acum_col)
                        b_parts.append(bm_t * pd["w_r"][l:l + 1, :])
                    lhs = jnp.concatenate([jnp.concatenate(w_parts, axis=1),
                                           jnp.concatenate(b_parts, axis=1)], axis=0).astype(BF16)
                    xt = xbc_ref[rows, ls]
                    zero = jnp.zeros_like(xt)
                    bd = jnp.concatenate([jnp.where(low, xt, zero), jnp.where(low, zero, xt)], axis=0)
                    res = _dot(lhs, bd)
                    y_new = y_scr[rows, ls] + res[0:q_len]
                    st_inc = res[q_len:q_len + SSM_STATE]
                    if carry_state:
                        st = st_scr[d, :, ls]
                        l0 = d * SSM_HEADS + h0
                        ea_tot = pd["ea_tot"]
                        scale = jnp.exp(jnp.where(low, acum_cols[0], acum_cols[1]))
                        y_new = y_new + _dot(cm_g, st.astype(BF16)) * scale
                        st_inc = st * jnp.where(low, ea_tot[:, l0:l0 + 1], ea_tot[:, l0 + 1:l0 + 2]) + st_inc
                    y_scr[rows, ls] = y_new
                    st_scr[d, :, ls] = st_inc

    if n_chunks == 1:
        chunk_step(0, 0, True)
    else:
        def body(c, carry):
            chunk_step(c, n_chunks - 1 - c, False)
            return carry

        lax.fori_loop(0, n_chunks, body, 0)

    if emit_s:
        if n_prev:
            sfin_ref[0:n_prev] = prev_ref[...]
        for d in range(2):
            for g in range(SSM_GROUPS):
                gs = slice(g * SSM_GROUP_W, (g + 1) * SSM_GROUP_W)
                sfin_ref[n_prev, d, gs, :] = st_scr[d, :, gs].T
    def finish(r, carry):
        rows = pl.ds(pl.multiple_of(r * row_tile, row_tile), row_tile)
        y = y_scr[rows, :] + xbc_ref[rows, 0:SSM_INNER].astype(F32) * dskip_ref[...]
        y = y * _silu(z_ref[rows, :].astype(F32))
        y = y * lax.rsqrt(jnp.mean(y * y, axis=-1, keepdims=True) + NORM_EPS) * gain_ref[...]
        o_ref[rows, :] = y.astype(o_ref.dtype)
        return carry

    lax.fori_loop(0, n // row_tile, finish, 0)


def _ssd(z, xbc, dt_raw, a_log, dt_bias, d_skip, out_gain, *, n_batch, n, tok0,
         s0=None, emit_s=False, prev=None):
    b0 = tok0 // n
    tokb = lambda b: (b0 + b, 0)
    const = lambda b: (0, 0)
    pcol = jnp.zeros((2, LANES), F32)
    pcol = pcol.at[0, 0:2 * SSM_HEADS].set(a_log.reshape(-1)).at[1, 0:2 * SSM_HEADS].set(dt_bias.reshape(-1))
    args = [z, xbc, dt_raw, pcol,
            jnp.repeat(d_skip, SSM_HEAD_DIM).reshape(1, SSM_INNER), out_gain.reshape(1, SSM_INNER)]
    big = dict(pipeline_mode=pl.Buffered(1)) if n > SEQ else {}
    in_specs = [
        pl.BlockSpec((n, SSM_INNER), tokb, **big),
        pl.BlockSpec((n, SSM_CONV_DIM), tokb, **big),
        pl.BlockSpec((n, LANES), tokb),
        pl.BlockSpec((2, LANES), const),
        pl.BlockSpec((1, SSM_INNER), const),
        pl.BlockSpec((1, SSM_INNER), const),
    ]
    state_spec = pl.BlockSpec((None, 2, SSM_INNER, SSM_STATE), lambda b: (b, 0, 0, 0))
    if s0 is not None:
        args.append(s0.reshape(n_batch, 2, SSM_INNER, SSM_STATE))
        in_specs.append(state_spec)
    n_prev = 0 if prev is None else prev.shape[1]
    stacked_spec = lambda k: pl.BlockSpec((None, k, 2, SSM_INNER, SSM_STATE), lambda b: (b, 0, 0, 0, 0))
    if n_prev:
        args.append(prev)
        in_specs.append(stacked_spec(n_prev))
    out_shape = [jax.ShapeDtypeStruct((n_batch * n, SSM_INNER), BF16)]
    out_specs = [pl.BlockSpec((n, SSM_INNER), lambda b: (b, 0))]
    if emit_s:
        out_shape.append(jax.ShapeDtypeStruct((n_batch, n_prev + 1, 2, SSM_INNER, SSM_STATE), F32))
        out_specs.append(stacked_spec(n_prev + 1))
    return pl.pallas_call(
        functools.partial(_ssd_kernel, n=n, has_s0=s0 is not None, emit_s=emit_s, n_prev=n_prev),
        grid=(n_batch,),
        in_specs=in_specs,
        out_specs=out_specs,
        out_shape=out_shape,
        scratch_shapes=[
            pltpu.VMEM((n, SSM_INNER), F32), pltpu.VMEM((n, LANES), F32),
            pltpu.VMEM((2, SSM_STATE, SSM_INNER), F32),
        ],
        compiler_params=_cparams(1),
        name="ssd_latent" if s0 is not None else "ssd_context",
    )(*args)


EVEN_MAIN = ATT_Q_DIM + 2 * ATT_KV_DIM + DN_CONV_DIM + DN_V_DIM
EVEN_SPLITS = ((0, ATT_Q_DIM), (ATT_Q_DIM, ATT_Q_DIM + ATT_KV_DIM),
               (ATT_Q_DIM + ATT_KV_DIM, ATT_Q_DIM + 2 * ATT_KV_DIM),
               (ATT_Q_DIM + 2 * ATT_KV_DIM, ATT_Q_DIM + 2 * ATT_KV_DIM + DN_CONV_DIM),
               (ATT_Q_DIM + 2 * ATT_KV_DIM + DN_CONV_DIM, EVEN_MAIN))
EVEN_DTYPES = (F32, F32, F32, BF16, BF16)
EVEN_CONV_OUT = 3
ODD_MAIN = SSM_INNER + SSM_CONV_DIM
ODD_SPLITS = ((0, SSM_INNER), (SSM_INNER, ODD_MAIN))
ODD_DTYPES = (BF16, BF16)
ODD_CONV_OUT = 1


def _tail_cols(w, start):
    tail = w[:, :, start:]
    return jnp.pad(tail, ((0, 0), (0, 0), (0, LANES - tail.shape[2]))).astype(BF16)


def kernel(x_prompt, x_sample, c, cache_attn_k, cache_attn_v, state_delta, state_ssm, c_ctx, norm_mix_g, norm_mlp_g, w_mod, b_mod, w_mlp_in, w_mlp_out, w_in_even, attn_q_norm_g, attn_k_norm_g, delta_conv_w, delta_a_log, delta_dt_bias, delta_norm_g, w_out_even, w_in_odd, ssm_conv_w, ssm_conv_b, ssm_a_log, ssm_dt_bias, ssm_d, ssm_norm_g, w_out_odd, final_norm_g):
    x = (x_prompt.reshape(N_PROMPT_TOK, D_MODEL), x_sample.reshape(N_SAMPLE_TOK, D_MODEL))
    cond = jnp.zeros((N_COND, D_MODEL), F32).at[0].set(c_ctx).at[1:1 + DEC_BATCH].set(c)
    mod = _modulation(cond, w_mod, b_mod).reshape(DEPTH, N_COND, 1, 6 * D_MODEL)
    w_in_even_b = w_in_even.astype(BF16)
    w_in_odd_b = w_in_odd.astype(BF16)
    w_out_even_b, w_out_odd_b = _cast_bf16(w_out_even), _cast_bf16(w_out_odd)
    w_mlp_in_b, w_mlp_out_b = _cast_bf16(w_mlp_in), _cast_bf16(w_mlp_out)
    tail_even, tail_odd = _tail_cols(w_in_even, EVEN_MAIN), _tail_cols(w_in_odd, ODD_MAIN)

    ks, vs = [], []
    s_delta = s_ssm = None
    for layer in range(DEPTH):
        j = layer // 2
        if layer % 2 == 0:
            q, k, v, dqkv, dz, small = _inproj(x, norm_mix_g[layer], mod[layer], w_in_even_b, j, tail_even[j],
                                               EVEN_SPLITS, EVEN_DTYPES, EVEN_CONV_OUT, delta_conv_w[j],
                                               jnp.zeros((DN_CONV_DIM,), F32), "inproj_even")
            ctx_k = cache_attn_k[:, j].reshape(DEC_BATCH, PAST_LEN, ATT_KV_DIM)
            ctx_v = cache_attn_v[:, j].reshape(DEC_BATCH, PAST_LEN, ATT_KV_DIM)
            o_att_p, k_norm = _attention(q, k, v, attn_q_norm_g[j], attn_k_norm_g[j],
                                         n_batch=BATCH, n=SEQ, tok0=0, emit_k=True)
            (o_att_s,) = _attention(q, k, v, attn_q_norm_g[j], attn_k_norm_g[j],
                                    n_batch=DEC_BATCH, n=DEC_SEQ, tok0=N_PROMPT_TOK, ctx=(ctx_k, ctx_v))
            dn_args = (dqkv, dz, small, delta_a_log[j], delta_dt_bias[j], delta_norm_g[j])
            o_dn_p, s_delta = _delta(*dn_args, n_batch=BATCH, n=SEQ, tok0=0, emit_s=True, prev=s_delta)
            (o_dn_s,) = _delta(*dn_args, n_batch=DEC_BATCH, n=DEC_SEQ, tok0=N_PROMPT_TOK, s0=state_delta[:, j])
            ys = [(o_att_p, o_att_s), (o_dn_p, o_dn_s)]
            w_out_b = w_out_even_b
            ks.append(k_norm.reshape(BATCH, SEQ, ATT_KV_HEADS, ATT_HEAD_DIM))
            vs.append(v[:N_PROMPT_TOK].reshape(BATCH, SEQ, ATT_KV_HEADS, ATT_HEAD_DIM))
        else:
            z, xbc, dt_raw = _inproj(x, norm_mix_g[layer], mod[layer], w_in_odd_b, j, tail_odd[j],
                                     ODD_SPLITS, ODD_DTYPES, ODD_CONV_OUT, ssm_conv_w[j], ssm_conv_b[j],
                                     "inproj_odd")
            ssd_args = (z, xbc, dt_raw, ssm_a_log[j], ssm_dt_bias[j], ssm_d[j], ssm_norm_g[j])
            y_p, s_ssm = _ssd(*ssd_args, n_batch=BATCH, n=SEQ, tok0=0, emit_s=True, prev=s_ssm)
            (y_s,) = _ssd(*ssd_args, n_batch=DEC_BATCH, n=DEC_SEQ, tok0=N_PROMPT_TOK, s0=state_ssm[:, j])
            ys = [(y_p, y_s)]
            w_out_b = w_out_odd_b
        x = _outproj_mlp(x, ys, w_out_b, j, norm_mlp_g[layer], mod[layer], w_mlp_in_b, w_mlp_out_b, layer,
                         "outproj_mlp_even" if layer % 2 == 0 else "outproj_mlp_odd")

    y_prompt = _final_norm(x, final_norm_g, 0, N_PROMPT_TOK).reshape(BATCH, SEQ, D_MODEL)
    y_sample = _final_norm(x, final_norm_g, N_PROMPT_TOK, N_SAMPLE_TOK).reshape(DEC_BATCH, DEC_SEQ, D_MODEL)
    return (y_prompt, y_sample, jnp.stack(ks, axis=1), jnp.stack(vs, axis=1),
            s_delta.reshape(BATCH, N_EVEN, 2, DN_HEADS, DN_KEY_DIM, DN_VAL_DIM),
            s_ssm.reshape(BATCH, N_ODD, 2, SSM_HEADS, SSM_HEAD_DIM, SSM_STATE))
```

```python
import functools
import math

import jax
import jax.numpy as jnp
import numpy as np
from jax import lax
from jax.experimental import pallas as pl
from jax.experimental.pallas import tpu as pltpu

F32 = jnp.float32
BF16 = jnp.bfloat16

D_MODEL = 1024
BATCH = 32
SEQ = 256
DEPTH = 4
DEC_BATCH = 4
DEC_SEQ = 1024
PAST_LEN = 512
GRID_W = 64
N_EVEN = (DEPTH + 1) // 2
N_ODD = DEPTH // 2
ATT_HEAD_DIM = 64
ATT_HEADS = 8
ATT_KV_HEADS = 2
ATT_Q_DIM = ATT_HEADS * ATT_HEAD_DIM
ATT_KV_DIM = ATT_KV_HEADS * ATT_HEAD_DIM
DN_KEY_DIM = 128
DN_VAL_DIM = 128
DN_HEADS = 4
DN_QK_DIM = DN_HEADS * DN_KEY_DIM
DN_V_DIM = DN_HEADS * DN_VAL_DIM
DN_CONV_DIM = 2 * DN_QK_DIM + DN_V_DIM
SSM_INNER = 2 * D_MODEL
SSM_HEAD_DIM = 64
SSM_HEADS = SSM_INNER // SSM_HEAD_DIM
SSM_GROUPS = 8
SSM_STATE = 128
SSM_BC_DIM = SSM_GROUPS * SSM_STATE
SSM_CONV_DIM = SSM_INNER + 2 * SSM_BC_DIM
MLP_HIDDEN = 4 * D_MODEL
CHUNK = 64
ROPE_THETA = 10000.0
NORM_EPS = 1e-6

LANES = 128
N_PROMPT_TOK = BATCH * SEQ
N_SAMPLE_TOK = DEC_BATCH * DEC_SEQ
N_TOK = N_PROMPT_TOK + N_SAMPLE_TOK
N_COND = 8
TOKEN_TILE = 1024
MLP_TILE = 512
N_PROMPT_TILES = N_PROMPT_TOK // TOKEN_TILE
TILES_PER_DEC_SEQ = DEC_SEQ // TOKEN_TILE
COL_CHUNK = 512
CONV_CHUNK = 256
NEG_BIG = -1e30
VMEM_LIMIT = 56 * 1024 * 1024


def _cparams(n_grid):
    return pltpu.CompilerParams(dimension_semantics=("arbitrary",) * n_grid,
                                vmem_limit_bytes=VMEM_LIMIT)


def _silu(x):
    half = 0.5 * x
    return half + half * jnp.tanh(half)


def _sigmoid(x):
    return 1.0 / (1.0 + jnp.exp(-x))


def _softplus(x):
    return jnp.maximum(x, 0.0) + jnp.log1p(jnp.exp(-jnp.abs(x)))


def _dot(a, b):
    return jnp.dot(a, b, preferred_element_type=F32)


def _dot_nt(a, b):
    return lax.dot_general(a, b, (((1,), (1,)), ((), ())), preferred_element_type=F32)


def _dot_tn(a, b):
    return lax.dot_general(a, b, (((0,), (0,)), ((), ())), preferred_element_type=F32)


def _split3(a):
    hi = a.astype(BF16)
    r = a - hi.astype(F32)
    mid = r.astype(BF16)
    lo = (r - mid.astype(F32)).astype(BF16)
    return hi, mid, lo


def _dot_xl(a, b_exact):
    hi, mid, lo = _split3(a)
    return _dot(hi, b_exact) + _dot(mid, b_exact) + _dot(lo, b_exact)


def _dot_xr(a_exact, b):
    hi, mid, lo = _split3(b)
    return _dot(a_exact, hi) + _dot(a_exact, mid) + _dot(a_exact, lo)


def _mod_norm(x, gain, shift, scale):
    y = x * lax.rsqrt(jnp.mean(x * x, axis=-1, keepdims=True) + NORM_EPS) * gain
    return y * (1.0 + scale) + shift


def _mod_row(i):
    return jnp.where(i < N_PROMPT_TILES, 0, 1 + (i - N_PROMPT_TILES) // TILES_PER_DEC_SEQ)


def _shr(i, k):
    return lax.shift_right_logical(i, jnp.int32(k))


def _rep_rows(x, k):
    return jnp.concatenate([x] * k, axis=0)


def _rep_lanes(x, k):
    return jnp.concatenate([x] * k, axis=1)


def _cast_kernel(x_ref, o_ref):
    o_ref[...] = x_ref[...].astype(o_ref.dtype)


def _cast_bf16(w):
    n_l, n_r, n_c = w.shape
    rows = 512
    return pl.pallas_call(
        _cast_kernel,
        grid=(n_l, n_r // rows),
        in_specs=[pl.BlockSpec((None, rows, n_c), lambda l, r: (l, r, 0))],
        out_specs=pl.BlockSpec((None, rows, n_c), lambda l, r: (l, r, 0)),
        out_shape=jax.ShapeDtypeStruct(w.shape, BF16),
        compiler_params=_cparams(2),
        name="weight_cast",
    )(w)


def _mod_kernel(c_ref, w_ref, b_ref, o_ref):
    s = _silu(c_ref[...]).astype(BF16)
    o_ref[...] = _dot(s, w_ref[...].astype(BF16)) + b_ref[...]


def _modulation(cond, w_mod, b_mod):
    n_col = 6 * D_MODEL // D_MODEL
    return pl.pallas_call(
        _mod_kernel,
        grid=(DEPTH, n_col),
        in_specs=[
            pl.BlockSpec((N_COND, D_MODEL), lambda l, j: (0, 0)),
            pl.BlockSpec((None, D_MODEL, D_MODEL), lambda l, j: (l, 0, j)),
            pl.BlockSpec((None, 1, D_MODEL), lambda l, j: (l, 0, j)),
        ],
        out_specs=pl.BlockSpec((None, N_COND, D_MODEL), lambda l, j: (l, 0, j)),
        out_shape=jax.ShapeDtypeStruct((DEPTH, N_COND, 6 * D_MODEL), F32),
        compiler_params=_cparams(2),
        name="modulation",
    )(cond, w_mod, b_mod.reshape(DEPTH, 1, 6 * D_MODEL))


RESIDENT = dict(pipeline_mode=pl.Buffered(1))


def _stream_specs(x, tile):
    if not isinstance(x, tuple):
        return [x], [pl.BlockSpec((tile, x.shape[1]), lambda i: (i, 0))]
    n_ctx = x[0].shape[0] // tile
    w = x[0].shape[1]
    return list(x), [pl.BlockSpec((tile, w), lambda i: (jnp.minimum(i, n_ctx - 1), 0)),
                     pl.BlockSpec((tile, w), lambda i: (jnp.maximum(i - n_ctx, 0), 0))]


def _stream_value(refs, is_ctx):
    if len(refs) == 1:
        return refs[0][...]
    return jnp.where(is_ctx, refs[0][...], refs[1][...])


def _inproj_kernel(*refs, splits, x_parts, n_ctx_tiles, conv_out):
    x_refs = refs[:x_parts]
    gain_ref, mod_ref, w_ref, wt_ref, cw_ref, cb_ref = refs[x_parts:x_parts + 6]
    out_refs = refs[x_parts + 6:]
    m = mod_ref[...]
    is_ctx = pl.program_id(0) < n_ctx_tiles
    x = _stream_value(x_refs, is_ctx)
    h = _mod_norm(x, gain_ref[...], m[:, 0:D_MODEL], m[:, D_MODEL:2 * D_MODEL]).astype(BF16)
    pos = lax.broadcasted_iota(jnp.int32, (TOKEN_TILE, CONV_CHUNK), 0) & (jnp.where(is_ctx, SEQ, DEC_SEQ) - 1)
    seq_first = pos == 0
    seq_last = pos == jnp.where(is_ctx, SEQ, DEC_SEQ) - 1
    for idx, (o_ref, (a, b)) in enumerate(zip(out_refs[:-1], splits)):
        chunk = CONV_CHUNK if idx == conv_out else COL_CHUNK
        for c0 in range(a, b, chunk):
            c1 = min(c0 + chunk, b)
            y = _dot(h, w_ref[:, c0:c1])
            if idx == conv_out:
                cols = slice(c0 - a, c1 - a)
                cw = cw_ref[:, cols]
                prev = jnp.where(seq_first, 0.0, pltpu.roll(y, 1, axis=0))
                nxt = jnp.where(seq_last, 0.0, pltpu.roll(y, TOKEN_TILE - 1, axis=0))
                y = _silu(prev * cw[0:1, :] + y * cw[1:2, :] + nxt * cw[2:3, :] + cb_ref[:, cols])
            o_ref[:, c0 - a:c1 - a] = y.astype(o_ref.dtype)
    out_refs[-1][...] = _dot(h, wt_ref[...])


def _inproj(x, gain, mod_l, w_all, layer, w_tail, splits, dtypes, conv_out, conv_w, conv_b, name):
    n_c = w_all.shape[2]
    tok = lambda i: (i, 0)
    const = lambda i: (0, 0)
    widths = [b - a for a, b in splits] + [LANES]
    n_conv = widths[conv_out]
    assert n_conv % CONV_CHUNK == 0
    x_args, x_specs = _stream_specs(x, TOKEN_TILE)
    return pl.pallas_call(
        functools.partial(_inproj_kernel, splits=splits, x_parts=len(x_args), n_ctx_tiles=N_PROMPT_TILES,
                          conv_out=conv_out),
        grid=(N_TOK // TOKEN_TILE,),
        in_specs=x_specs + [
            pl.BlockSpec((1, D_MODEL), const),
            pl.BlockSpec((None, 1, 6 * D_MODEL), lambda i: (_mod_row(i), 0, 0)),
            pl.BlockSpec((None, D_MODEL, n_c), lambda i: (layer, 0, 0), **RESIDENT),
            pl.BlockSpec((D_MODEL, LANES), const),
            pl.BlockSpec((3, n_conv), const),
            pl.BlockSpec((1, n_conv), const),
        ],
        out_specs=[pl.BlockSpec((TOKEN_TILE, w), tok) for w in widths],
        out_shape=[jax.ShapeDtypeStruct((N_TOK, w), dt) for w, dt in zip(widths, tuple(dtypes) + (F32,))],
        compiler_params=_cparams(1),
        name=name,
    )(*x_args, gain.reshape(1, D_MODEL), mod_l, w_all, w_tail, conv_w, conv_b.reshape(1, n_conv))


def _outproj_mlp_kernel(*refs, k_sizes, x_parts, n_ctx_tiles):
    x_refs = refs[:x_parts]
    y_refs = refs[x_parts:x_parts + 2 * len(k_sizes)]
    wo_ref, gain_ref, mod_ref, w1_ref, w2_ref, o_ref = refs[x_parts + 2 * len(k_sizes):]
    m = mod_ref[...]
    g1 = m[:, 2 * D_MODEL:3 * D_MODEL]
    sh2 = m[:, 3 * D_MODEL:4 * D_MODEL]
    sc2 = m[:, 4 * D_MODEL:5 * D_MODEL]
    g2 = m[:, 5 * D_MODEL:6 * D_MODEL]
    is_ctx = pl.program_id(0) < n_ctx_tiles
    proj = None
    off = 0
    for idx, k in enumerate(k_sizes):
        y = _stream_value(y_refs[2 * idx:2 * idx + 2], is_ctx)
        part = _dot(y, wo_ref[off:off + k, :])
        proj = part if proj is None else proj + part
        off += k
    x1 = _stream_value(x_refs, is_ctx) + g1 * proj
    h = _mod_norm(x1, gain_ref[...], sh2, sc2).astype(BF16)
    acc = jnp.zeros(x1.shape, F32)
    for c0 in range(0, MLP_HIDDEN, COL_CHUNK):
        a = jnp.maximum(_dot(h, w1_ref[:, c0:c0 + COL_CHUNK]), 0.0)
        acc = acc + _dot((a * a).astype(BF16), w2_ref[c0:c0 + COL_CHUNK, :])
    o_ref[...] = x1 + g2 * acc


def _outproj_mlp(x, ys, w_out_all, j, gain, mod_l, w1_all, w2_all, layer, name):
    k_sizes = tuple(yp.shape[1] for yp, _ in ys)
    k_in = sum(k_sizes)
    tile = MLP_TILE
    per_row = TOKEN_TILE // tile
    n_ctx_tiles = N_PROMPT_TOK // tile
    tok = lambda i: (i, 0)
    const = lambda i: (0, 0)
    x_args, x_specs = _stream_specs(x, tile)
    y_args, y_specs = [], []
    for pair in ys:
        a, s = _stream_specs(pair, tile)
        y_args += a
        y_specs += s
    return pl.pallas_call(
        functools.partial(_outproj_mlp_kernel, k_sizes=k_sizes, x_parts=len(x_args), n_ctx_tiles=n_ctx_tiles),
        grid=(N_TOK // tile,),
        in_specs=x_specs + y_specs + [
            pl.BlockSpec((None, k_in, D_MODEL), lambda i: (j, 0, 0), **RESIDENT),
            pl.BlockSpec((1, D_MODEL), const),
            pl.BlockSpec((None, 1, 6 * D_MODEL), lambda i: (_mod_row(i // per_row), 0, 0)),
            pl.BlockSpec((None, D_MODEL, MLP_HIDDEN), lambda i: (layer, 0, 0), **RESIDENT),
            pl.BlockSpec((None, MLP_HIDDEN, D_MODEL), lambda i: (layer, 0, 0), **RESIDENT),
        ],
        out_specs=pl.BlockSpec((tile, D_MODEL), tok),
        out_shape=jax.ShapeDtypeStruct((N_TOK, D_MODEL), F32),
        compiler_params=_cparams(1),
        name=name,
    )(*x_args, *y_args, w_out_all, gain.reshape(1, D_MODEL), mod_l, w1_all, w2_all)


def _final_norm_kernel(x_ref, g_ref, o_ref):
    x = x_ref[...]
    o_ref[...] = x * lax.rsqrt(jnp.mean(x * x, axis=-1, keepdims=True) + NORM_EPS) * g_ref[...]


def _final_norm(x, gain, tok0, n_tok):
    t0 = tok0 // TOKEN_TILE
    return pl.pallas_call(
        _final_norm_kernel,
        grid=(n_tok // TOKEN_TILE,),
        in_specs=[pl.BlockSpec((TOKEN_TILE, D_MODEL), lambda i: (t0 + i, 0)),
                  pl.BlockSpec((1, D_MODEL), lambda i: (0, 0))],
        out_specs=pl.BlockSpec((TOKEN_TILE, D_MODEL), lambda i: (i, 0)),
        out_shape=jax.ShapeDtypeStruct((n_tok, D_MODEL), F32),
        compiler_params=_cparams(1),
        name="final_norm",
    )(x, gain.reshape(1, D_MODEL))


def _rope_arrays(n_tok):
    axis_dim = ATT_HEAD_DIM // 2
    quarter = ATT_HEAD_DIM // 4
    f32 = np.float32
    inv_freq = (f32(ROPE_THETA) ** (-np.arange(0, axis_dim, 2, dtype=f32) / f32(axis_dim))).astype(f32)
    t = np.arange(n_tok)
    row = (t // GRID_W).astype(f32)
    col = (t % GRID_W).astype(f32)
    d = np.arange(LANES) % ATT_HEAD_DIM
    part = d // axis_dim
    within = d % axis_dim
    freq = inv_freq[within % quarter]
    pos = np.where(part[None, :] == 0, row[:, None], col[:, None])
    ang = (pos * freq[None, :]).astype(f32)
    cos = np.cos(ang).astype(f32)
    sin = np.sin(ang).astype(f32)
    fh = (within < quarter)[None, :]
    zero = f32(0.0)
    return jnp.asarray(cos), jnp.asarray(np.where(fh, -sin, zero)), jnp.asarray(np.where(fh, zero, sin))


def _apply_rope(x, cos, sin_a, sin_b):
    quarter = ATT_HEAD_DIM // 4
    return (x * cos + pltpu.roll(x, LANES - quarter, axis=1) * sin_a
            + pltpu.roll(x, quarter, axis=1) * sin_b)


def _head_rms(x, gmat, gain):
    ss = _dot_xl(x * x, gmat)
    return x * lax.rsqrt(ss * (1.0 / ATT_HEAD_DIM) + NORM_EPS) * gain


def _attn_kernel(*refs, n, n_ctx, rope, emit_k, tq):
    it = iter(refs)
    q_ref, k_ref, v_ref = next(it), next(it), next(it)
    qg_ref, kg_ref, gmat_ref = next(it), next(it), next(it)
    if n_ctx:
        ck_ref, cv_ref = next(it), next(it)
    if rope:
        cos_ref, sa_ref, sb_ref = next(it), next(it), next(it)
    o_ref = next(it)
    if emit_k:
        ko_ref = next(it)
    keys_ref, vals_ref = next(it), next(it)

    gmat = gmat_ref[...]
    kn = _head_rms(k_ref[...], gmat, kg_ref[...])
    if emit_k:
        ko_ref[...] = kn
    if rope:
        kn = _apply_rope(kn, cos_ref[...], sa_ref[...], sb_ref[...])
    keys_ref[0:n, :] = kn.astype(BF16)
    vals_ref[0:n, :] = v_ref[...].astype(BF16)
    if n_ctx:
        keys_ref[n:n + n_ctx, :] = ck_ref[...].astype(BF16)
        vals_ref[n:n + n_ctx, :] = cv_ref[...].astype(BF16)
    low = lax.broadcasted_iota(jnp.int32, (1, LANES), 1) < ATT_HEAD_DIM
    scale = ATT_HEAD_DIM ** -0.5
    heads_per_tile = LANES // ATT_HEAD_DIM
    tiles_per_kv = (ATT_HEADS // ATT_KV_HEADS) // heads_per_tile

    n_tiles = ATT_Q_DIM // LANES

    def q_tile(qi, carry):
        rows = pl.ds(pl.multiple_of(qi * tq, tq), tq)
        qm = []
        for t in range(n_tiles):
            qn = _head_rms(q_ref[rows, t * LANES:(t + 1) * LANES], gmat, qg_ref[...])
            if rope:
                qn = _apply_rope(qn, cos_ref[rows, :], sa_ref[rows, :], sb_ref[rows, :])
            qn = qn * scale
            qs = pltpu.roll(qn, ATT_HEAD_DIM, axis=1)
            if t // tiles_per_kv == 0:
                qm += [jnp.where(low, qn, 0.0), jnp.where(low, qs, 0.0)]
            else:
                qm += [jnp.where(low, 0.0, qs), jnp.where(low, 0.0, qn)]
        keys = keys_ref[...]
        vals = vals_ref[...]
        scores = [_dot_nt(x.astype(BF16), keys) for x in qm]
        probs, inv_l = [], []
        for s in scores:
            p = jnp.exp(s - jnp.max(s, axis=-1, keepdims=True))
            inv_l.append(1.0 / jnp.sum(p, axis=-1, keepdims=True))
            probs.append(p.astype(BF16))
        res = [_dot(p, vals) * il for p, il in zip(probs, inv_l)]
        for t in range(n_tiles):
            r_e, r_o = res[2 * t], res[2 * t + 1]
            if t // tiles_per_kv == 0:
                o = jnp.where(low, r_e, pltpu.roll(r_o, ATT_HEAD_DIM, axis=1))
            else:
                o = jnp.where(low, pltpu.roll(r_e, ATT_HEAD_DIM, axis=1), r_o)
            o_ref[rows, t * LANES:(t + 1) * LANES] = o.astype(o_ref.dtype)
        return carry

    lax.fori_loop(0, n // tq, q_tile, 0)


def _attention(q, k, v, q_gain, k_gain, *, n_batch, n, tok0, ctx=None, emit_k=False):
    rope = ctx is not None
    n_ctx = ctx[0].shape[1] if rope else 0
    b0 = tok0 // n
    tq = 128 if rope else n
    tokb = lambda b: (b0 + b, 0)
    const = lambda b: (0, 0)
    heads_per_tile = LANES // ATT_HEAD_DIM
    gmat = jnp.asarray(np.kron(np.eye(heads_per_tile), np.ones((ATT_HEAD_DIM, ATT_HEAD_DIM))), BF16)
    args = [q, k, v, jnp.tile(q_gain, heads_per_tile).reshape(1, LANES),
            jnp.tile(k_gain, heads_per_tile).reshape(1, LANES), gmat]
    in_specs = [
        pl.BlockSpec((n, ATT_Q_DIM), tokb),
        pl.BlockSpec((n, ATT_KV_DIM), tokb),
        pl.BlockSpec((n, ATT_KV_DIM), tokb),
        pl.BlockSpec((1, LANES), const),
        pl.BlockSpec((1, LANES), const),
        pl.BlockSpec((LANES, LANES), const),
    ]
    if rope:
        args += [ctx[0], ctx[1]]
        in_specs += [pl.BlockSpec((None, n_ctx, ATT_KV_DIM), lambda b: (b, 0, 0))] * 2
        args += list(_rope_arrays(n))
        in_specs += [pl.BlockSpec((n, LANES), const)] * 3
    out_shape = [jax.ShapeDtypeStruct((n_batch * n, ATT_Q_DIM), BF16)]
    out_specs = [pl.BlockSpec((n, ATT_Q_DIM), lambda b: (b, 0))]
    if emit_k:
        out_shape.append(jax.ShapeDtypeStruct((n_batch * n, ATT_KV_DIM), F32))
        out_specs.append(pl.BlockSpec((n, ATT_KV_DIM), lambda b: (b, 0)))
    return pl.pallas_call(
        functools.partial(_attn_kernel, n=n, n_ctx=n_ctx, rope=rope, emit_k=emit_k, tq=tq),
        grid=(n_batch,),
        in_specs=in_specs,
        out_specs=out_specs,
        out_shape=out_shape,
        scratch_shapes=[pltpu.VMEM((n + n_ctx, ATT_KV_DIM), BF16),
                        pltpu.VMEM((n + n_ctx, ATT_KV_DIM), BF16)],
        compiler_params=_cparams(1),
        name="attention_latent" if rope else "attention_context",
    )(*args)


DN_CAT = DN_HEADS * CHUNK
DN_EXPAND_W = DN_CAT + DN_QK_DIM
DN_ROWS_PER_STEP = 1024
DN_MIN_SEQ_PER_STEP = 2


def _to_stack(x):
    return jnp.concatenate([x[:, h * LANES:(h + 1) * LANES] for h in range(DN_HEADS)], axis=0)


def _delta_kernel(*refs, n, n_seq, has_s0, emit_s, n_prev):
    it = iter(refs)
    dqkv_ref, dz_ref, small_ref, pcol_ref, gain_ref, expand_ref, cum_ref = (next(it) for _ in range(7))
    if has_s0:
        s0_ref = next(it)
    if n_prev:
        prev_ref = next(it)
    o_ref = next(it)
    if emit_s:
        sfin_ref = next(it)
    q_scr, k_scr, v_scr, g_scr, b_scr, of_scr, ob_scr, s_scr = (next(it) for _ in range(8))

    n_chunks = n // CHUNK
    n_hd = 2 * DN_HEADS
    for s in range(n_seq):
        seq = slice(s * n, (s + 1) * n)
        for h in range(DN_HEADS):
            for part, scr in ((0, q_scr), (1, k_scr), (2, v_scr)):
                c0 = part * DN_QK_DIM + h * DN_KEY_DIM
                x = dqkv_ref[seq, c0:c0 + LANES].astype(F32)
                if part < 2:
                    x = x * lax.rsqrt(jnp.sum(x * x, axis=-1, keepdims=True) + NORM_EPS)
                if part == 0:
                    x = x * (DN_KEY_DIM ** -0.5)
                scr[seq, h * LANES:(h + 1) * LANES] = x
    small = small_ref[...]
    pcol = pcol_ref[...]
    lane = lax.broadcasted_iota(jnp.int32, (1, LANES), 1)
    b_scr[...] = jnp.where(lane < n_hd, _sigmoid(small), 0.0)
    g_scr[...] = jnp.where(lane < n_hd, 0.0, jnp.where(
        lane < 2 * n_hd, -jnp.exp(pcol[0:1, :]) * _softplus(small + pcol[1:2, :]), 0.0))
    if has_s0:
        s_scr[...] = s0_ref[...]
    else:
        s_scr[...] = jnp.zeros(s_scr.shape, F32)

    rc = lax.broadcasted_iota(jnp.int32, (CHUNK, DN_CAT), 0)
    cc = lax.broadcasted_iota(jnp.int32, (CHUNK, DN_CAT), 1) & (CHUNK - 1)
    eye_cat = jnp.where(rc == cc, 1.0, 0.0)
    r4 = lax.broadcasted_iota(jnp.int32, (DN_CAT, DN_CAT), 0)
    c4 = lax.broadcasted_iota(jnp.int32, (DN_CAT, DN_CAT), 1)
    blk_sq = jnp.where(_shr(r4, 6) == _shr(c4, 6), 1.0, 0.0).astype(BF16)
    r5 = lax.broadcasted_iota(jnp.int32, (DN_CAT, DN_QK_DIM), 0)
    c5 = lax.broadcasted_iota(jnp.int32, (DN_CAT, DN_QK_DIM), 1)
    blk_wide = jnp.where(_shr(r5, 6) == _shr(c5, 7), 1.0, 0.0).astype(BF16)
    n_levels = int(math.log2(CHUNK))
    incl, strict_f, off_masks = [], [], []
    for d in range(2):
        upper = d == 1
        incl.append((rc <= cc) if upper else (rc >= cc))
        strict_f.append(jnp.where((rc < cc) if upper else (rc > cc), 1.0, 0.0))
        masks = []
        for lvl in range(n_levels):
            same_pair = _shr(rc, lvl + 1) == _shr(cc, lvl + 1)
            half_r, half_c = _shr(rc, lvl), _shr(cc, lvl)
            side = (half_r < half_c) if upper else (half_r > half_c)
            masks.append(jnp.where(same_pair, jnp.where(side, 1.0, 0.0), 0.0))
        off_masks.append(masks)

    def block_diag(x_cat):
        return _rep_rows(x_cat.astype(BF16), DN_HEADS) * blk_sq

    def block_wide(x_stack):
        return _rep_lanes(x_stack.astype(BF16), DN_HEADS) * blk_wide

    chains = [(s, d) for s in range(n_seq) for d in range(2)]

    def chunk_step(c):
        rows = {}
        for s, d in chains:
            chunk = c if d == 0 else n_chunks - 1 - c
            rows[s, d] = pl.ds(pl.multiple_of(s * n + chunk * CHUNK, CHUNK), CHUNK)
        gb, ex = {}, {}
        for ch in chains:
            pieces = _split3(jnp.concatenate([g_scr[rows[ch], :], b_scr[rows[ch], :]], axis=0))
            gb[ch] = _dot(cum_ref[ch[1]], jnp.concatenate(pieces, axis=0))
        for ch in chains:
            ex[ch] = _dot(jnp.concatenate(_split3(gb[ch]), axis=1), expand_ref[ch[1]])
        a, kb, k, q, egc, gc_wide, g_tot, decay, beta_wide = ({} for _ in range(9))
        for ch in chains:
            upper = ch[1] == 1
            gc_cat = ex[ch][0:CHUNK, 0:DN_CAT]
            gc_wide[ch] = ex[ch][0:CHUNK, DN_CAT:]
            beta_wide[ch] = ex[ch][CHUNK:2 * CHUNK, DN_CAT:]
            gr_cat = jnp.sum(gc_cat * eye_cat, axis=0, keepdims=True)
            decay[ch] = jnp.exp(jnp.where(incl[ch[1]], gc_cat - gr_cat, NEG_BIG))
            g_tot[ch] = gc_wide[ch][0:1, :] if upper else gc_wide[ch][CHUNK - 1:CHUNK, :]
            egc[ch] = jnp.exp(gc_wide[ch])
            q[ch] = q_scr[rows[ch], :]
            k[ch] = k_scr[rows[ch], :]
            kb[ch] = k[ch] * beta_wide[ch]
            k_bd = _rep_rows(k[ch].astype(BF16), DN_HEADS) * blk_wide
            a[ch] = _dot_nt(jnp.concatenate([kb[ch], q[ch]], axis=0).astype(BF16), k_bd)
        m, qk, p = {}, {}, {}
        for ch in chains:
            m[ch] = a[ch][0:CHUNK] * decay[ch] * strict_f[ch[1]]
            qk[ch] = a[ch][CHUNK:2 * CHUNK] * decay[ch]
            p[ch] = eye_cat - m[ch] * off_masks[ch[1]][0]
        for lvl in range(1, n_levels):
            t1 = {ch: _dot(p[ch].astype(BF16), block_diag(m[ch] * off_masks[ch[1]][lvl])) for ch in chains}
            for ch in chains:
                p[ch] = p[ch] - _dot(t1[ch].astype(BF16), block_diag(p[ch]))
        uw, s_old, wq = {}, {}, {}
        for ch in chains:
            v = v_scr[rows[ch], :]
            rhs = jnp.concatenate([_to_stack(v * beta_wide[ch]), _to_stack(kb[ch] * egc[ch])], axis=1)
            uw[ch] = _dot(block_diag(p[ch]), rhs.astype(BF16))
        for ch in chains:
            s_old[ch] = s_scr[ch[0], ch[1]]
            lhs = jnp.concatenate([block_wide(uw[ch][:, DN_VAL_DIM:]),
                                   block_wide(_to_stack(q[ch] * egc[ch]))], axis=0)
            wq[ch] = _dot(lhs, s_old[ch].astype(BF16))
        for ch in chains:
            v_new = (uw[ch][:, 0:DN_VAL_DIM] - wq[ch][0:DN_CAT]).astype(BF16)
            o = wq[ch][DN_CAT:2 * DN_CAT] + _dot(block_diag(qk[ch]), v_new)
            k_dec = _to_stack(k[ch] * jnp.exp(g_tot[ch] - gc_wide[ch]))
            dec = jnp.concatenate(
                [jnp.broadcast_to(jnp.exp(g_tot[ch][:, h * LANES:(h + 1) * LANES]), (DN_KEY_DIM, LANES))
                 for h in range(DN_HEADS)], axis=0)
            s_scr[ch[0], ch[1]] = s_old[ch] * dec + _dot_tn(block_wide(k_dec), v_new)
            o_scr = ob_scr if ch[1] == 1 else of_scr
            for h in range(DN_HEADS):
                o_scr[rows[ch], h * LANES:(h + 1) * LANES] = o[h * CHUNK:(h + 1) * CHUNK]

    def body(c, carry):
        chunk_step(c)
        return carry

    lax.fori_loop(0, n_chunks, body, 0)

    if emit_s:
        if n_prev:
            sfin_ref[:, 0:n_prev] = prev_ref[...]
        sfin_ref[:, n_prev] = s_scr[...]
    for h in range(DN_HEADS):
        cols = slice(h * LANES, (h + 1) * LANES)
        o = of_scr[:, cols] + ob_scr[:, cols]
        o = o * lax.rsqrt(jnp.mean(o * o, axis=-1, keepdims=True) + NORM_EPS) * gain_ref[...]
        o_ref[:, cols] = (o * _silu(dz_ref[:, cols].astype(F32))).astype(o_ref.dtype)


def _delta(dqkv, dz, small, a_log, dt_bias, out_gain, *, n_batch, n, tok0, s0=None, emit_s=False,
           prev=None):
    n_seq = max(DN_MIN_SEQ_PER_STEP, DN_ROWS_PER_STEP // n)
    rows = n_seq * n
    b0 = tok0 // rows
    n_hd = 2 * DN_HEADS
    tokb = lambda b: (b0 + b, 0)
    const = lambda b: (0, 0)
    const3 = lambda b: (0, 0, 0)
    pcol = jnp.zeros((2, LANES), F32)
    pcol = pcol.at[0, n_hd:2 * n_hd].set(a_log.reshape(-1)).at[1, n_hd:2 * n_hd].set(dt_bias.reshape(-1))
    expand = np.zeros((2, LANES, DN_EXPAND_W), np.float32)
    for d in range(2):
        for h in range(DN_HEADS):
            for src in (d * DN_HEADS + h, n_hd + d * DN_HEADS + h):
                expand[d, src, h * CHUNK:(h + 1) * CHUNK] = 1.0
                expand[d, src, DN_CAT + h * LANES:DN_CAT + (h + 1) * LANES] = 1.0
    expand = np.tile(expand, (1, 3, 1))
    tri = np.tril(np.ones((CHUNK, CHUNK), np.float32))
    eye = np.eye(CHUNK, dtype=np.float32)
    zero = np.zeros((CHUNK, CHUNK), np.float32)
    cum = np.stack([np.block([[t, zero] * 3, [zero, eye] * 3]) for t in (tri, tri.T)])
    args = [dqkv, dz, small, pcol, out_gain.reshape(1, LANES), jnp.asarray(expand, BF16),
            jnp.asarray(cum, BF16)]
    in_specs = [
        pl.BlockSpec((rows, DN_CONV_DIM), tokb),
        pl.BlockSpec((rows, DN_V_DIM), tokb),
        pl.BlockSpec((rows, LANES), tokb),
        pl.BlockSpec((2, LANES), const),
        pl.BlockSpec((1, LANES), const),
        pl.BlockSpec((2, 3 * LANES, DN_EXPAND_W), const3),
        pl.BlockSpec((2, 2 * CHUNK, 6 * CHUNK), const3),
    ]
    state_shape = (2, DN_HEADS * DN_KEY_DIM, DN_VAL_DIM)
    state_spec = pl.BlockSpec((n_seq,) + state_shape, lambda b: (b, 0, 0, 0))
    if s0 is not None:
        args.append(s0.reshape((n_batch,) + state_shape))
        in_specs.append(state_spec)
    n_prev = 0 if prev is None else prev.shape[1]
    stacked_spec = lambda k: pl.BlockSpec((n_seq, k) + state_shape, lambda b: (b, 0, 0, 0, 0))
    if n_prev:
        args.append(prev)
        in_specs.append(stacked_spec(n_prev))
    out_shape = [jax.ShapeDtypeStruct((n_batch * n, DN_V_DIM), BF16)]
    out_specs = [pl.BlockSpec((rows, DN_V_DIM), lambda b: (b, 0))]
    if emit_s:
        out_shape.append(jax.ShapeDtypeStruct((n_batch, n_prev + 1) + state_shape, F32))
        out_specs.append(stacked_spec(n_prev + 1))
    return pl.pallas_call(
        functools.partial(_delta_kernel, n=n, n_seq=n_seq, has_s0=s0 is not None, emit_s=emit_s, n_prev=n_prev),
        grid=(n_batch // n_seq,),
        in_specs=in_specs,
        out_specs=out_specs,
        out_shape=out_shape,
        scratch_shapes=[
            pltpu.VMEM((rows, DN_QK_DIM), F32), pltpu.VMEM((rows, DN_QK_DIM), F32),
            pltpu.VMEM((rows, DN_V_DIM), F32),
            pltpu.VMEM((rows, LANES), F32), pltpu.VMEM((rows, LANES), F32),
            pltpu.VMEM((rows, DN_V_DIM), F32), pltpu.VMEM((rows, DN_V_DIM), F32),
            pltpu.VMEM((n_seq,) + state_shape, F32),
        ],
        compiler_params=_cparams(1),
        name="delta_latent" if s0 is not None else "delta_context",
    )(*args)


SSD_CHUNK = 256
SSM_GROUP_W = SSM_INNER // SSM_GROUPS
HEADS_PER_TILE = LANES // SSM_HEAD_DIM


def _ssd_kernel(*refs, n, has_s0, emit_s, n_prev):
    it = iter(refs)
    z_ref, xbc_ref, dtc_ref, pcol_ref, dskip_ref, gain_ref = (next(it) for _ in range(6))
    if has_s0:
        s0_ref = next(it)
    if n_prev:
        prev_ref = next(it)
    o_ref = next(it)
    if emit_s:
        sfin_ref = next(it)
    y_scr, dt_scr, st_scr = (next(it) for _ in range(3))

    q_len = SSD_CHUNK
    n_chunks = n // q_len
    carry_state = has_s0 or n_chunks > 1
    row_tile = 128

    def clear(r, carry):
        y_scr[pl.ds(pl.multiple_of(r * row_tile, row_tile), row_tile), :] = jnp.zeros((row_tile, SSM_INNER), F32)
        return carry

    lax.fori_loop(0, n // row_tile, clear, 0)
    pcol = pcol_ref[...]
    dt_scr[...] = _softplus(dtc_ref[...] + pcol[1:2, :])
    neg_a_col = -jnp.exp(pcol[0:1, :])
    for d in range(2):
        for g in range(SSM_GROUPS):
            gs = slice(g * SSM_GROUP_W, (g + 1) * SSM_GROUP_W)
            if has_s0:
                st_scr[d, :, gs] = s0_ref[d, gs, :].T
            else:
                st_scr[d, :, gs] = jnp.zeros((SSM_STATE, SSM_GROUP_W), F32)

    ri = lax.broadcasted_iota(jnp.int32, (q_len, q_len), 0)
    ci = lax.broadcasted_iota(jnp.int32, (q_len, q_len), 1)
    tri3_b = (_rep_lanes((ri >= ci).astype(BF16), 3), _rep_lanes((ri <= ci).astype(BF16), 3))
    low = lax.broadcasted_iota(jnp.int32, (1, LANES), 1) < SSM_HEAD_DIM
    heads_per_group = SSM_HEADS // SSM_GROUPS

    def decays(c, d):
        upper = d == 1
        rows = pl.ds(pl.multiple_of(c * q_len, q_len), q_len)
        dt = dt_scr[rows, :]
        acum = _dot(tri3_b[d], jnp.concatenate(_split3(dt * neg_a_col), axis=0))
        a_tot = acum[0:1, :] if upper else acum[q_len - 1:q_len, :]
        w_col = dt * jnp.exp(a_tot - acum)
        return dict(rows=rows, acum=acum, acum_r=acum.T, dt_r=dt.T, w_r=w_col.T,
                    ea_tot=jnp.exp(a_tot))

    half = q_len // 2
    assert half == LANES
    rh =lax.broadcasted_iota(jnp.int32, (half, half), 0)
    ch = lax.broadcasted_iota(jnp.int32, (half, half), 1)
    incl_half = (rh >= ch, rh <= ch)

    def decay_weights(pd, l, cb, upper):
        col = jnp.broadcast_to(pd["acum"][:, l:l + 1], (q_len, half))
        row = pd["acum_r"][l:l + 1, :]
        dt_row = pd["dt_r"][l:l + 1, :]
        first, second = slice(0, half), slice(half, q_len)

        def quarter(r, c, masked):
            diff = col[r] - row[:, c]
            if masked:
                diff = jnp.where(incl_half[1 if upper else 0], diff, NEG_BIG)
            return cb[r, c] * (jnp.exp(diff) * dt_row[:, c])

        zero = jnp.zeros((half, half), F32)
        if upper:
            top = [quarter(first, first, True), quarter(first, second, False)]
            bottom = [zero, quarter(second, second, True)]
        else:
            top = [quarter(first, first, True), zero]
            bottom = [quarter(second, first, False), quarter(second, second, True)]
        return jnp.concatenate([jnp.concatenate(top, axis=1), jnp.concatenate(bottom, axis=1)], axis=0), col

    def chunk_step(c_fwd, c_bwd, same_chunk):
        pre = (decays(c_fwd, 0), decays(c_bwd, 1))
        for g in range(SSM_GROUPS):
            shared = None
            for d in range(2):
                pd = pre[d]
                rows = pd["rows"]
                if shared is None or not same_chunk:
                    b0 = SSM_INNER + g * SSM_STATE
                    bm_g = xbc_ref[rows, b0:b0 + SSM_STATE]
                    cm_g = xbc_ref[rows, b0 + SSM_BC_DIM:b0 + SSM_BC_DIM + SSM_STATE]
                    shared = (cm_g, _dot_nt(cm_g, bm_g), bm_g.astype(F32).T)
                cm_g, cb, bm_t = shared
                for tt in range(heads_per_group // HEADS_PER_TILE):
                    h0 = g * heads_per_group + tt * HEADS_PER_TILE
                    ls = slice(g * SSM_GROUP_W + tt * LANES, g * SSM_GROUP_W + (tt + 1) * LANES)
                    w_parts, b_parts, acum_cols = [], [], []
                    for hh in range(HEADS_PER_TILE):
                        l = d * SSM_HEADS + h0 + hh
                        w_h, acum_col = decay_weights(pd, l, cb, d == 1)
                        w_parts.append(w_h)
                        acum_cols.append(acum_col)
                        b_parts.append(bm_t * pd["w_r"][l:l + 1, :])
                    lhs = jnp.concatenate([jnp.concatenate(w_parts, axis=1),
                                           jnp.concatenate(b_parts, axis=1)], axis=0).astype(BF16)
                    xt = xbc_ref[rows, ls]
                    zero = jnp.zeros_like(xt)
                    bd = jnp.concatenate([jnp.where(low, xt, zero), jnp.where(low, zero, xt)], axis=0)
                    res = _dot(lhs, bd)
                    y_new = y_scr[rows, ls] + res[0:q_len]
                    st_inc = res[q_len:q_len + SSM_STATE]
                    if carry_state:
                        st = st_scr[d, :, ls]
                        l0 = d * SSM_HEADS + h0
                        ea_tot = pd["ea_tot"]
                        scale = jnp.exp(jnp.where(low, acum_cols[0], acum_cols[1]))
                        y_new = y_new + _dot(cm_g, st.astype(BF16)) * scale
                        st_inc = st * jnp.where(low, ea_tot[:, l0:l0 + 1], ea_tot[:, l0 + 1:l0 + 2]) + st_inc
                    y_scr[rows, ls] = y_new
                    st_scr[d, :, ls] = st_inc

    if n_chunks == 1:
        chunk_step(0, 0, True)
    else:
        def body(c, carry):
            chunk_step(c, n_chunks - 1 - c, False)
            return carry

        lax.fori_loop(0, n_chunks, body, 0)

    if emit_s:
        if n_prev:
            sfin_ref[0:n_prev] = prev_ref[...]
        for d in range(2):
            for g in range(SSM_GROUPS):
                gs = slice(g * SSM_GROUP_W, (g + 1) * SSM_GROUP_W)
                sfin_ref[n_prev, d, gs, :] = st_scr[d, :, gs].T
    def finish(r, carry):
        rows = pl.ds(pl.multiple_of(r * row_tile, row_tile), row_tile)
        y = y_scr[rows, :] + xbc_ref[rows, 0:SSM_INNER].astype(F32) * dskip_ref[...]
        y = y * _silu(z_ref[rows, :].astype(F32))
        y = y * lax.rsqrt(jnp.mean(y * y, axis=-1, keepdims=True) + NORM_EPS) * gain_ref[...]
        o_ref[rows, :] = y.astype(o_ref.dtype)
        return carry

    lax.fori_loop(0, n // row_tile, finish, 0)


def _ssd(z, xbc, dt_raw, a_log, dt_bias, d_skip, out_gain, *, n_batch, n, tok0,
         s0=None, emit_s=False, prev=None):
    b0 = tok0 // n
    tokb = lambda b: (b0 + b, 0)
    const = lambda b: (0, 0)
    pcol = jnp.zeros((2, LANES), F32)
    pcol = pcol.at[0, 0:2 * SSM_HEADS].set(a_log.reshape(-1)).at[1, 0:2 * SSM_HEADS].set(dt_bias.reshape(-1))
    args = [z, xbc, dt_raw, pcol,
            jnp.repeat(d_skip, SSM_HEAD_DIM).reshape(1, SSM_INNER), out_gain.reshape(1, SSM_INNER)]
    big = dict(pipeline_mode=pl.Buffered(1)) if n > SEQ else {}
    in_specs = [
        pl.BlockSpec((n, SSM_INNER), tokb, **big),
        pl.BlockSpec((n, SSM_CONV_DIM), tokb, **big),
        pl.BlockSpec((n, LANES), tokb),
        pl.BlockSpec((2, LANES), const),
        pl.BlockSpec((1, SSM_INNER), const),
        pl.BlockSpec((1, SSM_INNER), const),
    ]
    state_spec = pl.BlockSpec((None, 2, SSM_INNER, SSM_STATE), lambda b: (b, 0, 0, 0))
    if s0 is not None:
        args.append(s0.reshape(n_batch, 2, SSM_INNER, SSM_STATE))
        in_specs.append(state_spec)
    n_prev = 0 if prev is None else prev.shape[1]
    stacked_spec = lambda k: pl.BlockSpec((None, k, 2, SSM_INNER, SSM_STATE), lambda b: (b, 0, 0, 0, 0))
    if n_prev:
        args.append(prev)
        in_specs.append(stacked_spec(n_prev))
    out_shape = [jax.ShapeDtypeStruct((n_batch * n, SSM_INNER), BF16)]
    out_specs = [pl.BlockSpec((n, SSM_INNER), lambda b: (b, 0))]
    if emit_s:
        out_shape.append(jax.ShapeDtypeStruct((n_batch, n_prev + 1, 2, SSM_INNER, SSM_STATE), F32))
        out_specs.append(stacked_spec(n_prev + 1))
    return pl.pallas_call(
        functools.partial(_ssd_kernel, n=n, has_s0=s0 is not None, emit_s=emit_s, n_prev=n_prev),
        grid=(n_batch,),
        in_specs=in_specs,
        out_specs=out_specs,
        out_shape=out_shape,
        scratch_shapes=[
            pltpu.VMEM((n, SSM_INNER), F32), pltpu.VMEM((n, LANES), F32),
            pltpu.VMEM((2, SSM_STATE, SSM_INNER), F32),
        ],
        compiler_params=_cparams(1),
        name="ssd_latent" if s0 is not None else "ssd_context",
    )(*args)


EVEN_MAIN = ATT_Q_DIM + 2 * ATT_KV_DIM + DN_CONV_DIM + DN_V_DIM
EVEN_SPLITS = ((0, ATT_Q_DIM), (ATT_Q_DIM, ATT_Q_DIM + ATT_KV_DIM),
               (ATT_Q_DIM + ATT_KV_DIM, ATT_Q_DIM + 2 * ATT_KV_DIM),
               (ATT_Q_DIM + 2 * ATT_KV_DIM, ATT_Q_DIM + 2 * ATT_KV_DIM + DN_CONV_DIM),
               (ATT_Q_DIM + 2 * ATT_KV_DIM + DN_CONV_DIM, EVEN_MAIN))
EVEN_DTYPES = (F32, F32, F32, BF16, BF16)
EVEN_CONV_OUT = 3
ODD_MAIN = SSM_INNER + SSM_CONV_DIM
ODD_SPLITS = ((0, SSM_INNER), (SSM_INNER, ODD_MAIN))
ODD_DTYPES = (BF16, BF16)
ODD_CONV_OUT = 1


def _tail_cols(w, start):
    tail = w[:, :, start:]
    return jnp.pad(tail, ((0, 0), (0, 0), (0, LANES - tail.shape[2]))).astype(BF16)


def kernel(x_prompt, x_sample, c, cache_attn_k, cache_attn_v, state_delta, state_ssm, c_ctx, norm_mix_g, norm_mlp_g, w_mod, b_mod, w_mlp_in, w_mlp_out, w_in_even, attn_q_norm_g, attn_k_norm_g, delta_conv_w, delta_a_log, delta_dt_bias, delta_norm_g, w_out_even, w_in_odd, ssm_conv_w, ssm_conv_b, ssm_a_log, ssm_dt_bias, ssm_d, ssm_norm_g, w_out_odd, final_norm_g):
    x = (x_prompt.reshape(N_PROMPT_TOK, D_MODEL), x_sample.reshape(N_SAMPLE_TOK, D_MODEL))
    cond = jnp.zeros((N_COND, D_MODEL), F32).at[0].set(c_ctx).at[1:1 + DEC_BATCH].set(c)
    mod = _modulation(cond, w_mod, b_mod).reshape(DEPTH, N_COND, 1, 6 * D_MODEL)
    w_in_even_b = w_in_even.astype(BF16)
    w_in_odd_b = w_in_odd.astype(BF16)
    w_out_even_b, w_out_odd_b = _cast_bf16(w_out_even), _cast_bf16(w_out_odd)
    w_mlp_in_b, w_mlp_out_b = _cast_bf16(w_mlp_in), _cast_bf16(w_mlp_out)
    tail_even, tail_odd = _tail_cols(w_in_even, EVEN_MAIN), _tail_cols(w_in_odd, ODD_MAIN)

    ks, vs = [], []
    s_delta = s_ssm = None
    for layer in range(DEPTH):
        j = layer // 2
        if layer % 2 == 0:
            q, k, v, dqkv, dz, small = _inproj(x, norm_mix_g[layer], mod[layer], w_in_even_b, j, tail_even[j],
                                               EVEN_SPLITS, EVEN_DTYPES, EVEN_CONV_OUT, delta_conv_w[j],
                                               jnp.zeros((DN_CONV_DIM,), F32), "inproj_even")
            ctx_k = cache_attn_k[:, j].reshape(DEC_BATCH, PAST_LEN, ATT_KV_DIM)
            ctx_v = cache_attn_v[:, j].reshape(DEC_BATCH, PAST_LEN, ATT_KV_DIM)
            o_att_p, k_norm = _attention(q, k, v, attn_q_norm_g[j], attn_k_norm_g[j],
                                         n_batch=BATCH, n=SEQ, tok0=0, emit_k=True)
            (o_att_s,) = _attention(q, k, v, attn_q_norm_g[j], attn_k_norm_g[j],
                                    n_batch=DEC_BATCH, n=DEC_SEQ, tok0=N_PROMPT_TOK, ctx=(ctx_k, ctx_v))
            dn_args = (dqkv, dz, small, delta_a_log[j], delta_dt_bias[j], delta_norm_g[j])
            o_dn_p, s_delta = _delta(*dn_args, n_batch=BATCH, n=SEQ, tok0=0, emit_s=True, prev=s_delta)
            (o_dn_s,) = _delta(*dn_args, n_batch=DEC_BATCH, n=DEC_SEQ, tok0=N_PROMPT_TOK, s0=state_delta[:, j])
            ys = [(o_att_p, o_att_s), (o_dn_p, o_dn_s)]
            w_out_b = w_out_even_b
            ks.append(k_norm.reshape(BATCH, SEQ, ATT_KV_HEADS, ATT_HEAD_DIM))
            vs.append(v[:N_PROMPT_TOK].reshape(BATCH, SEQ, ATT_KV_HEADS, ATT_HEAD_DIM))
        else:
            z, xbc, dt_raw = _inproj(x, norm_mix_g[layer], mod[layer], w_in_odd_b, j, tail_odd[j],
                                     ODD_SPLITS, ODD_DTYPES, ODD_CONV_OUT, ssm_conv_w[j], ssm_conv_b[j],
                                     "inproj_odd")
            ssd_args = (z, xbc, dt_raw, ssm_a_log[j], ssm_dt_bias[j], ssm_d[j], ssm_norm_g[j])
            y_p, s_ssm = _ssd(*ssd_args, n_batch=BATCH, n=SEQ, tok0=0, emit_s=True, prev=s_ssm)
            (y_s,) = _ssd(*ssd_args, n_batch=DEC_BATCH, n=DEC_SEQ, tok0=N_PROMPT_TOK, s0=state_ssm[:, j])
            ys = [(y_p, y_s)]
            w_out_b = w_out_odd_b
        x = _outproj_mlp(x, ys, w_out_b, j, norm_mlp_g[layer], mod[layer], w_mlp_in_b, w_mlp_out_b, layer,
                         "outproj_mlp_even" if layer % 2 == 0 else "outproj_mlp_odd")

    y_prompt = _final_norm(x, final_norm_g, 0, N_PROMPT_TOK).reshape(BATCH, SEQ, D_MODEL)
    y_sample = _final_norm(x, final_norm_g, N_PROMPT_TOK, N_SAMPLE_TOK).reshape(DEC_BATCH, DEC_SEQ, D_MODEL)
    return (y_prompt, y_sample, jnp.stack(ks, axis=1), jnp.stack(vs, axis=1),
            s_delta.reshape(BATCH, N_EVEN, 2, DN_HEADS, DN_KEY_DIM, DN_VAL_DIM),
            s_ssm.reshape(BATCH, N_ODD, 2, SSM_HEADS, SSM_HEAD_DIM, SSM_STATE))
```
